```python
import math
import jax, jax.numpy as jnp
from jax import lax
import numpy as np

D_MODEL = 1024
BATCH = 8
SEQ = 2048
DEPTH = 2
DEC_BATCH = 128
DEC_SEQ = 4
PAST_LEN = 2048
PAGE_SIZE = 128

D_CONV = 512
CONV_WIDTH = 31
D_SSM = 512
SSM_GROUP = 16
N_SSM_GROUPS = D_SSM // SSM_GROUP
SSM_STATE = 64
DT_MIN = 0.001
DT_MAX = 0.1
HEAD_DIM = 64
HEADS_PER_GROUP = 4
ATT_PATTERNS = ((128, 1), (512, 4), (2048, 16))
N_ATT_HEADS = HEADS_PER_GROUP * len(ATT_PATTERNS)
D_ATT = N_ATT_HEADS * HEAD_DIM
D_ATT_OUT = HEADS_PER_GROUP * HEAD_DIM
ATT_SCALE = HEAD_DIM ** -0.5
D_FF = -(-8 * D_MODEL // (3 * 256)) * 256
N_IN = 2 * D_CONV + D_SSM + 3 * D_ATT + 3 * D_MODEL
RMS_EPS = 1e-6
LN_EPS = 1e-5
NEG = -1e30

kernel_name = 'hybrid_conv_s5_dilated_attn_decoder_step'


def rmsnorm(x, g):
    xf = x.astype(jnp.float32)
    y = xf * lax.rsqrt(jnp.mean(xf * xf, axis=-1, keepdims=True) + RMS_EPS)
    return (y * g.astype(jnp.float32)).astype(x.dtype)


def layernorm(x, g, b):
    xf = x.astype(jnp.float32)
    xc = xf - jnp.mean(xf, axis=-1, keepdims=True)
    y = xc * lax.rsqrt(jnp.mean(xc * xc, axis=-1, keepdims=True) + LN_EPS)
    return (y * g.astype(jnp.float32) + b.astype(jnp.float32)).astype(x.dtype)


def alibi_slopes():
    return jnp.exp2(-8.0 * jnp.arange(1, N_ATT_HEADS + 1, dtype=jnp.float32) / N_ATT_HEADS)


def conv_branch(a, prev, w_dw, b_dw, ln_g, ln_b, w_proj):
    u = a[..., :D_CONV] * jax.nn.sigmoid(a[..., D_CONV:])
    ucat = jnp.concatenate([prev.astype(u.dtype), u], axis=1)
    y = lax.conv_general_dilated(ucat, w_dw[:, None, :].astype(u.dtype), (1,), 'VALID',
                                 dimension_numbers=('NWC', 'WIO', 'NWC'),
                                 feature_group_count=D_CONV)
    y = jax.nn.silu(layernorm(y + b_dw, ln_g, ln_b))
    return y @ w_proj, ucat[:, ucat.shape[1] - (CONV_WIDTH - 1):]


def _complex_scan_combine(e1, e2):
    a1r, a1i, b1r, b1i = e1
    a2r, a2i, b2r, b2i = e2
    return (a2r * a1r - a2i * a1i, a2r * a1i + a2i * a1r,
            a2r * b1r - a2i * b1i + b2r, a2r * b1i + a2i * b1r + b2i)


def ssm_branch(u, h0_re, h0_im, a_re, a_im, log_dt, b_re, b_im, c_re, c_im, d_skip, w_glu):
    f32 = jnp.float32
    nb, T, _ = u.shape
    uf = u.astype(f32).reshape(nb, T, N_SSM_GROUPS, SSM_GROUP)
    ar, ai = a_re.astype(f32), a_im.astype(f32)
    dt = jnp.exp(log_dt.astype(f32))[:, None]
    mag = jnp.exp(dt * ar)
    abar_r, abar_i = mag * jnp.cos(dt * ai), mag * jnp.sin(dt * ai)
    den = ar * ar + ai * ai
    fr = ((abar_r - 1.0) * ar + abar_i * ai) / den
    fi = (abar_i * ar - (abar_r - 1.0) * ai) / den
    br, bi = b_re.astype(f32), b_im.astype(f32)
    bbr = fr[..., None] * br - fi[..., None] * bi
    bbi = fr[..., None] * bi + fi[..., None] * br
    bur = jnp.einsum('btgc,gpc->btgp', uf, bbr)
    bui = jnp.einsum('btgc,gpc->btgp', uf, bbi)
    shp = bur.shape
    acr, aci, hr, hi = lax.associative_scan(
        _complex_scan_combine,
        (jnp.broadcast_to(abar_r, shp), jnp.broadcast_to(abar_i, shp), bur, bui), axis=1)
    h0r = h0_re.astype(f32)[:, None]
    h0i = h0_im.astype(f32)[:, None]
    hr = hr + acr * h0r - aci * h0i
    hi = hi + acr * h0i + aci * h0r
    y = (jnp.einsum('btgp,gcp->btgc', hr, c_re.astype(f32))
         - jnp.einsum('btgp,gcp->btgc', hi, c_im.astype(f32))
         + d_skip.astype(f32) * uf)
    z = jax.nn.gelu(y.reshape(nb, T, D_SSM)).astype(u.dtype)
    g = z @ w_glu
    out = g[..., :D_MODEL] * jax.nn.sigmoid(g[..., D_MODEL:])
    return out, hr[:, -1], hi[:, -1]


def dilated_attention_prompt(q, k, v, window, dil, slopes):
    f32 = jnp.float32
    nb_, S, H, dh = q.shape
    R = window // dil
    Ls = S // dil
    nblk = -(-Ls // R)
    pad_end = nblk * R - Ls

    def by_residue(t):
        return t.astype(f32).reshape(nb_, Ls, dil, H, dh).transpose(0, 2, 1, 3, 4)

    qb = jnp.pad(by_residue(q), ((0, 0), (0, 0), (0, pad_end), (0, 0), (0, 0)))
    qb = qb.reshape(nb_, dil, nblk, R, H, dh)

    def key_blocks(t):
        ts = jnp.pad(by_residue(t), ((0, 0), (0, 0), (R, pad_end), (0, 0), (0, 0)))
        ts = ts.reshape(nb_, dil, nblk + 1, R, H, dh)
        return jnp.concatenate([ts[:, :, :-1], ts[:, :, 1:]], axis=3)

    kb, vb = key_blocks(k), key_blocks(v)
    s = jnp.einsum('brnqhc,brnkhc->brnhqk', qb, kb) * ATT_SCALE
    qi = jnp.arange(R)[:, None]
    ki = jnp.arange(2 * R)[None, :]
    dist = qi + R - ki
    kpos = jnp.arange(nblk)[:, None, None] * R + ki[None] - R
    valid = (dist >= 0) & (dist <= R) & (kpos >= 0)
    bias = -slopes[:, None, None] * (dil * dist).astype(f32)[None]
    s = jnp.where(valid[None, None, :, None], s + bias[None, None, None], NEG)
    mx = jnp.max(s, axis=-1, keepdims=True)
    p = jnp.exp(s - mx)
    den = jnp.sum(p, axis=-1)
    o = jnp.einsum('brnhqk,brnkhc->brnqhc', p, vb) / den.transpose(0, 1, 2, 4, 3)[..., None]
    lse = (mx[..., 0] + jnp.log(den)).transpose(0, 1, 2, 4, 3)
    o = o.reshape(nb_, dil, nblk * R, H, dh)[:, :, :Ls].transpose(0, 2, 1, 3, 4).reshape(nb_, S, H, dh)
    lse = lse.reshape(nb_, dil, nblk * R, H)[:, :, :Ls].transpose(0, 2, 1, 3).reshape(nb_, S, H)
    return o, lse


def dilated_attention_sample(q, k_new, v_new, k_buf, v_buf, window, dil, slopes):
    f32 = jnp.float32
    L = k_buf.shape[1]
    T = q.shape[1]
    R = window // dil
    kcat = jnp.concatenate([k_buf.astype(f32), k_new.astype(f32)], axis=1)
    vcat = jnp.concatenate([v_buf.astype(f32), v_new.astype(f32)], axis=1)
    j = jnp.arange(T)[:, None]
    kk = jnp.arange(R + 1)[None, :]
    idx = L + j - kk * dil
    valid = idx >= 0
    idx = jnp.maximum(idx, 0)
    kg = kcat[:, idx]
    vg = vcat[:, idx]
    s = jnp.einsum('bthc,btkhc->bhtk', q.astype(f32), kg) * ATT_SCALE
    bias = -slopes[:, None, None] * (kk * dil).astype(f32)[None]
    s = jnp.where(valid[None, None], s + bias[None], NEG)
    mx = jnp.max(s, axis=-1, keepdims=True)
    p = jnp.exp(s - mx)
    den = jnp.sum(p, axis=-1)
    o = jnp.einsum('bhtk,btkhc->bthc', p, vg) / den.transpose(0, 2, 1)[..., None]
    lse = (mx[..., 0] + jnp.log(den)).transpose(0, 2, 1)
    return o, lse


def trunk_layer(x, c, past, p, is_prompt):
    conv_prev, h0_re, h0_im, k_bufs, v_bufs = past
    nb, T, _ = x.shape
    mod = (jax.nn.silu(c) @ p['w_mod'] + p['b_mod'])[:, None, :]
    sh_m, sc_m, g_m, sh_f, sc_f, g_f = jnp.split(mod, 6, axis=-1)
    h = rmsnorm(x, p['g_pre_mix']) * (1.0 + sc_m) + sh_m
    z = h @ p['w_in']
    o1 = 2 * D_CONV
    o2 = o1 + D_SSM
    o3 = o2 + D_ATT
    o4 = o3 + D_ATT
    o5 = o4 + D_ATT
    a_conv, u_ssm = z[..., :o1], z[..., o1:o2]
    q, k, v, gates = z[..., o2:o3], z[..., o3:o4], z[..., o4:o5], z[..., o5:]
    conv_out, conv_state = conv_branch(a_conv, conv_prev, p['conv_w'], p['conv_b'],
                                       p['conv_ln_g'], p['conv_ln_b'], p['w_conv_out'])
    ssm_out, s_re, s_im = ssm_branch(u_ssm, h0_re, h0_im, p['ssm_a_re'], p['ssm_a_im'],
                                     p['ssm_log_dt'], p['ssm_b_re'], p['ssm_b_im'],
                                     p['ssm_c_re'], p['ssm_c_im'], p['ssm_d'], p['w_ssm_glu'])
    hshape = (nb, T, N_ATT_HEADS, HEAD_DIM)
    q, k, v = q.reshape(hshape), k.reshape(hshape), v.reshape(hshape)
    slopes = alibi_slopes()
    outs, lses, kv_new = [], [], []
    for g, (window, dil) in enumerate(ATT_PATTERNS):
        sl = slice(g * HEADS_PER_GROUP, (g + 1) * HEADS_PER_GROUP)
        qg, kg, vg = q[:, :, sl], k[:, :, sl], v[:, :, sl]
        if is_prompt:
            o, lse = dilated_attention_prompt(qg, kg, vg, window, dil, slopes[sl])
            keep = min(window, T)
            kv_new += [kg[:, T - keep:], vg[:, T - keep:]]
        else:
            o, lse = dilated_attention_sample(qg, kg, vg, k_bufs[g], v_bufs[g], window, dil, slopes[sl])
            kv_new += [kg, vg]
        outs.append(o)
        lses.append(lse)
    alpha = jax.nn.softmax(jnp.stack(lses), axis=0)
    att = jnp.sum(alpha[..., None] * jnp.stack(outs), axis=0).reshape(nb, T, D_ATT_OUT).astype(x.dtype)
    att_out = att @ p['w_att']
    g_a, g_b, g_c = jnp.split(gates, 3, axis=-1)
    merged = (jax.nn.sigmoid(g_a) * conv_out + jax.nn.sigmoid(g_b) * ssm_out
              + jax.nn.sigmoid(g_c) * att_out)
    x = x + g_m * rmsnorm(merged @ p['w_out'], p['g_post_mix'])
    hf = rmsnorm(x, p['g_pre_ffn']) * (1.0 + sc_f) + sh_f
    gu = hf @ p['w_ffn_in']
    f = jax.nn.silu(gu[..., :D_FF]) * gu[..., D_FF:]
    x = x + g_f * rmsnorm(f @ p['w_ffn_out'], p['g_post_ffn'])
    return x, (*kv_new, conv_state, s_re.astype(x.dtype), s_im.astype(x.dtype))


def setup_inputs(seed: int = 0) -> dict:
    key = jax.random.key(seed)
    ks = iter(jax.random.split(key, 64))

    def nrm(shape, scale):
        return jax.random.normal(next(ks), shape, jnp.float32) * scale

    inp = {}
    inp['x_prompt'] = nrm((BATCH, SEQ, D_MODEL), 1.0)
    inp['x_sample'] = nrm((DEC_BATCH, DEC_SEQ, D_MODEL), 1.0)
    for g, (window, dil) in enumerate(ATT_PATTERNS):
        L = min(window, PAST_LEN)
        inp['cache_k%d' % g] = nrm((DEPTH, DEC_BATCH, L, HEADS_PER_GROUP, HEAD_DIM), 1.0)
        inp['cache_v%d' % g] = nrm((DEPTH, DEC_BATCH, L, HEADS_PER_GROUP, HEAD_DIM), 1.0)
    inp['state_conv'] = nrm((DEPTH, DEC_BATCH, CONV_WIDTH - 1, D_CONV), 0.5)
    inp['state_ssm_re'] = nrm((DEPTH, DEC_BATCH, N_SSM_GROUPS, SSM_STATE), 0.1)
    inp['state_ssm_im'] = nrm((DEPTH, DEC_BATCH, N_SSM_GROUPS, SSM_STATE), 0.1)
    inp['c_prompt'] = nrm((BATCH, D_MODEL), 1.0)
    inp['c_sample'] = nrm((DEC_BATCH, D_MODEL), 1.0)
    inp['w_mod'] = nrm((DEPTH, D_MODEL, 6 * D_MODEL), 0.5 * D_MODEL ** -0.5)
    inp['b_mod'] = nrm((DEPTH, 6 * D_MODEL), 0.02)
    inp['g_pre_mix'] = 1.0 + nrm((DEPTH, D_MODEL), 0.02)
    inp['g_post_mix'] = 1.0 + nrm((DEPTH, D_MODEL), 0.02)
    inp['g_pre_ffn'] = 1.0 + nrm((DEPTH, D_MODEL), 0.02)
    inp['g_post_ffn'] = 1.0 + nrm((DEPTH, D_MODEL), 0.02)
    inp['w_in'] = nrm((DEPTH, D_MODEL, N_IN), D_MODEL ** -0.5)
    inp['conv_w'] = nrm((DEPTH, CONV_WIDTH, D_CONV), CONV_WIDTH ** -0.5)
    inp['conv_b'] = nrm((DEPTH, D_CONV), 0.02)
    inp['conv_ln_g'] = 1.0 + nrm((DEPTH, D_CONV), 0.02)
    inp['conv_ln_b'] = nrm((DEPTH, D_CONV), 0.02)
    inp['w_conv_out'] = nrm((DEPTH, D_CONV, D_MODEL), D_CONV ** -0.5)
    n_idx = jnp.arange(SSM_STATE, dtype=jnp.float32)
    inp['ssm_a_re'] = -0.5 + nrm((DEPTH, N_SSM_GROUPS, SSM_STATE), 0.01)
    inp['ssm_a_im'] = math.pi * n_idx + nrm((DEPTH, N_SSM_GROUPS, SSM_STATE), 0.01)
    inp['ssm_log_dt'] = jax.random.uniform(next(ks), (DEPTH, N_SSM_GROUPS), jnp.float32,
                                           math.log(DT_MIN), math.log(DT_MAX))
    inp['ssm_b_re'] = nrm((DEPTH, N_SSM_GROUPS, SSM_STATE, SSM_GROUP), (2 * SSM_GROUP) ** -0.5)
    inp['ssm_b_im'] = nrm((DEPTH, N_SSM_GROUPS, SSM_STATE, SSM_GROUP), (2 * SSM_GROUP) ** -0.5)
    inp['ssm_c_re'] = nrm((DEPTH, N_SSM_GROUPS, SSM_GROUP, SSM_STATE), SSM_STATE ** -0.5)
    inp['ssm_c_im'] = nrm((DEPTH, N_SSM_GROUPS, SSM_GROUP, SSM_STATE), SSM_STATE ** -0.5)
    inp['ssm_d'] = nrm((DEPTH, N_SSM_GROUPS, SSM_GROUP), 1.0)
    inp['w_ssm_glu'] = nrm((DEPTH, D_SSM, 2 * D_MODEL), D_SSM ** -0.5)
    inp['w_att'] = nrm((DEPTH, D_ATT_OUT, D_MODEL), D_ATT_OUT ** -0.5)
    inp['w_out'] = nrm((DEPTH, D_MODEL, D_MODEL), D_MODEL ** -0.5)
    inp['w_ffn_in'] = nrm((DEPTH, D_MODEL, 2 * D_FF), D_MODEL ** -0.5)
    inp['w_ffn_out'] = nrm((DEPTH, D_FF, D_MODEL), D_FF ** -0.5)
    return inp


def reference(x_prompt, x_sample, cache_k0, cache_v0, cache_k1, cache_v1, cache_k2, cache_v2,
              state_conv, state_ssm_re, state_ssm_im, c_prompt, c_sample,
              w_mod, b_mod, g_pre_mix, g_post_mix, g_pre_ffn, g_post_ffn, w_in,
              conv_w, conv_b, conv_ln_g, conv_ln_b, w_conv_out,
              ssm_a_re, ssm_a_im, ssm_log_dt, ssm_b_re, ssm_b_im, ssm_c_re, ssm_c_im, ssm_d, w_ssm_glu,
              w_att, w_out, w_ffn_in, w_ffn_out):
    caches_k = (cache_k0, cache_k1, cache_k2)
    caches_v = (cache_v0, cache_v1, cache_v2)
    xp, xs = x_prompt, x_sample
    nbp = xp.shape[0]
    prompt_states, sample_states = [], []
    for l in range(DEPTH):
        p = dict(w_mod=w_mod[l], b_mod=b_mod[l], g_pre_mix=g_pre_mix[l], g_post_mix=g_post_mix[l],
                 g_pre_ffn=g_pre_ffn[l], g_post_ffn=g_post_ffn[l], w_in=w_in[l],
                 conv_w=conv_w[l], conv_b=conv_b[l], conv_ln_g=conv_ln_g[l], conv_ln_b=conv_ln_b[l],
                 w_conv_out=w_conv_out[l], ssm_a_re=ssm_a_re[l], ssm_a_im=ssm_a_im[l],
                 ssm_log_dt=ssm_log_dt[l], ssm_b_re=ssm_b_re[l], ssm_b_im=ssm_b_im[l],
                 ssm_c_re=ssm_c_re[l], ssm_c_im=ssm_c_im[l], ssm_d=ssm_d[l], w_ssm_glu=w_ssm_glu[l],
                 w_att=w_att[l], w_out=w_out[l], w_ffn_in=w_ffn_in[l], w_ffn_out=w_ffn_out[l])
        past_p = (jnp.zeros((nbp, CONV_WIDTH - 1, D_CONV), xp.dtype),
                  jnp.zeros((nbp, N_SSM_GROUPS, SSM_STATE), jnp.float32),
                  jnp.zeros((nbp, N_SSM_GROUPS, SSM_STATE), jnp.float32), None, None)
        xp, st_p = trunk_layer(xp, c_prompt, past_p, p, True)
        past_s = (state_conv[l], state_ssm_re[l], state_ssm_im[l],
                  (caches_k[0][l], caches_k[1][l], caches_k[2][l]),
                  (caches_v[0][l], caches_v[1][l], caches_v[2][l]))
        xs, st_s = trunk_layer(xs, c_sample, past_s, p, False)
        prompt_states.append(st_p)
        sample_states.append(st_s)
    ps = [jnp.stack([s[i] for s in prompt_states]) for i in range(9)]
    ss = [jnp.stack([s[i] for s in sample_states]) for i in range(9)]
    return (xp, xs,
            ps[0], ps[1], ps[2], ps[3], ps[4], ps[5], ps[6], ps[7], ps[8],
            ss[0], ss[1], ss[2], ss[3], ss[4], ss[5], ss[6], ss[7], ss[8])
```

```python
import functools
import math

import numpy as np
import jax
import jax.numpy as jnp
from jax import lax
from jax.experimental import pallas as pl
from jax.experimental.pallas import tpu as pltpu

F32 = jnp.float32
BF16 = jnp.bfloat16

D_MODEL = 1024
DEPTH = 2
D_CONV = 512
CONV_WIDTH = 31
D_SSM = 512
SSM_GROUP = 16
N_SSM_GROUPS = 32
SSM_STATE = 64
HEAD_DIM = 64
HEADS_PER_GROUP = 4
ATT_PATTERNS = ((128, 1), (512, 4), (2048, 16))
N_ATT_HEADS = 12
D_ATT = 768
D_ATT_OUT = 256
ATT_SCALE = HEAD_DIM ** -0.5
D_FF = 2816
N_IN = 6912
RMS_EPS = 1e-6
LN_EPS = 1e-5
NEG = -1e30
BAND = 128

OFF_GATES = 0
OFF_CONV = 3072
OFF_SSM = 4096
OFF_Q = 4608
OFF_K = 5376
OFF_V = 6144

HALO = 32
N_STATE = N_SSM_GROUPS * SSM_STATE
OCTETS = 4
VMEM_LIMIT = 56 * 1024 * 1024


def _params(sem):
    return pltpu.CompilerParams(dimension_semantics=sem, vmem_limit_bytes=VMEM_LIMIT)


def _slopes():
    return [2.0 ** (-8.0 * (i + 1) / N_ATT_HEADS) for i in range(N_ATT_HEADS)]


def _rms(x, g):
    return x * lax.rsqrt(jnp.mean(x * x, axis=-1, keepdims=True) + RMS_EPS) * g


def _sigmoid(x):
    return 1.0 / (1.0 + jnp.exp(-x))


def _silu(x):
    return x * _sigmoid(x)


def _dot(a, b):
    return jnp.dot(a, b, preferred_element_type=F32)


def _mod_kernel(c_ref, w_ref, b_ref, o_ref):
    a = _silu(c_ref[...]).astype(BF16)
    o_ref[...] = _dot(a, w_ref[...].astype(BF16)) + b_ref[...]


def _modulation(c_all, w_mod, b_mod):
    nb = c_all.shape[0]
    tn = 512
    return pl.pallas_call(
        _mod_kernel,
        grid=(DEPTH, 6 * D_MODEL // tn),
        in_specs=[pl.BlockSpec((nb, D_MODEL), lambda l, j: (0, 0)),
                  pl.BlockSpec((None, D_MODEL, tn), lambda l, j: (l, 0, j)),
                  pl.BlockSpec((None, 1, tn), lambda l, j: (l, 0, j))],
        out_specs=pl.BlockSpec((None, nb, tn), lambda l, j: (l, 0, j)),
        out_shape=jax.ShapeDtypeStruct((DEPTH, nb, 6 * D_MODEL), F32),
        compiler_params=_params(("parallel", "parallel")),
        name="modulation",
    )(c_all, w_mod, b_mod.reshape(DEPTH, 1, 6 * D_MODEL))


def _ssm_disc_kernel(ar_ref, ai_ref, ldt_ref, br_ref, bi_ref, abr_ref, abi_ref, bbr_ref, bbi_ref):
    ar = ar_ref[...]
    ai = ai_ref[...]
    dt = jnp.exp(ldt_ref[...])
    mag = jnp.exp(dt * ar)
    abr = mag * jnp.cos(dt * ai)
    abi = mag * jnp.sin(dt * ai)
    den = ar * ar + ai * ai
    fr = ((abr - 1.0) * ar + abi * ai) / den
    fi = (abi * ar - (abr - 1.0) * ai) / den
    br = br_ref[...]
    bi = bi_ref[...]
    abr_ref[...] = abr
    abi_ref[...] = abi
    bbr_ref[...] = fr * br - fi * bi
    bbi_ref[...] = fr * bi + fi * br


def _ssm_discretise(a_re, a_im, log_dt, b_re, b_im):
    ar = a_re.reshape(DEPTH, 1, N_STATE)
    ai = a_im.reshape(DEPTH, 1, N_STATE)
    ldt = jnp.repeat(log_dt, SSM_STATE, axis=-1).reshape(DEPTH, 1, N_STATE)
    br = jnp.transpose(b_re, (0, 3, 1, 2)).reshape(DEPTH, SSM_GROUP, N_STATE)
    bi = jnp.transpose(b_im, (0, 3, 1, 2)).reshape(DEPTH, SSM_GROUP, N_STATE)
    row = pl.BlockSpec((None, 1, N_STATE), lambda l: (l, 0, 0))
    mat = pl.BlockSpec((None, SSM_GROUP, N_STATE), lambda l: (l, 0, 0))
    return pl.pallas_call(
        _ssm_disc_kernel,
        grid=(DEPTH,),
        in_specs=[row, row, row, mat, mat],
        out_specs=[row, row, mat, mat],
        out_shape=[jax.ShapeDtypeStruct((DEPTH, 1, N_STATE), F32)] * 2
        + [jax.ShapeDtypeStruct((DEPTH, SSM_GROUP, N_STATE), F32)] * 2,
        compiler_params=_params(("parallel",)),
        name="ssm_discretise",
    )(ar, ai, ldt, br, bi)


def _ssm_pack(abr, abi, bbr, bbi, c_re, c_im):
    eye = jnp.eye(8, dtype=F32)

    def oct_state(v):
        return v.reshape(OCTETS, 512)

    a8 = jnp.concatenate([oct_state(abr), oct_state(abi)], axis=1).reshape(1, OCTETS * 1024)
    a8 = jnp.broadcast_to(a8, (8, OCTETS * 1024))

    def b_tiles(bb):
        t = bb.reshape(SSM_GROUP, OCTETS, 8, SSM_STATE)
        t = jnp.transpose(t, (1, 2, 0, 3))
        t = t[:, :, :, None, :] * eye[None, :, None, :, None]
        return t.reshape(OCTETS, 128, 512)

    bw = jnp.concatenate([b_tiles(bbr), b_tiles(bbi)], axis=2).astype(BF16)

    def c_tiles(cc):
        t = cc.reshape(OCTETS, 8, SSM_GROUP, SSM_STATE)
        t = jnp.transpose(t, (0, 1, 3, 2))
        t = t[:, :, :, None, :] * eye[None, :, None, :, None]
        return t.reshape(OCTETS, 512, 128)

    cw = jnp.concatenate([c_tiles(c_re), -c_tiles(c_im)], axis=1).astype(BF16)
    return a8, bw, cw


def _in_proj_kernel(x_ref, sh_ref, sc_ref, g_ref, w_ref, o_ref, h_scr):
    @pl.when(pl.program_id(1) == 0)
    def _():
        y = _rms(x_ref[...], g_ref[...])
        h_scr[...] = (y * (1.0 + sc_ref[...]) + sh_ref[...]).astype(BF16)

    o_ref[...] = _dot(h_scr[...], w_ref[...])


def _in_proj(x2, mod_arr, mod_spec, g_pre, w_bf, tm):
    rows = x2.shape[0]
    tn = 768
    return pl.pallas_call(
        _in_proj_kernel,
        grid=(rows // tm, N_IN // tn),
        in_specs=[pl.BlockSpec((tm, D_MODEL), lambda i, j: (i, 0)),
                  mod_spec(0), mod_spec(1),
                  pl.BlockSpec((1, D_MODEL), lambda i, j: (0, 0)),
                  pl.BlockSpec((D_MODEL, tn), lambda i, j: (0, j))],
        out_specs=pl.BlockSpec((tm, tn), lambda i, j: (i, j)),
        out_shape=jax.ShapeDtypeStruct((rows, N_IN), F32),
        scratch_shapes=[pltpu.VMEM((tm, D_MODEL), BF16)],
        compiler_params=_params(("parallel", "arbitrary")),
        name="in_proj",
    )(x2, mod_arr, mod_arr, g_pre, w_bf)


def _conv_kernel(a_ref, prev_ref, w_ref, b_ref, lg_ref, lb_ref, y_ref, st_ref, ubuf, *, bb, tq, nt, sub):
    i = pl.program_id(1)
    for b in range(bb):
        @pl.when(i == 0)
        def _():
            ubuf[b, 0:HALO, :] = prev_ref[b]

        @pl.when(i > 0)
        def _():
            ubuf[b, 0:HALO, :] = ubuf[b, tq:tq + HALO, :]

        a = a_ref[b]
        ubuf[b, HALO:HALO + tq, :] = a[:, :D_CONV] * _sigmoid(a[:, D_CONV:])
        base = HALO - (CONV_WIDTH - 1)
        for r0 in range(0, tq, sub):
            acc = jnp.zeros((sub, D_CONV), F32) + b_ref[...]
            for j in range(CONV_WIDTH):
                acc = acc + w_ref[j:j + 1, :] * ubuf[b, base + r0 + j:base + r0 + j + sub, :]
            mu = jnp.mean(acc, axis=-1, keepdims=True)
            xc = acc - mu
            yn = xc * lax.rsqrt(jnp.mean(xc * xc, axis=-1, keepdims=True) + LN_EPS)
            y_ref[b, r0:r0 + sub, :] = _silu(yn * lg_ref[...] + lb_ref[...])

        @pl.when(i == nt - 1)
        def _():
            st_ref[b] = ubuf[b, base + tq:base + tq + CONV_WIDTH - 1, :]


def _conv_branch(z3, prev_pad, w_pad, b, lg, lb, bb, tq):
    nbatch, t_len, _ = z3.shape
    nt = t_len // tq
    sub = min(tq, 32)
    kern = functools.partial(_conv_kernel, bb=bb, tq=tq, nt=nt, sub=sub)
    vec = pl.BlockSpec((1, D_CONV), lambda i, j: (0, 0))
    return pl.pallas_call(
        kern,
        grid=(nbatch // bb, nt),
        in_specs=[pl.BlockSpec((bb, tq, 2 * D_CONV), lambda i, j: (i, j, OFF_CONV // (2 * D_CONV))),
                  pl.BlockSpec((bb, HALO, D_CONV), lambda i, j: (i, 0, 0)),
                  pl.BlockSpec((HALO, D_CONV), lambda i, j: (0, 0)),
                  vec, vec, vec],
        out_specs=[pl.BlockSpec((bb, tq, D_CONV), lambda i, j: (i, j, 0)),
                   pl.BlockSpec((bb, CONV_WIDTH - 1, D_CONV), lambda i, j: (i, 0, 0))],
        out_shape=[jax.ShapeDtypeStruct((nbatch, t_len, D_CONV), F32),
                   jax.ShapeDtypeStruct((nbatch, CONV_WIDTH - 1, D_CONV), F32)],
        scratch_shapes=[pltpu.VMEM((bb, HALO + max(tq, 8) + 8, D_CONV), F32)],
        compiler_params=_params(("parallel", "arbitrary")),
        name="conv_branch",
    )(z3, prev_pad, w_pad, b, lg, lb)


def _ssm_kernel(u_ref, h0r_ref, h0i_ref, a_ref, bw_ref, cw_ref, d_ref, y_ref, hr_ref, hi_ref,
                ub_scr, u_scr, s_scr, y_scr, h_scr, *, nb, tc, nt, flat):
    i = pl.program_id(0)
    nseq = 1 if flat else nb
    rows_seq = nb * tc // nseq

    @pl.when(i == 0)
    def _():
        for o in range(OCTETS):
            h_scr[:, 1024 * o:1024 * o + 512] = h0r_ref[:, 512 * o:512 * (o + 1)]
            h_scr[:, 1024 * o + 512:1024 * (o + 1)] = h0i_ref[:, 512 * o:512 * (o + 1)]

    for o in range(OCTETS):
        for b in range(nseq):
            src = u_ref[:, 128 * o:128 * (o + 1)] if flat else u_ref[b, :, 128 * o:128 * (o + 1)]
            ub_scr[o, b * rows_seq:(b + 1) * rows_seq, :] = src
        for t in range(tc):
            u_scr[t * nb:(t + 1) * nb, 128 * o:128 * (o + 1)] = ub_scr[o, pl.ds(t, nb, stride=tc), :]

    for o in range(OCTETS):
        ub = u_scr[:, 128 * o:128 * (o + 1)].astype(BF16)
        s_scr[:, 1024 * o:1024 * (o + 1)] = _dot(ub, bw_ref[o])

    for o in range(OCTETS):
        c0 = 1024 * o
        ar = a_ref[:, c0:c0 + 512]
        ai = a_ref[:, c0 + 512:c0 + 1024]

        def rows_body(rg, carry, c0=c0, ar=ar, ai=ai):
            r0 = pl.multiple_of(rg * 8, 8)
            hr = h_scr[pl.ds(r0, 8), c0:c0 + 512]
            hi = h_scr[pl.ds(r0, 8), c0 + 512:c0 + 1024]

            def step(t, h):
                hr, hi = h
                row = pl.multiple_of(t * nb + r0, 8)
                nr = ar * hr - ai * hi + s_scr[pl.ds(row, 8), c0:c0 + 512]
                ni = ar * hi + ai * hr + s_scr[pl.ds(row, 8), c0 + 512:c0 + 1024]
                s_scr[pl.ds(row, 8), c0:c0 + 512] = nr
                s_scr[pl.ds(row, 8), c0 + 512:c0 + 1024] = ni
                return nr, ni

            hr, hi = lax.fori_loop(0, tc, step, (hr, hi))
            h_scr[pl.ds(r0, 8), c0:c0 + 512] = hr
            h_scr[pl.ds(r0, 8), c0 + 512:c0 + 1024] = hi
            return carry

        lax.fori_loop(0, nb // 8, rows_body, 0)

    for o in range(OCTETS):
        hb = s_scr[:, 1024 * o:1024 * (o + 1)].astype(BF16)
        y_scr[:, 128 * o:128 * (o + 1)] = _dot(hb, cw_ref[o])

    y_scr[...] = jax.nn.gelu(y_scr[...] + d_ref[...] * u_scr[...])
    for o in range(OCTETS):
        for t in range(tc):
            ub_scr[o, pl.ds(t, nb, stride=tc), :] = y_scr[t * nb:(t + 1) * nb, 128 * o:128 * (o + 1)]
        for b in range(nseq):
            blk = ub_scr[o, b * rows_seq:(b + 1) * rows_seq, :]
            if flat:
                y_ref[:, 128 * o:128 * (o + 1)] = blk
            else:
                y_ref[b, :, 128 * o:128 * (o + 1)] = blk

    @pl.when(i == nt - 1)
    def _():
        for o in range(OCTETS):
            hr_ref[:, 512 * o:512 * (o + 1)] = h_scr[:, 1024 * o:1024 * o + 512]
            hi_ref[:, 512 * o:512 * (o + 1)] = h_scr[:, 1024 * o + 512:1024 * (o + 1)]


def _ssm_branch(z, h0r, h0i, a8, bw, cw, dvec, nb, t_len, tc, flat):
    nt = t_len // tc
    m = nb * tc
    kern = functools.partial(_ssm_kernel, nb=nb, tc=tc, nt=nt, flat=flat)
    if flat:
        u_spec = pl.BlockSpec((m, D_SSM), lambda i: (0, OFF_SSM // D_SSM))
        y_spec = pl.BlockSpec((m, D_SSM), lambda i: (0, 0))
        y_shape = jax.ShapeDtypeStruct((m, D_SSM), F32)
    else:
        u_spec = pl.BlockSpec((nb, tc, D_SSM), lambda i: (0, i, OFF_SSM // D_SSM))
        y_spec = pl.BlockSpec((nb, tc, D_SSM), lambda i: (0, i, 0))
        y_shape = jax.ShapeDtypeStruct((nb, t_len, D_SSM), F32)
    st_spec = pl.BlockSpec((nb, N_STATE), lambda i: (0, 0))
    return pl.pallas_call(
        kern,
        grid=(nt,),
        in_specs=[u_spec, st_spec, st_spec,
                  pl.BlockSpec((8, 2 * N_STATE), lambda i: (0, 0)),
                  pl.BlockSpec((OCTETS, 128, 1024), lambda i: (0, 0, 0)),
                  pl.BlockSpec((OCTETS, 1024, 128), lambda i: (0, 0, 0)),
                  pl.BlockSpec((1, D_SSM), lambda i: (0, 0))],
        out_specs=[y_spec, st_spec, st_spec],
        out_shape=[y_shape, jax.ShapeDtypeStruct((nb, N_STATE), F32),
                   jax.ShapeDtypeStruct((nb, N_STATE), F32)],
        scratch_shapes=[pltpu.VMEM((OCTETS, m, 128), F32), pltpu.VMEM((m, D_SSM), F32),
                        pltpu.VMEM((m, 2 * N_STATE), F32), pltpu.VMEM((m, D_SSM), F32),
                        pltpu.VMEM((nb, 2 * N_STATE), F32)],
        compiler_params=_params(("arbitrary",)),
        name="ssm_branch",
    )(z, h0r, h0i, a8, bw, cw, dvec)


def _att_prompt_kernel(q0_ref, q1_ref, k0_ref, k1_ref, v0_ref, v1_ref, o_ref, o_scr, l_scr, bp_scr, *, t_len):
    g_id = pl.program_id(1)
    slopes = _slopes()
    q_refs = (q0_ref, q1_ref)
    k_refs = (k0_ref, k1_ref)
    v_refs = (v0_ref, v1_ref)
    lane = lax.broadcasted_iota(jnp.int32, (BAND, 128), 1)
    low = lane < HEAD_DIM

    def group(g, dil):
        nblk = t_len // dil // BAND
        qi = lax.broadcasted_iota(jnp.int32, (BAND, 2 * BAND), 0)
        kj = lax.broadcasted_iota(jnp.int32, (BAND, 2 * BAND), 1)
        dist = qi + BAND - kj
        valid = (dist >= 0) & (dist <= BAND)
        valid0 = valid & (kj >= BAND)
        distf = dist.astype(F32)
        for h in range(HEADS_PER_GROUP):
            bias = (-slopes[g * HEADS_PER_GROUP + h] * dil) * distf
            bp_scr[0, h] = jnp.where(valid0, bias, NEG)
            bp_scr[1, h] = jnp.where(valid, bias, NEG)

        def rows(start):
            if dil == 1:
                return pl.ds(start, BAND)
            return pl.ds(start, BAND, stride=dil)

        def block(blk, carry):
            r = blk // nblk
            n = blk % nblk
            start_c = r + dil * BAND * n
            start_p = r + dil * BAND * jnp.maximum(n - 1, 0)
            sel = jnp.minimum(n, 1)
            for pair in range(2):
                qp = q_refs[pair][rows(start_c), :] * ATT_SCALE
                kp = jnp.concatenate([k_refs[pair][rows(start_p), :], k_refs[pair][rows(start_c), :]],
                                     axis=0).astype(BF16)
                vp = jnp.concatenate([v_refs[pair][rows(start_p), :], v_refs[pair][rows(start_c), :]],
                                     axis=0).astype(BF16)
                o_h, l_h = [], []
                for hh in range(2):
                    h = 2 * pair + hh
                    qm = jnp.where(low if hh == 0 else jnp.logical_not(low), qp, 0.0).astype(BF16)
                    s = lax.dot_general(qm, kp, (((1,), (1,)), ((), ())), preferred_element_type=F32)
                    t = s + bp_scr[sel, h]
                    mx = jnp.max(t, axis=-1, keepdims=True)
                    p = jnp.exp(t - mx)
                    den = jnp.sum(p, axis=-1, keepdims=True)
                    pv = _dot(p.astype(BF16), vp)
                    o_h.append(pv / den)
                    l_h.append(jnp.broadcast_to(mx + jnp.log(den), (BAND, 128)))
                o_scr[2 * g + pair, rows(start_c), :] = jnp.where(low, o_h[0], o_h[1])
                l_scr[2 * g + pair, rows(start_c), :] = jnp.where(low, l_h[0], l_h[1])
            return carry

        lax.fori_loop(0, dil * nblk, block, 0)

    for g, (window, dil) in enumerate(ATT_PATTERNS):
        @pl.when(g_id == g)
        def _(g=g, dil=dil):
            group(g, dil)

    @pl.when(g_id == len(ATT_PATTERNS) - 1)
    def _():
        tr = 256

        def comb(i, carry):
            r0 = pl.multiple_of(i * tr, tr)
            for pair in range(2):
                l0 = l_scr[pair, pl.ds(r0, tr), :]
                l1 = l_scr[2 + pair, pl.ds(r0, tr), :]
                l2 = l_scr[4 + pair, pl.ds(r0, tr), :]
                mx = jnp.maximum(jnp.maximum(l0, l1), l2)
                w0 = jnp.exp(l0 - mx)
                w1 = jnp.exp(l1 - mx)
                w2 = jnp.exp(l2 - mx)
                num = (w0 * o_scr[pair, pl.ds(r0, tr), :] + w1 * o_scr[2 + pair, pl.ds(r0, tr), :]
                       + w2 * o_scr[4 + pair, pl.ds(r0, tr), :])
                o_ref[pl.ds(r0, tr), 128 * pair:128 * (pair + 1)] = num / (w0 + w1 + w2)
            return carry

        lax.fori_loop(0, t_len // tr, comb, 0)


def _att_prompt(z3):
    nbatch, t_len, _ = z3.shape
    ng = len(ATT_PATTERNS)
    kern = functools.partial(_att_prompt_kernel, t_len=t_len)

    def spec(off, pair):
        return pl.BlockSpec((None, t_len, 128), lambda b, g: (b, 0, off // 128 + 2 * g + pair))

    return pl.pallas_call(
        kern,
        grid=(nbatch, ng),
        in_specs=[spec(off, pair) for off in (OFF_Q, OFF_K, OFF_V) for pair in range(2)],
        out_specs=pl.BlockSpec((None, t_len, D_ATT_OUT), lambda b, g: (b, 0, 0)),
        out_shape=jax.ShapeDtypeStruct((nbatch, t_len, D_ATT_OUT), F32),
        scratch_shapes=[pltpu.VMEM((2 * ng, t_len, 128), F32),
                        pltpu.VMEM((2 * ng, t_len, 128), F32),
                        pltpu.VMEM((2, HEADS_PER_GROUP, BAND, 2 * BAND), F32)],
        compiler_params=_params(("parallel", "arbitrary")),
        name="att_prompt",
    )(z3, z3, z3, z3, z3, z3)


def _split2(x):
    hi = x.astype(BF16)
    lo = (x - hi.astype(F32)).astype(BF16)
    return hi, lo


def _att_sample_kernel(*refs, bb, t_new):
    nq = t_new
    q_refs = refs[0:nq]
    k_refs = refs[nq:2 * nq]
    v_refs = refs[2 * nq:3 * nq]
    kc_refs = refs[3 * nq:3 * nq + 3]
    vc_refs = refs[3 * nq + 3:3 * nq + 6]
    bc_ref, bn_ref, seg_ref, segt_ref, o_ref = refs[3 * nq + 6:3 * nq + 11]
    w = D_ATT_OUT
    zrow = jnp.zeros((1, nq * w), F32)

    def tile_new(r, b, g):
        piece = r[pl.ds(b, 1), g * w:(g + 1) * w]
        return jnp.concatenate([piece] * nq, axis=1)

    def own_new(rs, b, g):
        return jnp.concatenate([r[pl.ds(b, 1), g * w:(g + 1) * w] for r in rs], axis=1)

    def seg_sum(x):
        hi, lo = _split2(x)
        return _dot(hi, seg_ref[...]) + _dot(lo, seg_ref[...])

    def body(b, carry):
        accs, dens, lses = [], [], []
        for g in range(len(ATT_PATTERNS)):
            qexp = own_new(q_refs, b, g) * ATT_SCALE
            kc = kc_refs[g][b]
            vc = vc_refs[g][b]
            if g == 0:
                kc = jnp.concatenate([kc] * nq, axis=1)
                vc = jnp.concatenate([vc] * nq, axis=1)
                kn = jnp.concatenate([tile_new(r, b, g) for r in k_refs] + [zrow] * (8 - nq), axis=0)
                vn = jnp.concatenate([tile_new(r, b, g) for r in v_refs] + [zrow] * (8 - nq), axis=0)
            else:
                kn = jnp.concatenate([own_new(k_refs, b, g)] + [zrow] * 7, axis=0)
                vn = jnp.concatenate([own_new(v_refs, b, g)] + [zrow] * 7, axis=0)
            bias_c = bc_ref[g]
            bias_n = bn_ref[g]
            t_c = jnp.where(bias_c > 0.5 * NEG, seg_sum(kc * qexp) + bias_c, NEG)
            t_n = jnp.where(bias_n > 0.5 * NEG, seg_sum(kn * qexp) + bias_n, NEG)
            mx = jnp.maximum(jnp.max(t_c, axis=0, keepdims=True), jnp.max(t_n, axis=0, keepdims=True))
            p_c = jnp.exp(t_c - mx)
            p_n = jnp.exp(t_n - mx)
            den = jnp.sum(p_c, axis=0, keepdims=True) + jnp.sum(p_n, axis=0, keepdims=True)
            pe_c = _dot(p_c.astype(BF16), segt_ref[...])
            pe_n = _dot(p_n.astype(BF16), segt_ref[...])
            acc = jnp.sum(pe_c * vc, axis=0, keepdims=True) + jnp.sum(pe_n * vn, axis=0, keepdims=True)
            accs.append(acc)
            dens.append(den)
            lses.append(mx + jnp.log(den))
        mx = jnp.maximum(jnp.maximum(lses[0], lses[1]), lses[2])
        ws = [jnp.exp(l - mx) for l in lses]
        tot = ws[0] + ws[1] + ws[2]
        coef = jnp.concatenate([wg / tot / dg for wg, dg in zip(ws, dens)]
                               + [jnp.zeros((5, 128), F32)], axis=0)
        c1 = coef.astype(BF16)
        r1 = coef - c1.astype(F32)
        c2 = r1.astype(BF16)
        c3 = (r1 - c2.astype(F32)).astype(BF16)
        cexp = _dot(c1, segt_ref[...]) + _dot(c2, segt_ref[...]) + _dot(c3, segt_ref[...])
        o_ref[pl.ds(b, 1), :] = (cexp[0:1] * accs[0] + cexp[1:2] * accs[1] + cexp[2:3] * accs[2])
        return carry

    lax.fori_loop(0, bb, body, 0)


def _att_sample_consts(t_new):
    slopes = _slopes()
    nslot = t_new * HEADS_PER_GROUP
    bc = np.full((3, BAND, 128), NEG, np.float32)
    bn = np.full((3, 8, 128), NEG, np.float32)
    for g, (window, dil) in enumerate(ATT_PATTERNS):
        for j in range(t_new):
            for h in range(HEADS_PER_GROUP):
                s = slopes[g * HEADS_PER_GROUP + h]
                col = j * HEADS_PER_GROUP + h
                for m_ in range(BAND):
                    if g == 0:
                        kk = BAND + j - m_
                        if kk <= BAND:
                            bc[g, m_, col] = -s * kk
                    else:
                        bc[g, m_, col] = -s * dil * (BAND - m_)
                if g == 0:
                    for i_ in range(j + 1):
                        bn[g, i_, col] = -s * (j - i_)
                else:
                    bn[g, 0, col] = 0.0
    seg = np.zeros((t_new * D_ATT_OUT, 128), np.float32)
    for c in range(t_new * D_ATT_OUT):
        seg[c, c // HEAD_DIM] = 1.0
    assert nslot <= 128
    return (jnp.asarray(bc), jnp.asarray(bn), jnp.asarray(seg, dtype=BF16),
            jnp.asarray(seg.T.copy(), dtype=BF16))


def _att_sample(z2, caches_k, caches_v, layer, nbatch, t_new):
    bb = 8
    w = D_ATT_OUT
    zrow = z2.reshape(nbatch, t_new * N_IN)
    nblk = N_IN // D_ATT
    bc, bn, seg, segt = _att_sample_consts(t_new)

    def new_spec(j, off):
        return pl.BlockSpec((bb, D_ATT), lambda i: (i, nblk * j + off // D_ATT))

    def cache_view(c, dil):
        return c.reshape(DEPTH, nbatch, BAND, dil * w)

    def cache_spec(dil):
        return pl.BlockSpec((None, bb, BAND, min(dil, t_new) * w), lambda i: (layer, i, 0, 0))

    kern = functools.partial(_att_sample_kernel, bb=bb, t_new=t_new)
    ins, specs = [], []
    for off in (OFF_Q, OFF_K, OFF_V):
        for j in range(t_new):
            ins.append(zrow)
            specs.append(new_spec(j, off))
    for cs in (caches_k, caches_v):
        for c, (window, dil) in zip(cs, ATT_PATTERNS):
            ins.append(cache_view(c, dil))
            specs.append(cache_spec(dil))
    ins += [bc, bn, seg, segt]
    specs += [pl.BlockSpec((3, BAND, 128), lambda i: (0, 0, 0)),
              pl.BlockSpec((3, 8, 128), lambda i: (0, 0, 0)),
              pl.BlockSpec((t_new * w, 128), lambda i: (0, 0)),
              pl.BlockSpec((128, t_new * w), lambda i: (0, 0))]
    out = pl.pallas_call(
        kern,
        grid=(nbatch // bb,),
        in_specs=specs,
        out_specs=pl.BlockSpec((bb, t_new * w), lambda i: (i, 0)),
        out_shape=jax.ShapeDtypeStruct((nbatch, t_new * w), F32),
        compiler_params=_params(("parallel",)),
        name="att_sample",
    )(*ins)
    return out.reshape(nbatch * t_new, w)


QROWS = 8


def _att_sample_t_kernel(q_ref, kn_ref, vn_ref, k0_ref, k1_ref, k2_ref, v0_ref, v1_ref, v2_ref,
                         b0_ref, b1_ref, b2_ref, bn_ref, o_ref):
    kc_refs = (k0_ref, k1_ref, k2_ref)
    vc_refs = (v0_ref, v1_ref, v2_ref)
    bc_refs = (b0_ref, b1_ref, b2_ref)
    w = D_ATT_OUT
    head = lax.broadcasted_iota(jnp.int32, (QROWS, w), 1) // HEAD_DIM
    zpad = jnp.zeros((128 - QROWS, w), BF16)
    outs, lses = [], []
    for g in range(len(ATT_PATTERNS)):
        qg = q_ref[:, g * w:(g + 1) * w] * ATT_SCALE
        qblk = jnp.concatenate([jnp.where(head == h, qg, 0.0) for h in range(HEADS_PER_GROUP)],
                               axis=0).astype(BF16)
        s_c = _dot(qblk, kc_refs[g][...].astype(BF16))
        bias_c = bc_refs[g][...]
        t_c = jnp.where(bias_c > 0.5 * NEG, s_c + bias_c, NEG)
        kn = jnp.concatenate([kn_ref[:, g * w:(g + 1) * w].astype(BF16), zpad], axis=0)
        vn = jnp.concatenate([vn_ref[:, g * w:(g + 1) * w].astype(BF16), zpad], axis=0)
        s_n = lax.dot_general(qblk, kn, (((1,), (1,)), ((), ())), preferred_element_type=F32)
        bias_n = bn_ref[g]
        t_n = jnp.where(bias_n > 0.5 * NEG, s_n + bias_n, NEG)
        mx = jnp.maximum(jnp.max(t_c, axis=-1, keepdims=True), jnp.max(t_n, axis=-1, keepdims=True))
        p_c = jnp.exp(t_c - mx)
        p_n = jnp.exp(t_n - mx)
        den = jnp.sum(p_c, axis=-1, keepdims=True) + jnp.sum(p_n, axis=-1, keepdims=True)
        o = lax.dot_general(p_c.astype(BF16), vc_refs[g][...].astype(BF16), (((1,), (1,)), ((), ())),
                            preferred_element_type=F32)
        o = (o + _dot(p_n.astype(BF16), vn)) / den
        lse = jnp.broadcast_to(mx + jnp.log(den), o.shape)
        og = jnp.zeros((QROWS, w), F32)
        lg = jnp.zeros((QROWS, w), F32)
        for h in range(HEADS_PER_GROUP):
            og = jnp.where(head == h, o[QROWS * h:QROWS * (h + 1)], og)
            lg = jnp.where(head == h, lse[QROWS * h:QROWS * (h + 1)], lg)
        outs.append(og)
        lses.append(lg)
    mx = jnp.maximum(jnp.maximum(lses[0], lses[1]), lses[2])
    ws = [jnp.exp(l - mx) for l in lses]
    o_ref[...] = (ws[0] * outs[0] + ws[1] * outs[1] + ws[2] * outs[2]) / (ws[0] + ws[1] + ws[2])


def _att_sample_t_consts(t_new):
    slopes = _slopes()
    rows = HEADS_PER_GROUP * QROWS
    bcs = []
    bn = np.full((3, rows, 128), NEG, np.float32)
    for g, (window, dil) in enumerate(ATT_PATTERNS):
        bc = np.zeros((rows, window), np.float32)
        pos = np.arange(window)
        for h in range(HEADS_PER_GROUP):
            s = slopes[g * HEADS_PER_GROUP + h]
            for j in range(QROWS):
                r = QROWS * h + j
                if j >= t_new:
                    bn[g, r, 0] = 0.0
                    continue
                dist = window + j - pos
                ok = (dist % dil == 0) & (dist <= window)
                bc[r] = np.where(ok, -s * dist, NEG)
                for i_ in range(j + 1):
                    if (j - i_) % dil == 0:
                        bn[g, r, i_] = -s * (j - i_)
        bcs.append(jnp.asarray(bc))
    return bcs, jnp.asarray(bn)


def _att_sample_t(z3, caches_k, caches_v, layer, nbatch, t_new):
    w = D_ATT_OUT
    qkv = jnp.pad(z3[:, :, OFF_Q:], ((0, 0), (0, QROWS - t_new), (0, 0))).reshape(nbatch * QROWS, 3 * D_ATT)
    bcs, bn = _att_sample_t_consts(t_new)

    def cache_t(c):
        d, nb_, length, nh, hd = c.shape
        return jnp.transpose(c, (0, 1, 3, 4, 2)).reshape(d, nb_, nh * hd, length)

    def new_spec(part):
        return pl.BlockSpec((QROWS, D_ATT), lambda i: (i, part))

    def cache_spec(length):
        return pl.BlockSpec((None, None, w, length), lambda i: (layer, i, 0, 0))

    rows = HEADS_PER_GROUP * QROWS
    ins = [qkv, qkv, qkv] + [cache_t(c) for c in caches_k] + [cache_t(c) for c in caches_v] + bcs + [bn]
    specs = ([new_spec(0), new_spec(1), new_spec(2)]
             + [cache_spec(win) for win, _ in ATT_PATTERNS] * 2
             + [pl.BlockSpec((rows, win), lambda i: (0, 0)) for win, _ in ATT_PATTERNS]
             + [pl.BlockSpec((3, rows, 128), lambda i: (0, 0, 0))])
    out = pl.pallas_call(
        _att_sample_t_kernel,
        grid=(nbatch,),
        in_specs=specs,
        out_specs=pl.BlockSpec((QROWS, w), lambda i: (i, 0)),
        out_shape=jax.ShapeDtypeStruct((nbatch * QROWS, w), F32),
        compiler_params=_params(("parallel",)),
        name="att_sample",
    )(*ins)
    return out.reshape(nbatch, QROWS, w)[:, :t_new].reshape(nbatch * t_new, w)


def _merge_kernel(cy_ref, zs_ref, at_ref, gt_ref, x_ref, gm_ref, shf_ref, scf_ref,
                  wco_ref, wsg_ref, wat_ref, wo_ref, gpost_ref, gpre_ref, x1_ref, hf_ref):
    a = _dot(cy_ref[...].astype(BF16), wco_ref[...])
    gl = _dot(zs_ref[...].astype(BF16), wsg_ref[...])
    bm = gl[:, :D_MODEL] * _sigmoid(gl[:, D_MODEL:])
    c = _dot(at_ref[...].astype(BF16), wat_ref[...])
    merged = (_sigmoid(gt_ref[:, 0:D_MODEL]) * a
              + _sigmoid(gt_ref[:, D_MODEL:2 * D_MODEL]) * bm
              + _sigmoid(gt_ref[:, 2 * D_MODEL:3 * D_MODEL]) * c)
    mo = _dot(merged.astype(BF16), wo_ref[...])
    x1 = x_ref[...] + gm_ref[...] * _rms(mo, gpost_ref[...])
    x1_ref[...] = x1
    hf_ref[...] = (_rms(x1, gpre_ref[...]) * (1.0 + scf_ref[...]) + shf_ref[...]).astype(BF16)


def _merge(cy, zs, at, z2, x2, mod_arr, mod_spec, wco, wsg, wat, wo, g_post, g_pre_ffn, tm):
    rows = x2.shape[0]

    def rowspec(width):
        return pl.BlockSpec((tm, width), lambda i: (i, 0))

    def full(shape):
        return pl.BlockSpec(shape, lambda i: (0, 0))

    return pl.pallas_call(
        _merge_kernel,
        grid=(rows // tm,),
        in_specs=[rowspec(D_CONV), rowspec(D_SSM), rowspec(D_ATT_OUT), rowspec(3 * D_MODEL),
                  rowspec(D_MODEL), mod_spec(2), mod_spec(3), mod_spec(4),
                  full((D_CONV, D_MODEL)), full((D_SSM, 2 * D_MODEL)), full((D_ATT_OUT, D_MODEL)),
                  full((D_MODEL, D_MODEL)), full((1, D_MODEL)), full((1, D_MODEL))],
        out_specs=[rowspec(D_MODEL), rowspec(D_MODEL)],
        out_shape=[jax.ShapeDtypeStruct((rows, D_MODEL), F32),
                   jax.ShapeDtypeStruct((rows, D_MODEL), BF16)],
        compiler_params=_params(("parallel",)),
        name="merge",
    )(cy, zs, at, z2, x2, mod_arr, mod_arr, mod_arr, wco, wsg, wat, wo, g_post, g_pre_ffn)


FF_CHUNK = 256


def _ffn_kernel(hf_ref, x1_ref, gf_ref, w1_ref, w2_ref, gpost_ref, o_ref):
    hf = hf_ref[...]
    acc = jnp.zeros(o_ref.shape, F32)
    for c in range(D_FF // FF_CHUNK):
        lo = c * FF_CHUNK
        a = _dot(hf, w1_ref[:, lo:lo + FF_CHUNK])
        b = _dot(hf, w1_ref[:, D_FF + lo:D_FF + lo + FF_CHUNK])
        f = (_silu(a) * b).astype(BF16)
        acc = acc + _dot(f, w2_ref[lo:lo + FF_CHUNK, :])
    o_ref[...] = x1_ref[...] + gf_ref[...] * _rms(acc, gpost_ref[...])


def _ffn(hf, x1, mod_arr, mod_spec, w1, w2, g_post, tm):
    rows = x1.shape[0]
    return pl.pallas_call(
        _ffn_kernel,
        grid=(rows // tm,),
        in_specs=[pl.BlockSpec((tm, D_MODEL), lambda i: (i, 0)),
                  pl.BlockSpec((tm, D_MODEL), lambda i: (i, 0)),
                  mod_spec(5),
                  pl.BlockSpec((D_MODEL, 2 * D_FF), lambda i: (0, 0)),
                  pl.BlockSpec((D_FF, D_MODEL), lambda i: (0, 0)),
                  pl.BlockSpec((1, D_MODEL), lambda i: (0, 0))],
        out_specs=pl.BlockSpec((tm, D_MODEL), lambda i: (i, 0)),
        out_shape=jax.ShapeDtypeStruct((rows, D_MODEL), F32),
        compiler_params=_params(("parallel",)),
        name="ffn",
    )(hf, x1, mod_arr, w1, w2, g_post)


def _layer(x2, nbatch, t_len, mod_arr, mod_specs, p, past, is_prompt, layer):
    rows = nbatch * t_len
    tm_in = 1024 if is_prompt else rows
    tm_row = 256
    z2 = _in_proj(x2, mod_arr, mod_specs(tm_in, 2), p["g_pre_mix"], p["w_in"], tm_in)
    z3 = z2.reshape(nbatch, t_len, N_IN)

    conv_prev, h0r, h0i, caches_k, caches_v = past
    if is_prompt:
        cy, conv_state = _conv_branch(z3, conv_prev, p["conv_w"], p["conv_b"], p["conv_ln_g"],
                                      p["conv_ln_b"], bb=1, tq=256)
        zs, s_re, s_im = _ssm_branch(z3, h0r, h0i, p["a8"], p["bw"], p["cw"], p["ssm_d"],
                                     nb=nbatch, t_len=t_len, tc=64, flat=False)
        att = _att_prompt(z3).reshape(rows, D_ATT_OUT)
    else:
        cy, conv_state = _conv_branch(z3, conv_prev, p["conv_w"], p["conv_b"], p["conv_ln_g"],
                                      p["conv_ln_b"], bb=8, tq=t_len)
        zs, s_re, s_im = _ssm_branch(z2, h0r, h0i, p["a8"], p["bw"], p["cw"], p["ssm_d"],
                                     nb=nbatch, t_len=t_len, tc=t_len, flat=True)
        att = _att_sample_t(z3, caches_k, caches_v, layer, nbatch, t_len)

    x1, hf = _merge(cy.reshape(rows, D_CONV), zs.reshape(rows, D_SSM), att, z2, x2,
                    mod_arr, mod_specs(tm_row, 1), p["w_conv_out"], p["w_ssm_glu"], p["w_att"],
                    p["w_out"], p["g_post_mix"], p["g_pre_ffn"], tm_row)
    x_out = _ffn(hf, x1, mod_arr, mod_specs(tm_row, 1), p["w_ffn_in"], p["w_ffn_out"],
                 p["g_post_ffn"], tm_row)

    kv = []
    for g, (window, dil) in enumerate(ATT_PATTERNS):
        keep = min(window, t_len) if is_prompt else t_len
        for off in (OFF_K, OFF_V):
            cols = z3[:, t_len - keep:, off + g * D_ATT_OUT:off + (g + 1) * D_ATT_OUT]
            kv.append(cols.reshape(nbatch, keep, HEADS_PER_GROUP, HEAD_DIM))
    states = (*kv, conv_state,
              s_re.reshape(nbatch, N_SSM_GROUPS, SSM_STATE), s_im.reshape(nbatch, N_SSM_GROUPS, SSM_STATE))
    return x_out, states


def kernel(x_prompt, x_sample, cache_k0, cache_v0, cache_k1, cache_v1, cache_k2, cache_v2, state_conv, state_ssm_re, state_ssm_im, c_prompt, c_sample, w_mod, b_mod, g_pre_mix, g_post_mix, g_pre_ffn, g_post_ffn, w_in, conv_w, conv_b, conv_ln_g, conv_ln_b, w_conv_out, ssm_a_re, ssm_a_im, ssm_log_dt, ssm_b_re, ssm_b_im, ssm_c_re, ssm_c_im, ssm_d, w_ssm_glu, w_att, w_out, w_ffn_in, w_ffn_out):
    nbp, t_p, _ = x_prompt.shape
    nbs, t_s, _ = x_sample.shape
    caches_k = (cache_k0, cache_k1, cache_k2)
    caches_v = (cache_v0, cache_v1, cache_v2)

    mod = _modulation(jnp.concatenate([c_prompt, c_sample], axis=0), w_mod, b_mod)
    abr, abi, bbr, bbi = _ssm_discretise(ssm_a_re, ssm_a_im, ssm_log_dt, ssm_b_re, ssm_b_im)

    xp = x_prompt.reshape(nbp * t_p, D_MODEL)
    xs = x_sample.reshape(nbs * t_s, D_MODEL)
    prompt_states, sample_states = [], []
    for l in range(DEPTH):
        a8, bw, cw = _ssm_pack(abr[l], abi[l], bbr[l], bbi[l], ssm_c_re[l], ssm_c_im[l])
        p = dict(
            g_pre_mix=g_pre_mix[l][None], g_post_mix=g_post_mix[l][None],
            g_pre_ffn=g_pre_ffn[l][None], g_post_ffn=g_post_ffn[l][None],
            w_in=jnp.concatenate([w_in[l][:, 3840:], w_in[l][:, :3840]], axis=1).astype(BF16),
            conv_w=jnp.pad(conv_w[l], ((0, HALO - CONV_WIDTH), (0, 0))),
            conv_b=conv_b[l][None], conv_ln_g=conv_ln_g[l][None], conv_ln_b=conv_ln_b[l][None],
            w_conv_out=w_conv_out[l].astype(BF16), a8=a8, bw=bw, cw=cw,
            ssm_d=ssm_d[l].reshape(1, D_SSM), w_ssm_glu=w_ssm_glu[l].astype(BF16),
            w_att=w_att[l].astype(BF16), w_out=w_out[l].astype(BF16),
            w_ffn_in=w_ffn_in[l].astype(BF16), w_ffn_out=w_ffn_out[l].astype(BF16))

        mod_p = mod[l, :nbp].reshape(nbp, 1, 6 * D_MODEL)

        def specs_p(tm, ngrid, t_p=t_p):
            per_seq = t_p // tm

            def chunk(c):
                if ngrid == 2:
                    return pl.BlockSpec((None, 1, D_MODEL), lambda i, j: (i // per_seq, 0, c))
                return pl.BlockSpec((None, 1, D_MODEL), lambda i: (i // per_seq, 0, c))
            return chunk

        past_p = (jnp.zeros((nbp, HALO, D_CONV), F32), jnp.zeros((nbp, N_STATE), F32),
                  jnp.zeros((nbp, N_STATE), F32), None, None)
        xp, st_p = _layer(xp, nbp, t_p, mod_p, specs_p, p, past_p, True, l)

        mod_s = jnp.repeat(mod[l, nbp:], t_s, axis=0)

        def specs_s(tm, ngrid):
            def chunk(c):
                if ngrid == 2:
                    return pl.BlockSpec((tm, D_MODEL), lambda i, j: (i, c))
                return pl.BlockSpec((tm, D_MODEL), lambda i: (i, c))
            return chunk

        past_s = (jnp.pad(state_conv[l], ((0, 0), (HALO - CONV_WIDTH + 1, 0), (0, 0))),
                  state_ssm_re[l].reshape(nbs, N_STATE), state_ssm_im[l].reshape(nbs, N_STATE),
                  caches_k, caches_v)
        xs, st_s = _layer(xs, nbs, t_s, mod_s, specs_s, p, past_s, False, l)
        prompt_states.append(st_p)
        sample_states.append(st_s)

    ps = [jnp.stack([s[i] for s in prompt_states]) for i in range(9)]
    ss = [jnp.stack([s[i] for s in sample_states]) for i in range(9)]
    return (xp.reshape(nbp, t_p, D_MODEL), xs.reshape(nbs, t_s, D_MODEL), *ps, *ss)
```

```python
import functools

import numpy as np
import jax
import jax.numpy as jnp
from jax import lax
from jax.experimental import pallas as pl
from jax.experimental.pallas import tpu as pltpu

F32 = jnp.float32
BF16 = jnp.bfloat16

D_MODEL = 1024
DEPTH = 2
D_CONV = 512
CONV_WIDTH = 31
D_SSM = 512
SSM_GROUP = 16
N_SSM_GROUPS = 32
SSM_STATE = 64
HEAD_DIM = 64
HEADS_PER_GROUP = 4
ATT_PATTERNS = ((128, 1), (512, 4), (2048, 16))
N_ATT_HEADS = 12
D_ATT = 768
D_ATT_OUT = 256
ATT_SCALE = HEAD_DIM ** -0.5
D_FF = 2816
N_IN = 6912
RMS_EPS = 1e-6
LN_EPS = 1e-5
NEG = -1e30
BAND = 128

OFF_GATES = 0
OFF_CONV = 3072
OFF_SSM = 4096
OFF_Q = 4608
OFF_K = 5376
OFF_V = 6144
IN_CHUNK = 768

LANES = 128
SUBLANES = 8
HALO = 32
N_STATE = N_SSM_GROUPS * SSM_STATE
OCTETS = 4
OCT_IN = D_SSM // OCTETS
OCT_ST = N_STATE // OCTETS
VMEM_LIMIT = 56 * 1024 * 1024


def _params(sem):
    return pltpu.CompilerParams(dimension_semantics=sem, vmem_limit_bytes=VMEM_LIMIT)


def _resident(shape):
    return pl.BlockSpec(shape, lambda *_: (0,) * len(shape), pipeline_mode=pl.Buffered(1))


def _slopes():
    return [2.0 ** (-8.0 * (i + 1) / N_ATT_HEADS) for i in range(N_ATT_HEADS)]


def _rms(x, g):
    return x * lax.rsqrt(jnp.mean(x * x, axis=-1, keepdims=True) + RMS_EPS) * g


def _sigmoid(x):
    return 0.5 * jnp.tanh(0.5 * x) + 0.5


def _silu(x):
    return x * _sigmoid(x)


def _dot(a, b):
    return jnp.dot(a, b, preferred_element_type=F32)


def _dot_nt(a, b):
    return lax.dot_general(a, b, (((1,), (1,)), ((), ())), preferred_element_type=F32)


def _mod_kernel(c_ref, w_ref, b_ref, o_ref):
    a = _silu(c_ref[...]).astype(BF16)
    o_ref[...] = _dot(a, w_ref[...].astype(BF16)) + b_ref[...]


def _modulation(c_all, w_mod, b_mod):
    nb = c_all.shape[0]
    tn = 512
    return pl.pallas_call(
        _mod_kernel,
        grid=(DEPTH, 6 * D_MODEL // tn),
        in_specs=[pl.BlockSpec((nb, D_MODEL), lambda l, j: (0, 0)),
                  pl.BlockSpec((None, D_MODEL, tn), lambda l, j: (l, 0, j)),
                  pl.BlockSpec((None, 1, tn), lambda l, j: (l, 0, j))],
        out_specs=pl.BlockSpec((None, nb, tn), lambda l, j: (l, 0, j)),
        out_shape=jax.ShapeDtypeStruct((DEPTH, nb, 6 * D_MODEL), F32),
        compiler_params=_params(("parallel", "parallel")),
        name="modulation",
    )(c_all, w_mod, b_mod.reshape(DEPTH, 1, 6 * D_MODEL))


def _ssm_disc_kernel(ar_ref, ai_ref, ldt_ref, br_ref, bi_ref, abr_ref, abi_ref, bbr_ref, bbi_ref):
    ar = ar_ref[...]
    ai = ai_ref[...]
    dt = jnp.exp(ldt_ref[...])
    mag = jnp.exp(dt * ar)
    abr = mag * jnp.cos(dt * ai)
    abi = mag * jnp.sin(dt * ai)
    den = ar * ar + ai * ai
    fr = ((abr - 1.0) * ar + abi * ai) / den
    fi = (abi * ar - (abr - 1.0) * ai) / den
    br = br_ref[...]
    bi = bi_ref[...]
    abr_ref[...] = abr
    abi_ref[...] = abi
    bbr_ref[...] = fr * br - fi * bi
    bbi_ref[...] = fr * bi + fi * br


def _ssm_discretise(a_re, a_im, log_dt, b_re, b_im):
    ar = a_re.reshape(DEPTH, 1, N_STATE)
    ai = a_im.reshape(DEPTH, 1, N_STATE)
    ldt = jnp.repeat(log_dt, SSM_STATE, axis=-1).reshape(DEPTH, 1, N_STATE)
    br = jnp.transpose(b_re, (0, 3, 1, 2)).reshape(DEPTH, SSM_GROUP, N_STATE)
    bi = jnp.transpose(b_im, (0, 3, 1, 2)).reshape(DEPTH, SSM_GROUP, N_STATE)
    row = pl.BlockSpec((None, 1, N_STATE), lambda l: (l, 0, 0))
    mat = pl.BlockSpec((None, SSM_GROUP, N_STATE), lambda l: (l, 0, 0))
    return pl.pallas_call(
        _ssm_disc_kernel,
        grid=(DEPTH,),
        in_specs=[row, row, row, mat, mat],
        out_specs=[row, row, mat, mat],
        out_shape=[jax.ShapeDtypeStruct((DEPTH, 1, N_STATE), F32)] * 2
        + [jax.ShapeDtypeStruct((DEPTH, SSM_GROUP, N_STATE), F32)] * 2,
        compiler_params=_params(("parallel",)),
        name="ssm_discretise",
    )(ar, ai, ldt, br, bi)


def _ssm_pack(abr, abi, bbr, bbi, c_re, c_im):
    gpo = N_SSM_GROUPS // OCTETS
    eye = jnp.eye(gpo, dtype=F32)

    def oct_state(v):
        return v.reshape(OCTETS, OCT_ST)

    a8 = jnp.concatenate([oct_state(abr), oct_state(abi)], axis=1).reshape(1, 2 * N_STATE)
    a8 = jnp.broadcast_to(a8, (SUBLANES, 2 * N_STATE))

    def b_tiles(bb):
        t = bb.reshape(SSM_GROUP, OCTETS, gpo, SSM_STATE)
        t = jnp.transpose(t, (1, 2, 0, 3))
        t = t[:, :, :, None, :] * eye[None, :, None, :, None]
        return t.reshape(OCTETS, OCT_IN, OCT_ST)

    bw = jnp.concatenate([b_tiles(bbr), b_tiles(bbi)], axis=2).astype(BF16)

    def c_tiles(cc):
        t = cc.reshape(OCTETS, gpo, SSM_GROUP, SSM_STATE)
        t = jnp.transpose(t, (0, 1, 3, 2))
        t = t[:, :, :, None, :] * eye[None, :, None, :, None]
        return t.reshape(OCTETS, OCT_ST, OCT_IN)

    cw = jnp.concatenate([c_tiles(c_re), -c_tiles(c_im)], axis=1).astype(BF16)
    return a8, bw, cw


def _in_proj_kernel(x_ref, sh_ref, sc_ref, g_ref, w_ref, z_ref, kv_ref):
    y = _rms(x_ref[...], g_ref[...])
    h = (y * (1.0 + sc_ref[...]) + sh_ref[...]).astype(BF16)
    for c in range(N_IN // IN_CHUNK):
        lo = c * IN_CHUNK
        r = _dot(h, w_ref[:, lo:lo + IN_CHUNK])
        z_ref[:, lo:lo + IN_CHUNK] = r.astype(BF16)
        if lo >= OFF_K:
            kv_ref[:, lo - OFF_K:lo - OFF_K + IN_CHUNK] = r


def _in_proj(x2, mod_arr, mod_spec, g_pre, w_bf, tm):
    rows = x2.shape[0]
    return pl.pallas_call(
        _in_proj_kernel,
        grid=(rows // tm,),
        in_specs=[pl.BlockSpec((tm, D_MODEL), lambda i: (i, 0)),
                  mod_spec(0), mod_spec(1),
                  _resident((1, D_MODEL)),
                  _resident((D_MODEL, N_IN))],
        out_specs=[pl.BlockSpec((tm, N_IN), lambda i: (i, 0)),
                   pl.BlockSpec((tm, 2 * D_ATT), lambda i: (i, 0))],
        out_shape=[jax.ShapeDtypeStruct((rows, N_IN), BF16),
                   jax.ShapeDtypeStruct((rows, 2 * D_ATT), F32)],
        compiler_params=_params(("parallel",)),
        name="in_proj",
    )(x2, mod_arr, mod_arr, g_pre, w_bf)


def _conv_kernel(a_ref, prev_ref, w_ref, b_ref, lg_ref, lb_ref, y_ref, st_ref, ubuf, sh_scr,
                 *, bb, tq, nt, sub):
    i = pl.program_id(1)
    base = HALO - (CONV_WIDTH - 1)
    span = tq + HALO - SUBLANES
    nrt = max(sub // SUBLANES, 1)
    rpt = min(sub, SUBLANES)
    for b in range(bb):
        @pl.when(i == 0)
        def _():
            ubuf[b, 0:HALO, :] = prev_ref[b]

        @pl.when(i > 0)
        def _():
            ubuf[b, 0:HALO, :] = ubuf[b, tq:tq + HALO, :]

        a = a_ref[b].astype(F32)
        ubuf[b, HALO:HALO + tq, :] = a[:, :D_CONV] * _sigmoid(a[:, D_CONV:])
        for r in range(1, SUBLANES):
            sh_scr[r - 1, 0:span, :] = ubuf[b, r:r + span, :]
        for r0 in range(0, tq, sub):
            acc = jnp.zeros((nrt, rpt, D_CONV), F32) + b_ref[...]
            for j in range(CONV_WIDTH):
                off = base + j
                al = r0 + (off // SUBLANES) * SUBLANES
                if off % SUBLANES == 0:
                    src = ubuf[b, al:al + sub, :]
                else:
                    src = sh_scr[off % SUBLANES - 1, al:al + sub, :]
                acc = acc + w_ref[j, 0:rpt, :][None] * src.reshape(nrt, rpt, D_CONV)
            acc = acc.reshape(sub, D_CONV)
            mu = jnp.mean(acc, axis=-1, keepdims=True)
            xc = acc - mu
            yn = xc * lax.rsqrt(jnp.mean(xc * xc, axis=-1, keepdims=True) + LN_EPS)
            y_ref[b, r0:r0 + sub, :] = _silu(yn * lg_ref[...] + lb_ref[...]).astype(y_ref.dtype)

        @pl.when(i == nt - 1)
        def _():
            st_ref[b] = ubuf[b, base + tq:base + tq + CONV_WIDTH - 1, :]


def _conv_branch(z3, prev_pad, w_pad, b, lg, lb, bb, tq):
    nbatch, t_len, _ = z3.shape
    nt = t_len // tq
    sub = min(tq, 32)
    kern = functools.partial(_conv_kernel, bb=bb, tq=tq, nt=nt, sub=sub)
    tq_pad = max(tq, SUBLANES)
    return pl.pallas_call(
        kern,
        grid=(nbatch // bb, nt),
        in_specs=[pl.BlockSpec((bb, tq, 2 * D_CONV), lambda i, j: (i, j, OFF_CONV // (2 * D_CONV))),
                  pl.BlockSpec((bb, HALO, D_CONV), lambda i, j: (i, 0, 0)),
                  _resident((CONV_WIDTH, SUBLANES, D_CONV)),
                  _resident((1, D_CONV)), _resident((1, D_CONV)), _resident((1, D_CONV))],
        out_specs=[pl.BlockSpec((bb, tq, D_CONV), lambda i, j: (i, j, 0)),
                   pl.BlockSpec((bb, CONV_WIDTH - 1, D_CONV), lambda i, j: (i, 0, 0))],
        out_shape=[jax.ShapeDtypeStruct((nbatch, t_len, D_CONV), BF16),
                   jax.ShapeDtypeStruct((nbatch, CONV_WIDTH - 1, D_CONV), F32)],
        scratch_shapes=[pltpu.VMEM((bb, HALO + tq_pad + SUBLANES, D_CONV), F32),
                        pltpu.VMEM((SUBLANES - 1, HALO + tq_pad, D_CONV), F32)],
        compiler_params=_params(("parallel", "arbitrary")),
        name="conv_branch",
    )(z3, prev_pad, w_pad, b, lg, lb)


def _ssm_kernel(u_ref, h0r_ref, h0i_ref, a_ref, bw_ref, cw_ref, d_ref, y_ref, hr_ref, hi_ref,
                ub_scr, u_scr, s_scr, y_scr, h_scr, *, nb, tc, nt, flat):
    i = pl.program_id(0)
    nseq = 1 if flat else nb
    rows_seq = nb * tc // nseq
    st2 = 2 * OCT_ST

    @pl.when(i == 0)
    def _():
        for o in range(OCTETS):
            h_scr[:, st2 * o:st2 * o + OCT_ST] = h0r_ref[:, OCT_ST * o:OCT_ST * (o + 1)]
            h_scr[:, st2 * o + OCT_ST:st2 * (o + 1)] = h0i_ref[:, OCT_ST * o:OCT_ST * (o + 1)]

    for o in range(OCTETS):
        cols = slice(OCT_IN * o, OCT_IN * (o + 1))
        for b in range(nseq):
            src = u_ref[:, cols] if flat else u_ref[b, :, cols]
            ub_scr[o, b * rows_seq:(b + 1) * rows_seq, :] = src.astype(F32)
        for t in range(tc):
            u_scr[t * nb:(t + 1) * nb, cols] = ub_scr[o, pl.ds(t, nb, stride=tc), :]

    for o in range(OCTETS):
        ub = u_scr[:, OCT_IN * o:OCT_IN * (o + 1)].astype(BF16)
        s_scr[:, st2 * o:st2 * (o + 1)] = _dot(ub, bw_ref[o])

    for o in range(OCTETS):
        c0 = st2 * o
        re = slice(c0, c0 + OCT_ST)
        im = slice(c0 + OCT_ST, c0 + st2)
        ar = a_ref[:, re]
        ai = a_ref[:, im]

        def run(r0, hr, hi, re=re, im=im, ar=ar, ai=ai):
            for t in range(tc):
                row = r0 + t * nb
                if isinstance(row, int):
                    rows = slice(row, row + SUBLANES)
                else:
                    rows = pl.ds(pl.multiple_of(row, SUBLANES), SUBLANES)
                nr = ar * hr - ai * hi + s_scr[rows, re]
                ni = ar * hi + ai * hr + s_scr[rows, im]
                s_scr[rows, re] = nr
                s_scr[rows, im] = ni
                hr, hi = nr, ni
            return hr, hi

        if nb == SUBLANES:
            hr, hi = run(0, h_scr[:, re], h_scr[:, im])
            h_scr[:, re] = hr
            h_scr[:, im] = hi
        else:
            def rows_body(rg, carry, re=re, im=im, run=run):
                r0 = pl.multiple_of(rg * SUBLANES, SUBLANES)
                hr, hi = run(r0, h_scr[pl.ds(r0, SUBLANES), re], h_scr[pl.ds(r0, SUBLANES), im])
                h_scr[pl.ds(r0, SUBLANES), re] = hr
                h_scr[pl.ds(r0, SUBLANES), im] = hi
                return carry

            lax.fori_loop(0, nb // SUBLANES, rows_body, 0)

    for o in range(OCTETS):
        hb = s_scr[:, st2 * o:st2 * (o + 1)].astype(BF16)
        y_scr[:, OCT_IN * o:OCT_IN * (o + 1)] = _dot(hb, cw_ref[o])

    y_scr[...] = jax.nn.gelu(y_scr[...] + d_ref[...] * u_scr[...])
    for o in range(OCTETS):
        cols = slice(OCT_IN * o, OCT_IN * (o + 1))
        for t in range(tc):
            ub_scr[o, pl.ds(t, nb, stride=tc), :] = y_scr[t * nb:(t + 1) * nb, cols]
        for b in range(nseq):
            blk = ub_scr[o, b * rows_seq:(b + 1) * rows_seq, :].astype(y_ref.dtype)
            if flat:
                y_ref[:, cols] = blk
            else:
                y_ref[b, :, cols] = blk

    @pl.when(i == nt - 1)
    def _():
        for o in range(OCTETS):
            hr_ref[:, OCT_ST * o:OCT_ST * (o + 1)] = h_scr[:, st2 * o:st2 * o + OCT_ST]
            hi_ref[:, OCT_ST * o:OCT_ST * (o + 1)] = h_scr[:, st2 * o + OCT_ST:st2 * (o + 1)]


def _ssm_branch(z, h0r, h0i, a8, bw, cw, dvec, nb, t_len, tc, flat):
    nt = t_len // tc
    m = nb * tc
    kern = functools.partial(_ssm_kernel, nb=nb, tc=tc, nt=nt, flat=flat)
    if flat:
        u_spec = pl.BlockSpec((m, D_SSM), lambda i: (0, OFF_SSM // D_SSM))
        y_spec = pl.BlockSpec((m, D_SSM), lambda i: (0, 0))
        y_shape = jax.ShapeDtypeStruct((m, D_SSM), BF16)
    else:
        u_spec = pl.BlockSpec((nb, tc, D_SSM), lambda i: (0, i, OFF_SSM // D_SSM))
        y_spec = pl.BlockSpec((nb, tc, D_SSM), lambda i: (0, i, 0))
        y_shape = jax.ShapeDtypeStruct((nb, t_len, D_SSM), BF16)
    st_spec = pl.BlockSpec((nb, N_STATE), lambda i: (0, 0))
    return pl.pallas_call(
        kern,
        grid=(nt,),
        in_specs=[u_spec, st_spec, st_spec,
                  _resident((SUBLANES, 2 * N_STATE)),
                  _resident((OCTETS, OCT_IN, 2 * OCT_ST)),
                  _resident((OCTETS, 2 * OCT_ST, OCT_IN)),
                  _resident((1, D_SSM))],
        out_specs=[y_spec, st_spec, st_spec],
        out_shape=[y_shape, jax.ShapeDtypeStruct((nb, N_STATE), F32),
                   jax.ShapeDtypeStruct((nb, N_STATE), F32)],
        scratch_shapes=[pltpu.VMEM((OCTETS, m, OCT_IN), F32), pltpu.VMEM((m, D_SSM), F32),
                        pltpu.VMEM((m, 2 * N_STATE), F32), pltpu.VMEM((m, D_SSM), F32),
                        pltpu.VMEM((nb, 2 * N_STATE), F32)],
        compiler_params=_params(("arbitrary",)),
        name="ssm_branch",
    )(z, h0r, h0i, a8, bw, cw, dvec)


def _att_prompt_kernel(q0_ref, q1_ref, k0_ref, k1_ref, v0_ref, v1_ref, o_ref,
                       qkv_scr, o_scr, l_scr, bp_scr, *, t_len):
    g_id = pl.program_id(1)
    slopes = _slopes()
    lane = lax.broadcasted_iota(jnp.int32, (BAND, LANES), 1)
    low = lane < HEAD_DIM

    for n, r in enumerate((q0_ref, q1_ref, k0_ref, k1_ref, v0_ref, v1_ref)):
        qkv_scr[n] = r[...].astype(F32)

    def group(g, dil):
        nblk = t_len // dil // BAND
        qi = lax.broadcasted_iota(jnp.int32, (BAND, 2 * BAND), 0)
        kj = lax.broadcasted_iota(jnp.int32, (BAND, 2 * BAND), 1)
        dist = qi + BAND - kj
        valid = (dist >= 0) & (dist <= BAND)
        valid0 = valid & (kj >= BAND)
        distf = dist.astype(F32)
        for h in range(HEADS_PER_GROUP):
            bias = (-slopes[g * HEADS_PER_GROUP + h] * dil) * distf
            bp_scr[0, h] = jnp.where(valid0, bias, NEG)
            bp_scr[1, h] = jnp.where(valid, bias, NEG)

        def rows(start):
            if dil == 1:
                return pl.ds(start, BAND)
            return pl.ds(start, BAND, stride=dil)

        def block(blk, carry):
            r = blk // nblk
            n = blk % nblk
            start_c = r + dil * BAND * n
            start_p = r + dil * BAND * jnp.maximum(n - 1, 0)
            sel = jnp.minimum(n, 1)
            for pair in range(2):
                qp = qkv_scr[pair, rows(start_c), :] * ATT_SCALE
                kp = jnp.concatenate([qkv_scr[2 + pair, rows(start_p), :],
                                      qkv_scr[2 + pair, rows(start_c), :]], axis=0).astype(BF16)
                vp = jnp.concatenate([qkv_scr[4 + pair, rows(start_p), :],
                                      qkv_scr[4 + pair, rows(start_c), :]], axis=0).astype(BF16)
                o_h, l_h = [], []
                for hh in range(2):
                    h = 2 * pair + hh
                    qm = jnp.where(low if hh == 0 else jnp.logical_not(low), qp, 0.0).astype(BF16)
                    t = _dot_nt(qm, kp) + bp_scr[sel, h]
                    mx = jnp.max(t, axis=-1, keepdims=True)
                    p = jnp.exp(t - mx)
                    den = jnp.sum(p, axis=-1, keepdims=True)
                    pv = _dot(p.astype(BF16), vp)
                    o_h.append(pv / den)
                    l_h.append(jnp.broadcast_to(mx + jnp.log(den), (BAND, LANES)))
                o_scr[2 * g + pair, rows(start_c), :] = jnp.where(low, o_h[0], o_h[1])
                l_scr[2 * g + pair, rows(start_c), :] = jnp.where(low, l_h[0], l_h[1])
            return carry

        lax.fori_loop(0, dil * nblk, block, 0, unroll=2)

    for g, (window, dil) in enumerate(ATT_PATTERNS):
        @pl.when(g_id == g)
        def _(g=g, dil=dil):
            group(g, dil)

    @pl.when(g_id == len(ATT_PATTERNS) - 1)
    def _():
        tr = 256

        def comb(i, carry):
            r0 = pl.multiple_of(i * tr, tr)
            for pair in range(2):
                l0 = l_scr[pair, pl.ds(r0, tr), :]
                l1 = l_scr[2 + pair, pl.ds(r0, tr), :]
                l2 = l_scr[4 + pair, pl.ds(r0, tr), :]
                mx = jnp.maximum(jnp.maximum(l0, l1), l2)
                w0 = jnp.exp(l0 - mx)
                w1 = jnp.exp(l1 - mx)
                w2 = jnp.exp(l2 - mx)
                num = (w0 * o_scr[pair, pl.ds(r0, tr), :] + w1 * o_scr[2 + pair, pl.ds(r0, tr), :]
                       + w2 * o_scr[4 + pair, pl.ds(r0, tr), :])
                o_ref[pl.ds(r0, tr), LANES * pair:LANES * (pair + 1)] = (
                    num / (w0 + w1 + w2)).astype(o_ref.dtype)
            return carry

        lax.fori_loop(0, t_len // tr, comb, 0)


def _att_prompt(z3):
    nbatch, t_len, _ = z3.shape
    ng = len(ATT_PATTERNS)
    kern = functools.partial(_att_prompt_kernel, t_len=t_len)

    def spec(off, pair):
        return pl.BlockSpec((None, t_len, LANES), lambda b, g: (b, 0, off // LANES + 2 * g + pair))

    return pl.pallas_call(
        kern,
        grid=(nbatch, ng),
        in_specs=[spec(off, pair) for off in (OFF_Q, OFF_K, OFF_V) for pair in range(2)],
        out_specs=pl.BlockSpec((None, t_len, D_ATT_OUT), lambda b, g: (b, 0, 0)),
        out_shape=jax.ShapeDtypeStruct((nbatch, t_len, D_ATT_OUT), BF16),
        scratch_shapes=[pltpu.VMEM((6, t_len, LANES), F32),
                        pltpu.VMEM((2 * ng, t_len, LANES), F32),
                        pltpu.VMEM((2 * ng, t_len, LANES), F32),
                        pltpu.VMEM((2, HEADS_PER_GROUP, BAND, 2 * BAND), F32)],
        compiler_params=_params(("parallel", "arbitrary")),
        name="att_prompt",
    )(z3, z3, z3, z3, z3, z3)


QROWS = 8
ATT_S_BB = 2


def _att_sample_kernel(q_ref, kn_ref, vn_ref, k0_ref, k1_ref, k2_ref, v0_ref, v1_ref, v2_ref,
                       b0_ref, b1_ref, b2_ref, bn_ref, o_ref):
    kc_refs = (k0_ref, k1_ref, k2_ref)
    vc_refs = (v0_ref, v1_ref, v2_ref)
    bc_refs = (b0_ref, b1_ref, b2_ref)
    w = D_ATT_OUT
    head = lax.broadcasted_iota(jnp.int32, (QROWS, w), 1) // HEAD_DIM
    zpad = jnp.zeros((LANES - QROWS, w), BF16)
    for b in range(ATT_S_BB):
        rows = slice(QROWS * b, QROWS * (b + 1))
        outs, lses = [], []
        for g in range(len(ATT_PATTERNS)):
            cols = slice(g * w, (g + 1) * w)
            qg = q_ref[rows, cols] * ATT_SCALE
            qblk = jnp.concatenate([jnp.where(head == h, qg, 0.0) for h in range(HEADS_PER_GROUP)],
                                   axis=0).astype(BF16)
            s_c = _dot(qblk, kc_refs[g][b].astype(BF16))
            bias_c = bc_refs[g][...]
            t_c = jnp.where(bias_c > 0.5 * NEG, s_c + bias_c, NEG)
            kn = jnp.concatenate([kn_ref[rows, cols].astype(BF16), zpad], axis=0)
            vn = jnp.concatenate([vn_ref[rows, cols].astype(BF16), zpad], axis=0)
            bias_n = bn_ref[g]
            t_n = jnp.where(bias_n > 0.5 * NEG, _dot_nt(qblk, kn) + bias_n, NEG)
            mx = jnp.maximum(jnp.max(t_c, axis=-1, keepdims=True), jnp.max(t_n, axis=-1, keepdims=True))
            p_c = jnp.exp(t_c - mx)
            p_n = jnp.exp(t_n - mx)
            den = jnp.sum(p_c, axis=-1, keepdims=True) + jnp.sum(p_n, axis=-1, keepdims=True)
            o = _dot_nt(p_c.astype(BF16), vc_refs[g][b].astype(BF16))
            o = (o + _dot(p_n.astype(BF16), vn)) / den
            lse = jnp.broadcast_to(mx + jnp.log(den), o.shape)
            og = jnp.zeros((QROWS, w), F32)
            lg = jnp.zeros((QROWS, w), F32)
            for h in range(HEADS_PER_GROUP):
                og = jnp.where(head == h, o[QROWS * h:QROWS * (h + 1)], og)
                lg = jnp.where(head == h, lse[QROWS * h:QROWS * (h + 1)], lg)
            outs.append(og)
            lses.append(lg)
        mx = jnp.maximum(jnp.maximum(lses[0], lses[1]), lses[2])
        ws = [jnp.exp(l - mx) for l in lses]
        o_ref[rows, :] = ((ws[0] * outs[0] + ws[1] * outs[1] + ws[2] * outs[2])
                          / (ws[0] + ws[1] + ws[2])).astype(o_ref.dtype)


def _att_sample_consts(t_new):
    slopes = _slopes()
    rows = HEADS_PER_GROUP * QROWS
    bcs = []
    bn = np.full((3, rows, LANES), NEG, np.float32)
    for g, (window, dil) in enumerate(ATT_PATTERNS):
        bc = np.zeros((rows, window), np.float32)
        pos = np.arange(window)
        for h in range(HEADS_PER_GROUP):
            s = slopes[g * HEADS_PER_GROUP + h]
            for j in range(QROWS):
                r = QROWS * h + j
                if j >= t_new:
                    bn[g, r, 0] = 0.0
                    continue
                dist = window + j - pos
                ok = (dist % dil == 0) & (dist <= window)
                bc[r] = np.where(ok, -s * dist, NEG)
                for i_ in range(j + 1):
                    if (j - i_) % dil == 0:
                        bn[g, r, i_] = -s * (j - i_)
        bcs.append(jnp.asarray(bc))
    return bcs, jnp.asarray(bn)


def _att_sample(z3, caches_k, caches_v, layer, nbatch, t_new):
    w = D_ATT_OUT
    bb = ATT_S_BB
    qkv = jnp.pad(z3[:, :, OFF_Q:].astype(F32), ((0, 0), (0, QROWS - t_new), (0, 0)))
    qkv = qkv.reshape(nbatch * QROWS, 3 * D_ATT)
    bcs, bn = _att_sample_consts(t_new)

    def cache_t(c):
        d, nb_, length, nh, hd = c.shape
        return jnp.transpose(c, (0, 1, 3, 4, 2)).reshape(d, nb_, nh * hd, length)

    def new_spec(part):
        return pl.BlockSpec((bb * QROWS, D_ATT), lambda i: (i, part))

    def cache_spec(length):
        return pl.BlockSpec((None, bb, w, length), lambda i: (layer, i, 0, 0))

    rows = HEADS_PER_GROUP * QROWS
    ins = [qkv, qkv, qkv] + [cache_t(c) for c in caches_k] + [cache_t(c) for c in caches_v] + bcs + [bn]
    specs = ([new_spec(0), new_spec(1), new_spec(2)]
             + [cache_spec(win) for win, _ in ATT_PATTERNS] * 2
             + [_resident((rows, win)) for win, _ in ATT_PATTERNS]
             + [_resident((3, rows, LANES))])
    out = pl.pallas_call(
        _att_sample_kernel,
        grid=(nbatch // bb,),
        in_specs=specs,
        out_specs=pl.BlockSpec((bb * QROWS, w), lambda i: (i, 0)),
        out_shape=jax.ShapeDtypeStruct((nbatch * QROWS, w), BF16),
        compiler_params=_params(("parallel",)),
        name="att_sample",
    )(*ins)
    return out.reshape(nbatch, QROWS, w)[:, :t_new].reshape(nbatch * t_new, w)


def _merge_kernel(cy_ref, zs_ref, at_ref, gt_ref, x_ref, gm_ref, shf_ref, scf_ref,
                  wco_ref, wsg_ref, wat_ref, wo_ref, gpost_ref, gpre_ref, x1_ref, hf_ref):
    a = _dot(cy_ref[...], wco_ref[...])
    merged = _sigmoid(gt_ref[:, 0:D_MODEL].astype(F32)) * a
    gl = _dot(zs_ref[...], wsg_ref[...])
    bm = gl[:, :D_MODEL] * _sigmoid(gl[:, D_MODEL:])
    merged = merged + _sigmoid(gt_ref[:, D_MODEL:2 * D_MODEL].astype(F32)) * bm
    c = _dot(at_ref[...], wat_ref[...])
    merged = merged + _sigmoid(gt_ref[:, 2 * D_MODEL:3 * D_MODEL].astype(F32)) * c
    mo = _dot(merged.astype(BF16), wo_ref[...])
    x1 = x_ref[...] + gm_ref[...] * _rms(mo, gpost_ref[...])
    x1_ref[...] = x1
    hf_ref[...] = (_rms(x1, gpre_ref[...]) * (1.0 + scf_ref[...]) + shf_ref[...]).astype(BF16)


def _merge(cy, zs, at, z2, x2, mod_arr, mod_spec, wco, wsg, wat, wo, g_post, g_pre_ffn, tm):
    rows = x2.shape[0]

    def rowspec(width):
        return pl.BlockSpec((tm, width), lambda i: (i, 0))

    return pl.pallas_call(
        _merge_kernel,
        grid=(rows // tm,),
        in_specs=[rowspec(D_CONV), rowspec(D_SSM), rowspec(D_ATT_OUT), rowspec(3 * D_MODEL),
                  rowspec(D_MODEL), mod_spec(2), mod_spec(3), mod_spec(4),
                  _resident((D_CONV, D_MODEL)), _resident((D_SSM, 2 * D_MODEL)),
                  _resident((D_ATT_OUT, D_MODEL)), _resident((D_MODEL, D_MODEL)),
                  _resident((1, D_MODEL)), _resident((1, D_MODEL))],
        out_specs=[rowspec(D_MODEL), rowspec(D_MODEL)],
        out_shape=[jax.ShapeDtypeStruct((rows, D_MODEL), F32),
                   jax.ShapeDtypeStruct((rows, D_MODEL), BF16)],
        compiler_params=_params(("parallel",)),
        name="merge",
    )(cy, zs, at, z2, x2, mod_arr, mod_arr, mod_arr, wco, wsg, wat, wo, g_post, g_pre_ffn)


FF_CHUNK = 256


def _ffn_kernel(hf_ref, x1_ref, gf_ref, w1_ref, w2_ref, gpost_ref, o_ref, acc_ref):
    hf = hf_ref[...]
    for c in range(D_FF // FF_CHUNK):
        lo = c * FF_CHUNK
        a = _dot(hf, w1_ref[:, lo:lo + FF_CHUNK])
        b = _dot(hf, w1_ref[:, D_FF + lo:D_FF + lo + FF_CHUNK])
        f = (_silu(a) * b).astype(BF16)
        part = _dot(f, w2_ref[lo:lo + FF_CHUNK, :])
        if c == 0:
            acc_ref[...] = part
        else:
            acc_ref[...] += part
    o_ref[...] = x1_ref[...] + gf_ref[...] * _rms(acc_ref[...], gpost_ref[...])


def _ffn(hf, x1, mod_arr, mod_spec, w1, w2, g_post, tm):
    rows = x1.shape[0]
    return pl.pallas_call(
        _ffn_kernel,
        grid=(rows // tm,),
        in_specs=[pl.BlockSpec((tm, D_MODEL), lambda i: (i, 0)),
                  pl.BlockSpec((tm, D_MODEL), lambda i: (i, 0)),
                  mod_spec(5),
                  _resident((D_MODEL, 2 * D_FF)),
                  _resident((D_FF, D_MODEL)),
                  _resident((1, D_MODEL))],
        out_specs=pl.BlockSpec((tm, D_MODEL), lambda i: (i, 0)),
        out_shape=jax.ShapeDtypeStruct((rows, D_MODEL), F32),
        scratch_shapes=[pltpu.VMEM((tm, D_MODEL), F32)],
        compiler_params=_params(("parallel",)),
        name="ffn",
    )(hf, x1, mod_arr, w1, w2, g_post)


def _layer(x2, nbatch, t_len, mod_arr, mod_specs, p, past, is_prompt, layer):
    rows = nbatch * t_len
    tm = 512
    z2, kv = _in_proj(x2, mod_arr, mod_specs(tm), p["g_pre_mix"], p["w_in"], tm)
    z3 = z2.reshape(nbatch, t_len, N_IN)

    conv_prev, h0r, h0i, caches_k, caches_v = past
    if is_prompt:
        cy, conv_state = _conv_branch(z3, conv_prev, p["conv_w"], p["conv_b"], p["conv_ln_g"],
                                      p["conv_ln_b"], bb=1, tq=256)
        zs, s_re, s_im = _ssm_branch(z3, h0r, h0i, p["a8"], p["bw"], p["cw"], p["ssm_d"],
                                     nb=nbatch, t_len=t_len, tc=64, flat=False)
        att = _att_prompt(z3).reshape(rows, D_ATT_OUT)
    else:
        cy, conv_state = _conv_branch(z3, conv_prev, p["conv_w"], p["conv_b"], p["conv_ln_g"],
                                      p["conv_ln_b"], bb=8, tq=t_len)
        zs, s_re, s_im = _ssm_branch(z2, h0r, h0i, p["a8"], p["bw"], p["cw"], p["ssm_d"],
                                     nb=nbatch, t_len=t_len, tc=t_len, flat=True)
        att = _att_sample(z3, caches_k, caches_v, layer, nbatch, t_len)

    x1, hf = _merge(cy.reshape(rows, D_CONV), zs.reshape(rows, D_SSM), att, z2, x2,
                    mod_arr, mod_specs(tm), p["w_conv_out"], p["w_ssm_glu"], p["w_att"],
                    p["w_out"], p["g_post_mix"], p["g_pre_ffn"], tm)
    x_out = _ffn(hf, x1, mod_arr, mod_specs(tm), p["w_ffn_in"], p["w_ffn_out"], p["g_post_ffn"], tm)

    kv3 = kv.reshape(nbatch, t_len, 2 * D_ATT)
    kvs = []
    for g, (window, dil) in enumerate(ATT_PATTERNS):
        keep = min(window, t_len) if is_prompt else t_len
        for off in (0, D_ATT):
            cols = kv3[:, t_len - keep:, off + g * D_ATT_OUT:off + (g + 1) * D_ATT_OUT]
            kvs.append(cols.reshape(nbatch, keep, HEADS_PER_GROUP, HEAD_DIM))
    states = (*kvs, conv_state,
              s_re.reshape(nbatch, N_SSM_GROUPS, SSM_STATE), s_im.reshape(nbatch, N_SSM_GROUPS, SSM_STATE))
    return x_out, states


def kernel(x_prompt, x_sample, cache_k0, cache_v0, cache_k1, cache_v1, cache_k2, cache_v2, state_conv, state_ssm_re, state_ssm_im, c_prompt, c_sample, w_mod, b_mod, g_pre_mix, g_post_mix, g_pre_ffn, g_post_ffn, w_in, conv_w, conv_b, conv_ln_g, conv_ln_b, w_conv_out, ssm_a_re, ssm_a_im, ssm_log_dt, ssm_b_re, ssm_b_im, ssm_c_re, ssm_c_im, ssm_d, w_ssm_glu, w_att, w_out, w_ffn_in, w_ffn_out):
    nbp, t_p, _ = x_prompt.shape
    nbs, t_s, _ = x_sample.shape
    caches_k = (cache_k0, cache_k1, cache_k2)
    caches_v = (cache_v0, cache_v1, cache_v2)

    mod = _modulation(jnp.concatenate([c_prompt, c_sample], axis=0), w_mod, b_mod)
    abr, abi, bbr, bbi = _ssm_discretise(ssm_a_re, ssm_a_im, ssm_log_dt, ssm_b_re, ssm_b_im)

    xp = x_prompt.reshape(nbp * t_p, D_MODEL)
    xs = x_sample.reshape(nbs * t_s, D_MODEL)
    n_rot = N_IN - 3 * D_MODEL
    prompt_states, sample_states = [], []
    for l in range(DEPTH):
        a8, bw, cw = _ssm_pack(abr[l], abi[l], bbr[l], bbi[l], ssm_c_re[l], ssm_c_im[l])
        p = dict(
            g_pre_mix=g_pre_mix[l][None], g_post_mix=g_post_mix[l][None],
            g_pre_ffn=g_pre_ffn[l][None], g_post_ffn=g_post_ffn[l][None],
            w_in=jnp.concatenate([w_in[l][:, n_rot:], w_in[l][:, :n_rot]], axis=1).astype(BF16),
            conv_w=jnp.broadcast_to(conv_w[l][:, None, :], (CONV_WIDTH, SUBLANES, D_CONV)),
            conv_b=conv_b[l][None], conv_ln_g=conv_ln_g[l][None], conv_ln_b=conv_ln_b[l][None],
            w_conv_out=w_conv_out[l].astype(BF16), a8=a8, bw=bw, cw=cw,
            ssm_d=ssm_d[l].reshape(1, D_SSM), w_ssm_glu=w_ssm_glu[l].astype(BF16),
            w_att=w_att[l].astype(BF16), w_out=w_out[l].astype(BF16),
            w_ffn_in=w_ffn_in[l].astype(BF16), w_ffn_out=w_ffn_out[l].astype(BF16))

        mod_p = mod[l, :nbp].reshape(nbp, 1, 6 * D_MODEL)

        def specs_p(tm, t_p=t_p):
            per_seq = t_p // tm
            return lambda c: pl.BlockSpec((None, 1, D_MODEL), lambda i: (i // per_seq, 0, c))

        past_p = (jnp.zeros((nbp, HALO, D_CONV), F32), jnp.zeros((nbp, N_STATE), F32),
                  jnp.zeros((nbp, N_STATE), F32), None, None)
        xp, st_p = _layer(xp, nbp, t_p, mod_p, specs_p, p, past_p, True, l)

        mod_s = jnp.repeat(mod[l, nbp:], t_s, axis=0)

        def specs_s(tm):
            return lambda c: pl.BlockSpec((tm, D_MODEL), lambda i: (i, c))

        past_s = (jnp.pad(state_conv[l], ((0, 0), (HALO - CONV_WIDTH + 1, 0), (0, 0))),
                  state_ssm_re[l].reshape(nbs, N_STATE), state_ssm_im[l].reshape(nbs, N_STATE),
                  caches_k, caches_v)
        xs, st_s = _layer(xs, nbs, t_s, mod_s, specs_s, p, past_s, False, l)
        prompt_states.append(st_p)
        sample_states.append(st_s)

    ps = [jnp.stack([s[i] for s in prompt_states]) for i in range(9)]
    ss = [jnp.stack([s[i] for s in sample_states]) for i in range(9)]
    return (xp.reshape(nbp, t_p, D_MODEL), xs.reshape(nbs, t_s, D_MODEL), *ps, *ss)
```

```python
import functools

import numpy as np
import jax
import jax.numpy as jnp
from jax import lax
from jax.experimental import pallas as pl
from jax.experimental.pallas import tpu as pltpu

F32 = jnp.float32
BF16 = jnp.bfloat16

D_MODEL = 1024
DEPTH = 2
D_CONV = 512
CONV_WIDTH = 31
D_SSM = 512
SSM_GROUP = 16
N_SSM_GROUPS = 32
SSM_STATE = 64
HEAD_DIM = 64
HEADS_PER_GROUP = 4
ATT_PATTERNS = ((128, 1), (512, 4), (2048, 16))
N_ATT_HEADS = 12
D_ATT = 768
D_ATT_OUT = 256
ATT_SCALE = HEAD_DIM ** -0.5
D_FF = 2816
N_IN = 6912
RMS_EPS = 1e-6
LN_EPS = 1e-5
NEG = -1e30
BAND = 128

OFF_GATES = 0
OFF_CONV = 3072
OFF_SSM = 4096
OFF_Q = 4608
OFF_K = 5376
OFF_V = 6144
IN_CHUNK = 768

LANES = 128
SUBLANES = 8
HALO = 32
N_STATE = N_SSM_GROUPS * SSM_STATE
OCTETS = 4
OCT_IN = D_SSM // OCTETS
OCT_ST = N_STATE // OCTETS
VMEM_LIMIT = 56 * 1024 * 1024


def _params(sem):
    return pltpu.CompilerParams(dimension_semantics=sem, vmem_limit_bytes=VMEM_LIMIT)


def _resident(shape):
    return pl.BlockSpec(shape, lambda *_: (0,) * len(shape), pipeline_mode=pl.Buffered(1))


def _slopes():
    return [2.0 ** (-8.0 * (i + 1) / N_ATT_HEADS) for i in range(N_ATT_HEADS)]


def _rms(x, g):
    return x * lax.rsqrt(jnp.mean(x * x, axis=-1, keepdims=True) + RMS_EPS) * g


def _sigmoid(x):
    return 0.5 * jnp.tanh(0.5 * x) + 0.5


def _silu(x):
    return x * _sigmoid(x)


def _dot(a, b):
    return jnp.dot(a, b, preferred_element_type=F32)


def _dot_nt(a, b):
    return lax.dot_general(a, b, (((1,), (1,)), ((), ())), preferred_element_type=F32)


def _mod_kernel(c_ref, w_ref, b_ref, o_ref):
    a = _silu(c_ref[...]).astype(BF16)
    o_ref[...] = _dot(a, w_ref[...].astype(BF16)) + b_ref[...]


def _modulation(c_all, w_mod, b_mod):
    nb = c_all.shape[0]
    tn = 512
    return pl.pallas_call(
        _mod_kernel,
        grid=(DEPTH, 6 * D_MODEL // tn),
        in_specs=[pl.BlockSpec((nb, D_MODEL), lambda l, j: (0, 0)),
                  pl.BlockSpec((None, D_MODEL, tn), lambda l, j: (l, 0, j)),
                  pl.BlockSpec((None, 1, tn), lambda l, j: (l, 0, j))],
        out_specs=pl.BlockSpec((None, nb, tn), lambda l, j: (l, 0, j)),
        out_shape=jax.ShapeDtypeStruct((DEPTH, nb, 6 * D_MODEL), F32),
        compiler_params=_params(("parallel", "parallel")),
        name="modulation",
    )(c_all, w_mod, b_mod.reshape(DEPTH, 1, 6 * D_MODEL))


def _ssm_disc_kernel(ar_ref, ai_ref, ldt_ref, br_ref, bi_ref, abr_ref, abi_ref, bbr_ref, bbi_ref):
    ar = ar_ref[...]
    ai = ai_ref[...]
    dt = jnp.exp(ldt_ref[...])
    mag = jnp.exp(dt * ar)
    abr = mag * jnp.cos(dt * ai)
    abi = mag * jnp.sin(dt * ai)
    den = ar * ar + ai * ai
    fr = ((abr - 1.0) * ar + abi * ai) / den
    fi = (abi * ar - (abr - 1.0) * ai) / den
    br = br_ref[...]
    bi = bi_ref[...]
    abr_ref[...] = abr
    abi_ref[...] = abi
    bbr_ref[...] = fr * br - fi * bi
    bbi_ref[...] = fr * bi + fi * br


def _ssm_discretise(a_re, a_im, log_dt, b_re, b_im):
    ar = a_re.reshape(DEPTH, 1, N_STATE)
    ai = a_im.reshape(DEPTH, 1, N_STATE)
    ldt = jnp.repeat(log_dt, SSM_STATE, axis=-1).reshape(DEPTH, 1, N_STATE)
    br = jnp.transpose(b_re, (0, 3, 1, 2)).reshape(DEPTH, SSM_GROUP, N_STATE)
    bi = jnp.transpose(b_im, (0, 3, 1, 2)).reshape(DEPTH, SSM_GROUP, N_STATE)
    row = pl.BlockSpec((None, 1, N_STATE), lambda l: (l, 0, 0))
    mat = pl.BlockSpec((None, SSM_GROUP, N_STATE), lambda l: (l, 0, 0))
    return pl.pallas_call(
        _ssm_disc_kernel,
        grid=(DEPTH,),
        in_specs=[row, row, row, mat, mat],
        out_specs=[row, row, mat, mat],
        out_shape=[jax.ShapeDtypeStruct((DEPTH, 1, N_STATE), F32)] * 2
        + [jax.ShapeDtypeStruct((DEPTH, SSM_GROUP, N_STATE), F32)] * 2,
        compiler_params=_params(("parallel",)),
        name="ssm_discretise",
    )(ar, ai, ldt, br, bi)


def _ssm_pack(abr, abi, bbr, bbi, c_re, c_im):
    gpo = N_SSM_GROUPS // OCTETS
    eye = jnp.eye(gpo, dtype=F32)

    def oct_state(v):
        return v.reshape(OCTETS, OCT_ST)

    a8 = jnp.concatenate([oct_state(abr), oct_state(abi)], axis=1).reshape(1, 2 * N_STATE)
    a8 = jnp.broadcast_to(a8, (SUBLANES, 2 * N_STATE))

    def b_tiles(bb):
        t = bb.reshape(SSM_GROUP, OCTETS, gpo, SSM_STATE)
        t = jnp.transpose(t, (1, 2, 0, 3))
        t = t[:, :, :, None, :] * eye[None, :, None, :, None]
        return t.reshape(OCTETS, OCT_IN, OCT_ST)

    bw = jnp.concatenate([b_tiles(bbr), b_tiles(bbi)], axis=2).astype(BF16)

    def c_tiles(cc):
        t = cc.reshape(OCTETS, gpo, SSM_GROUP, SSM_STATE)
        t = jnp.transpose(t, (0, 1, 3, 2))
        t = t[:, :, :, None, :] * eye[None, :, None, :, None]
        return t.reshape(OCTETS, OCT_ST, OCT_IN)

    cw = jnp.concatenate([c_tiles(c_re), -c_tiles(c_im)], axis=1).astype(BF16)
    return a8, bw, cw


def _in_proj_kernel(x_ref, sh_ref, sc_ref, g_ref, w_ref, z_ref, kv_ref):
    y = _rms(x_ref[...], g_ref[...])
    h = (y * (1.0 + sc_ref[...]) + sh_ref[...]).astype(BF16)
    for c in range(N_IN // IN_CHUNK):
        lo = c * IN_CHUNK
        r = _dot(h, w_ref[:, lo:lo + IN_CHUNK])
        z_ref[:, lo:lo + IN_CHUNK] = r.astype(BF16)
        if lo >= OFF_K:
            kv_ref[:, lo - OFF_K:lo - OFF_K + IN_CHUNK] = r


def _in_proj(x2, mod_arr, mod_spec, g_pre, w_bf, tm):
    rows = x2.shape[0]
    return pl.pallas_call(
        _in_proj_kernel,
        grid=(rows // tm,),
        in_specs=[pl.BlockSpec((tm, D_MODEL), lambda i: (i, 0)),
                  mod_spec(0), mod_spec(1),
                  _resident((1, D_MODEL)),
                  _resident((D_MODEL, N_IN))],
        out_specs=[pl.BlockSpec((tm, N_IN), lambda i: (i, 0)),
                   pl.BlockSpec((tm, 2 * D_ATT), lambda i: (i, 0))],
        out_shape=[jax.ShapeDtypeStruct((rows, N_IN), BF16),
                   jax.ShapeDtypeStruct((rows, 2 * D_ATT), F32)],
        compiler_params=_params(("parallel",)),
        name="in_proj",
    )(x2, mod_arr, mod_arr, g_pre, w_bf)


def _in_proj_prompt_kernel(*refs, keeps, tm, n_alias):
    x_ref, sh_ref, sc_ref, g_ref, w_ref = refs[:5]
    z_ref = refs[5 + n_alias]
    kvt_refs = refs[6 + n_alias:]
    y = _rms(x_ref[...], g_ref[...])
    h = (y * (1.0 + sc_ref[...]) + sh_ref[...]).astype(BF16)
    for c in range(N_IN // IN_CHUNK):
        lo = c * IN_CHUNK
        r = _dot(h, w_ref[:, lo:lo + IN_CHUNK])
        z_ref[:, lo:lo + IN_CHUNK] = r.astype(BF16)
        if lo >= OFF_K:
            which = (lo - OFF_K) // D_ATT
            for g, keep in enumerate(keeps):
                kept = min(keep, tm)
                kvt_refs[2 * g + which][...] = r[tm - kept:, g * D_ATT_OUT:(g + 1) * D_ATT_OUT].T


def _in_proj_prompt(x2, mod_arr, mod_spec, g_pre, w_bf, tm, nbatch, t_len, layer, prev_kvt):
    rows = x2.shape[0]
    per_seq = t_len // tm
    keeps = tuple(min(window, t_len) for window, _ in ATT_PATTERNS)
    n_alias = 0 if prev_kvt is None else len(prev_kvt)
    kern = functools.partial(_in_proj_prompt_kernel, keeps=keeps, tm=tm, n_alias=n_alias)
    kvt_specs, kvt_shapes = [], []
    for keep in keeps:
        first = per_seq - max(keep // tm, 1)
        kept = min(keep, tm)
        for _ in range(2):
            kvt_specs.append(pl.BlockSpec(
                (None, None, D_ATT_OUT, kept),
                lambda i, first=first: (layer, i // per_seq, 0, jnp.maximum(i % per_seq - first, 0))))
            kvt_shapes.append(jax.ShapeDtypeStruct((DEPTH, nbatch, D_ATT_OUT, keep), F32))
    outs = pl.pallas_call(
        kern,
        grid=(rows // tm,),
        in_specs=[pl.BlockSpec((tm, D_MODEL), lambda i: (i, 0)),
                  mod_spec(0), mod_spec(1),
                  _resident((1, D_MODEL)),
                  _resident((D_MODEL, N_IN))] + [pl.BlockSpec(memory_space=pl.ANY)] * n_alias,
        out_specs=[pl.BlockSpec((tm, N_IN), lambda i: (i, 0))] + kvt_specs,
        out_shape=[jax.ShapeDtypeStruct((rows, N_IN), BF16)] + kvt_shapes,
        input_output_aliases={5 + k: 1 + k for k in range(n_alias)},
        compiler_params=_params(("arbitrary",)),
        name="in_proj_prompt",
    )(x2, mod_arr, mod_arr, g_pre, w_bf, *(prev_kvt or ()))
    return outs[0], list(outs[1:])


def _conv_kernel(a_ref, prev_ref, w_ref, b_ref, lg_ref, lb_ref, y_ref, st_ref, ubuf, sh_scr,
                 *, bb, tq, nt, sub):
    i = pl.program_id(1)
    base = HALO - (CONV_WIDTH - 1)
    span = tq + HALO - SUBLANES
    nrt = max(sub // SUBLANES, 1)
    rpt = min(sub, SUBLANES)
    for b in range(bb):
        @pl.when(i == 0)
        def _():
            ubuf[b, 0:HALO, :] = prev_ref[b]

        @pl.when(i > 0)
        def _():
            ubuf[b, 0:HALO, :] = ubuf[b, tq:tq + HALO, :]

        a = a_ref[b].astype(F32)
        ubuf[b, HALO:HALO + tq, :] = a[:, :D_CONV] * _sigmoid(a[:, D_CONV:])
        for r in range(1, SUBLANES):
            sh_scr[r - 1, 0:span, :] = ubuf[b, r:r + span, :]
        for r0 in range(0, tq, sub):
            acc = jnp.zeros((nrt, rpt, D_CONV), F32) + b_ref[...]
            for j in range(CONV_WIDTH):
                off = base + j
                al = r0 + (off // SUBLANES) * SUBLANES
                if off % SUBLANES == 0:
                    src = ubuf[b, al:al + sub, :]
                else:
                    src = sh_scr[off % SUBLANES - 1, al:al + sub, :]
                acc = acc + w_ref[j, 0:rpt, :][None] * src.reshape(nrt, rpt, D_CONV)
            acc = acc.reshape(sub, D_CONV)
            mu = jnp.mean(acc, axis=-1, keepdims=True)
            xc = acc - mu
            yn = xc * lax.rsqrt(jnp.mean(xc * xc, axis=-1, keepdims=True) + LN_EPS)
            y_ref[b, r0:r0 + sub, :] = _silu(yn * lg_ref[...] + lb_ref[...]).astype(y_ref.dtype)

        @pl.when(i == nt - 1)
        def _():
            st_ref[b] = ubuf[b, base + tq:base + tq + CONV_WIDTH - 1, :]


def _conv_branch(z3, prev_pad, w_pad, b, lg, lb, bb, tq):
    nbatch, t_len, _ = z3.shape
    nt = t_len // tq
    sub = min(tq, 32)
    kern = functools.partial(_conv_kernel, bb=bb, tq=tq, nt=nt, sub=sub)
    tq_pad = max(tq, SUBLANES)
    return pl.pallas_call(
        kern,
        grid=(nbatch // bb, nt),
        in_specs=[pl.BlockSpec((bb, tq, 2 * D_CONV), lambda i, j: (i, j, OFF_CONV // (2 * D_CONV))),
                  pl.BlockSpec((bb, HALO, D_CONV), lambda i, j: (i, 0, 0)),
                  _resident((CONV_WIDTH, SUBLANES, D_CONV)),
                  _resident((1, D_CONV)), _resident((1, D_CONV)), _resident((1, D_CONV))],
        out_specs=[pl.BlockSpec((bb, tq, D_CONV), lambda i, j: (i, j, 0)),
                   pl.BlockSpec((bb, CONV_WIDTH - 1, D_CONV), lambda i, j: (i, 0, 0))],
        out_shape=[jax.ShapeDtypeStruct((nbatch, t_len, D_CONV), BF16),
                   jax.ShapeDtypeStruct((nbatch, CONV_WIDTH - 1, D_CONV), F32)],
        scratch_shapes=[pltpu.VMEM((bb, HALO + tq_pad + SUBLANES, D_CONV), F32),
                        pltpu.VMEM((SUBLANES - 1, HALO + tq_pad, D_CONV), F32)],
        compiler_params=_params(("parallel", "arbitrary")),
        name="conv_branch",
    )(z3, prev_pad, w_pad, b, lg, lb)


def _ssm_kernel(u_ref, h0r_ref, h0i_ref, a_ref, bw_ref, cw_ref, d_ref, y_ref, hr_ref, hi_ref,
                ub_scr, u_scr, s_scr, y_scr, h_scr, *, nb, tc, nt, flat):
    i = pl.program_id(0)
    nseq = 1 if flat else nb
    rows_seq = nb * tc // nseq
    st2 = 2 * OCT_ST

    @pl.when(i == 0)
    def _():
        for o in range(OCTETS):
            h_scr[:, st2 * o:st2 * o + OCT_ST] = h0r_ref[:, OCT_ST * o:OCT_ST * (o + 1)]
            h_scr[:, st2 * o + OCT_ST:st2 * (o + 1)] = h0i_ref[:, OCT_ST * o:OCT_ST * (o + 1)]

    for o in range(OCTETS):
        cols = slice(OCT_IN * o, OCT_IN * (o + 1))
        for b in range(nseq):
            src = u_ref[:, cols] if flat else u_ref[b, :, cols]
            ub_scr[o, b * rows_seq:(b + 1) * rows_seq, :] = src.astype(F32)
        for t in range(tc):
            u_scr[t * nb:(t + 1) * nb, cols] = ub_scr[o, pl.ds(t, nb, stride=tc), :]

    for o in range(OCTETS):
        ub = u_scr[:, OCT_IN * o:OCT_IN * (o + 1)].astype(BF16)
        s_scr[:, st2 * o:st2 * (o + 1)] = _dot(ub, bw_ref[o])

    for o in range(OCTETS):
        c0 = st2 * o
        re = slice(c0, c0 + OCT_ST)
        im = slice(c0 + OCT_ST, c0 + st2)
        ar = a_ref[:, re]
        ai = a_ref[:, im]

        def run(r0, hr, hi, re=re, im=im, ar=ar, ai=ai):
            for t in range(tc):
                row = r0 + t * nb
                if isinstance(row, int):
                    rows = slice(row, row + SUBLANES)
                else:
                    rows = pl.ds(pl.multiple_of(row, SUBLANES), SUBLANES)
                nr = ar * hr - ai * hi + s_scr[rows, re]
                ni = ar * hi + ai * hr + s_scr[rows, im]
                s_scr[rows, re] = nr
                s_scr[rows, im] = ni
                hr, hi = nr, ni
            return hr, hi

        if nb == SUBLANES:
            hr, hi = run(0, h_scr[:, re], h_scr[:, im])
            h_scr[:, re] = hr
            h_scr[:, im] = hi
        else:
            def rows_body(rg, carry, re=re, im=im, run=run):
                r0 = pl.multiple_of(rg * SUBLANES, SUBLANES)
                hr, hi = run(r0, h_scr[pl.ds(r0, SUBLANES), re], h_scr[pl.ds(r0, SUBLANES), im])
                h_scr[pl.ds(r0, SUBLANES), re] = hr
                h_scr[pl.ds(r0, SUBLANES), im] = hi
                return carry

            lax.fori_loop(0, nb // SUBLANES, rows_body, 0)

    for o in range(OCTETS):
        hb = s_scr[:, st2 * o:st2 * (o + 1)].astype(BF16)
        y_scr[:, OCT_IN * o:OCT_IN * (o + 1)] = _dot(hb, cw_ref[o])

    y_scr[...] = jax.nn.gelu(y_scr[...] + d_ref[...] * u_scr[...])
    for o in range(OCTETS):
        cols = slice(OCT_IN * o, OCT_IN * (o + 1))
        for t in range(tc):
            ub_scr[o, pl.ds(t, nb, stride=tc), :] = y_scr[t * nb:(t + 1) * nb, cols]
        for b in range(nseq):
            blk = ub_scr[o, b * rows_seq:(b + 1) * rows_seq, :].astype(y_ref.dtype)
            if flat:
                y_ref[:, cols] = blk
            else:
                y_ref[b, :, cols] = blk

    @pl.when(i == nt - 1)
    def _():
        for o in range(OCTETS):
            hr_ref[:, OCT_ST * o:OCT_ST * (o + 1)] = h_scr[:, st2 * o:st2 * o + OCT_ST]
            hi_ref[:, OCT_ST * o:OCT_ST * (o + 1)] = h_scr[:, st2 * o + OCT_ST:st2 * (o + 1)]


def _ssm_branch(z, h0r, h0i, a8, bw, cw, dvec, nb, t_len, tc, flat):
    nt = t_len // tc
    m = nb * tc
    kern = functools.partial(_ssm_kernel, nb=nb, tc=tc, nt=nt, flat=flat)
    if flat:
        u_spec = pl.BlockSpec((m, D_SSM), lambda i: (0, OFF_SSM // D_SSM))
        y_spec = pl.BlockSpec((m, D_SSM), lambda i: (0, 0))
        y_shape = jax.ShapeDtypeStruct((m, D_SSM), BF16)
    else:
        u_spec = pl.BlockSpec((nb, tc, D_SSM), lambda i: (0, i, OFF_SSM // D_SSM))
        y_spec = pl.BlockSpec((nb, tc, D_SSM), lambda i: (0, i, 0))
        y_shape = jax.ShapeDtypeStruct((nb, t_len, D_SSM), BF16)
    st_spec = pl.BlockSpec((nb, N_STATE), lambda i: (0, 0))
    return pl.pallas_call(
        kern,
        grid=(nt,),
        in_specs=[u_spec, st_spec, st_spec,
                  _resident((SUBLANES, 2 * N_STATE)),
                  _resident((OCTETS, OCT_IN, 2 * OCT_ST)),
                  _resident((OCTETS, 2 * OCT_ST, OCT_IN)),
                  _resident((1, D_SSM))],
        out_specs=[y_spec, st_spec, st_spec],
        out_shape=[y_shape, jax.ShapeDtypeStruct((nb, N_STATE), F32),
                   jax.ShapeDtypeStruct((nb, N_STATE), F32)],
        scratch_shapes=[pltpu.VMEM((OCTETS, m, OCT_IN), F32), pltpu.VMEM((m, D_SSM), F32),
                        pltpu.VMEM((m, 2 * N_STATE), F32), pltpu.VMEM((m, D_SSM), F32),
                        pltpu.VMEM((nb, 2 * N_STATE), F32)],
        compiler_params=_params(("arbitrary",)),
        name="ssm_branch",
    )(z, h0r, h0i, a8, bw, cw, dvec)


def _att_prompt_kernel(q0_ref, q1_ref, k0_ref, k1_ref, v0_ref, v1_ref, o_ref,
                       qkv_scr, o_scr, l_scr, bp_scr, *, t_len):
    g_id = pl.program_id(1)
    slopes = _slopes()
    lane = lax.broadcasted_iota(jnp.int32, (BAND, LANES), 1)
    low = lane < HEAD_DIM

    for n, r in enumerate((q0_ref, q1_ref, k0_ref, k1_ref, v0_ref, v1_ref)):
        qkv_scr[n] = r[...].astype(F32)

    def group(g, dil):
        nblk = t_len // dil // BAND
        qi = lax.broadcasted_iota(jnp.int32, (BAND, 2 * BAND), 0)
        kj = lax.broadcasted_iota(jnp.int32, (BAND, 2 * BAND), 1)
        dist = qi + BAND - kj
        valid = (dist >= 0) & (dist <= BAND)
        valid0 = valid & (kj >= BAND)
        distf = dist.astype(F32)
        for h in range(HEADS_PER_GROUP):
            bias = (-slopes[g * HEADS_PER_GROUP + h] * dil) * distf
            bp_scr[0, h] = jnp.where(valid0, bias, NEG)
            bp_scr[1, h] = jnp.where(valid, bias, NEG)

        def rows(start):
            if dil == 1:
                return pl.ds(start, BAND)
            return pl.ds(start, BAND, stride=dil)

        def block(blk, carry):
            r = blk // nblk
            n = blk % nblk
            start_c = r + dil * BAND * n
            start_p = r + dil * BAND * jnp.maximum(n - 1, 0)
            sel = jnp.minimum(n, 1)
            for pair in range(2):
                qp = qkv_scr[pair, rows(start_c), :] * ATT_SCALE
                kp = jnp.concatenate([qkv_scr[2 + pair, rows(start_p), :],
                                      qkv_scr[2 + pair, rows(start_c), :]], axis=0).astype(BF16)
                vp = jnp.concatenate([qkv_scr[4 + pair, rows(start_p), :],
                                      qkv_scr[4 + pair, rows(start_c), :]], axis=0).astype(BF16)
                o_h, l_h = [], []
                for hh in range(2):
                    h = 2 * pair + hh
                    qm = jnp.where(low if hh == 0 else jnp.logical_not(low), qp, 0.0).astype(BF16)
                    t = _dot_nt(qm, kp) + bp_scr[sel, h]
                    mx = jnp.max(t, axis=-1, keepdims=True)
                    p = jnp.exp(t - mx)
                    den = jnp.sum(p, axis=-1, keepdims=True)
                    pv = _dot(p.astype(BF16), vp)
                    o_h.append(pv / den)
                    l_h.append(jnp.broadcast_to(mx + jnp.log(den), (BAND, LANES)))
                o_scr[2 * g + pair, rows(start_c), :] = jnp.where(low, o_h[0], o_h[1])
                l_scr[2 * g + pair, rows(start_c), :] = jnp.where(low, l_h[0], l_h[1])
            return carry

        lax.fori_loop(0, dil * nblk, block, 0, unroll=2)

    for g, (window, dil) in enumerate(ATT_PATTERNS):
        @pl.when(g_id == g)
        def _(g=g, dil=dil):
            group(g, dil)

    @pl.when(g_id == len(ATT_PATTERNS) - 1)
    def _():
        tr = 256

        def comb(i, carry):
            r0 = pl.multiple_of(i * tr, tr)
            for pair in range(2):
                l0 = l_scr[pair, pl.ds(r0, tr), :]
                l1 = l_scr[2 + pair, pl.ds(r0, tr), :]
                l2 = l_scr[4 + pair, pl.ds(r0, tr), :]
                mx = jnp.maximum(jnp.maximum(l0, l1), l2)
                w0 = jnp.exp(l0 - mx)
                w1 = jnp.exp(l1 - mx)
                w2 = jnp.exp(l2 - mx)
                num = (w0 * o_scr[pair, pl.ds(r0, tr), :] + w1 * o_scr[2 + pair, pl.ds(r0, tr), :]
                       + w2 * o_scr[4 + pair, pl.ds(r0, tr), :])
                o_ref[pl.ds(r0, tr), LANES * pair:LANES * (pair + 1)] = (
                    num / (w0 + w1 + w2)).astype(o_ref.dtype)
            return carry

        lax.fori_loop(0, t_len // tr, comb, 0)


def _att_prompt(z3):
    nbatch, t_len, _ = z3.shape
    ng = len(ATT_PATTERNS)
    kern = functools.partial(_att_prompt_kernel, t_len=t_len)

    def spec(off, pair):
        return pl.BlockSpec((None, t_len, LANES), lambda b, g: (b, 0, off // LANES + 2 * g + pair))

    return pl.pallas_call(
        kern,
        grid=(nbatch, ng),
        in_specs=[spec(off, pair) for off in (OFF_Q, OFF_K, OFF_V) for pair in range(2)],
        out_specs=pl.BlockSpec((None, t_len, D_ATT_OUT), lambda b, g: (b, 0, 0)),
        out_shape=jax.ShapeDtypeStruct((nbatch, t_len, D_ATT_OUT), BF16),
        scratch_shapes=[pltpu.VMEM((6, t_len, LANES), F32),
                        pltpu.VMEM((2 * ng, t_len, LANES), F32),
                        pltpu.VMEM((2 * ng, t_len, LANES), F32),
                        pltpu.VMEM((2, HEADS_PER_GROUP, BAND, 2 * BAND), F32)],
        compiler_params=_params(("parallel", "arbitrary")),
        name="att_prompt",
    )(z3, z3, z3, z3, z3, z3)


QROWS = 8
ATT_S_BB = 2


def _att_sample_kernel(q_ref, kn_ref, vn_ref, k0_ref, k1_ref, k2_ref, v0_ref, v1_ref, v2_ref,
                       b0_ref, b1_ref, b2_ref, bn_ref, o_ref):
    kc_refs = (k0_ref, k1_ref, k2_ref)
    vc_refs = (v0_ref, v1_ref, v2_ref)
    bc_refs = (b0_ref, b1_ref, b2_ref)
    w = D_ATT_OUT
    head = lax.broadcasted_iota(jnp.int32, (QROWS, w), 1) // HEAD_DIM
    zpad = jnp.zeros((LANES - QROWS, w), BF16)
    for b in range(ATT_S_BB):
        rows = slice(QROWS * b, QROWS * (b + 1))
        outs, lses = [], []
        for g in range(len(ATT_PATTERNS)):
            cols = slice(g * w, (g + 1) * w)
            qg = q_ref[rows, cols] * ATT_SCALE
            qblk = jnp.concatenate([jnp.where(head == h, qg, 0.0) for h in range(HEADS_PER_GROUP)],
                                   axis=0).astype(BF16)
            s_c = _dot(qblk, kc_refs[g][b].astype(BF16))
            bias_c = bc_refs[g][...]
            t_c = jnp.where(bias_c > 0.5 * NEG, s_c + bias_c, NEG)
            kn = jnp.concatenate([kn_ref[rows, cols].astype(BF16), zpad], axis=0)
            vn = jnp.concatenate([vn_ref[rows, cols].astype(BF16), zpad], axis=0)
            bias_n = bn_ref[g]
            t_n = jnp.where(bias_n > 0.5 * NEG, _dot_nt(qblk, kn) + bias_n, NEG)
            mx = jnp.maximum(jnp.max(t_c, axis=-1, keepdims=True), jnp.max(t_n, axis=-1, keepdims=True))
            p_c = jnp.exp(t_c - mx)
            p_n = jnp.exp(t_n - mx)
            den = jnp.sum(p_c, axis=-1, keepdims=True) + jnp.sum(p_n, axis=-1, keepdims=True)
            o = _dot_nt(p_c.astype(BF16), vc_refs[g][b].astype(BF16))
            o = (o + _dot(p_n.astype(BF16), vn)) / den
            lse = jnp.broadcast_to(mx + jnp.log(den), o.shape)
            og = jnp.zeros((QROWS, w), F32)
            lg = jnp.zeros((QROWS, w), F32)
            for h in range(HEADS_PER_GROUP):
                og = jnp.where(head == h, o[QROWS * h:QROWS * (h + 1)], og)
                lg = jnp.where(head == h, lse[QROWS * h:QROWS * (h + 1)], lg)
            outs.append(og)
            lses.append(lg)
        mx = jnp.maximum(jnp.maximum(lses[0], lses[1]), lses[2])
        ws = [jnp.exp(l - mx) for l in lses]
        o_ref[rows, :] = ((ws[0] * outs[0] + ws[1] * outs[1] + ws[2] * outs[2])
                          / (ws[0] + ws[1] + ws[2])).astype(o_ref.dtype)


def _att_sample_consts(t_new):
    slopes = _slopes()
    rows = HEADS_PER_GROUP * QROWS
    bcs = []
    bn = np.full((3, rows, LANES), NEG, np.float32)
    for g, (window, dil) in enumerate(ATT_PATTERNS):
        bc = np.zeros((rows, window), np.float32)
        pos = np.arange(window)
        for h in range(HEADS_PER_GROUP):
            s = slopes[g * HEADS_PER_GROUP + h]
            for j in range(QROWS):
                r = QROWS * h + j
                if j >= t_new:
                    bn[g, r, 0] = 0.0
                    continue
                dist = window + j - pos
                ok = (dist % dil == 0) & (dist <= window)
                bc[r] = np.where(ok, -s * dist, NEG)
                for i_ in range(j + 1):
                    if (j - i_) % dil == 0:
                        bn[g, r, i_] = -s * (j - i_)
        bcs.append(jnp.asarray(bc))
    return bcs, jnp.asarray(bn)


def _att_sample(z3, caches_k, caches_v, layer, nbatch, t_new):
    w = D_ATT_OUT
    bb = ATT_S_BB
    qkv = jnp.pad(z3[:, :, OFF_Q:].astype(F32), ((0, 0), (0, QROWS - t_new), (0, 0)))
    qkv = qkv.reshape(nbatch * QROWS, 3 * D_ATT)
    bcs, bn = _att_sample_consts(t_new)

    def cache_t(c):
        d, nb_, length, nh, hd = c.shape
        return jnp.transpose(c, (0, 1, 3, 4, 2)).reshape(d, nb_, nh * hd, length)

    def new_spec(part):
        return pl.BlockSpec((bb * QROWS, D_ATT), lambda i: (i, part))

    def cache_spec(length):
        return pl.BlockSpec((None, bb, w, length), lambda i: (layer, i, 0, 0))

    rows = HEADS_PER_GROUP * QROWS
    ins = [qkv, qkv, qkv] + [cache_t(c) for c in caches_k] + [cache_t(c) for c in caches_v] + bcs + [bn]
    specs = ([new_spec(0), new_spec(1), new_spec(2)]
             + [cache_spec(win) for win, _ in ATT_PATTERNS] * 2
             + [_resident((rows, win)) for win, _ in ATT_PATTERNS]
             + [_resident((3, rows, LANES))])
    out = pl.pallas_call(
        _att_sample_kernel,
        grid=(nbatch // bb,),
        in_specs=specs,
        out_specs=pl.BlockSpec((bb * QROWS, w), lambda i: (i, 0)),
        out_shape=jax.ShapeDtypeStruct((nbatch * QROWS, w), BF16),
        compiler_params=_params(("parallel",)),
        name="att_sample",
    )(*ins)
    return out.reshape(nbatch, QROWS, w)[:, :t_new].reshape(nbatch * t_new, w)


MERGE_CHUNK = 256

def _merge_kernel(cy_ref, zs_ref, at_ref, gt_ref, x_ref, gm_ref, shf_ref, scf_ref,
                  wco_ref, wsg_ref, wat_ref, wo_ref, gpost_ref, gpre_ref, x1_ref, hf_ref, mg_scr):
    cy = cy_ref[...]
    zs = zs_ref[...]
    at = at_ref[...]
    for c in range(D_MODEL // MERGE_CHUNK):
        lo = c * MERGE_CHUNK
        cols = slice(lo, lo + MERGE_CHUNK)
        a = _dot(cy, wco_ref[:, cols])
        bm = _dot(zs, wsg_ref[:, cols]) * _sigmoid(_dot(zs, wsg_ref[:, D_MODEL + lo:D_MODEL + lo + MERGE_CHUNK]))
        cc = _dot(at, wat_ref[:, cols])
        merged = (_sigmoid(gt_ref[:, cols].astype(F32)) * a
                  + _sigmoid(gt_ref[:, D_MODEL + lo:D_MODEL + lo + MERGE_CHUNK].astype(F32)) * bm
                  + _sigmoid(gt_ref[:, 2 * D_MODEL + lo:2 * D_MODEL + lo + MERGE_CHUNK].astype(F32)) * cc)
        mg_scr[:, cols] = merged.astype(BF16)
    mo = _dot(mg_scr[...], wo_ref[...])
    x1 = x_ref[...] + gm_ref[...] * _rms(mo, gpost_ref[...])
    x1_ref[...] = x1
    hf_ref[...] = (_rms(x1, gpre_ref[...]) * (1.0 + scf_ref[...]) + shf_ref[...]).astype(BF16)


def _merge(cy, zs, at, z2, x2, mod_arr, mod_spec, wco, wsg, wat, wo, g_post, g_pre_ffn, tm):
    rows = x2.shape[0]

    def rowspec(width):
        return pl.BlockSpec((tm, width), lambda i: (i, 0))

    return pl.pallas_call(
        _merge_kernel,
        grid=(rows // tm,),
        in_specs=[rowspec(D_CONV), rowspec(D_SSM), rowspec(D_ATT_OUT), rowspec(3 * D_MODEL),
                  rowspec(D_MODEL), mod_spec(2), mod_spec(3), mod_spec(4),
                  _resident((D_CONV, D_MODEL)), _resident((D_SSM, 2 * D_MODEL)),
                  _resident((D_ATT_OUT, D_MODEL)), _resident((D_MODEL, D_MODEL)),
                  _resident((1, D_MODEL)), _resident((1, D_MODEL))],
        out_specs=[rowspec(D_MODEL), rowspec(D_MODEL)],
        out_shape=[jax.ShapeDtypeStruct((rows, D_MODEL), F32),
                   jax.ShapeDtypeStruct((rows, D_MODEL), BF16)],
        scratch_shapes=[pltpu.VMEM((tm, D_MODEL), BF16)],
        compiler_params=_params(("parallel",)),
        name="merge",
    )(cy, zs, at, z2, x2, mod_arr, mod_arr, mod_arr, wco, wsg, wat, wo, g_post, g_pre_ffn)


FF_CHUNK = 256


def _ffn_kernel(hf_ref, x1_ref, gf_ref, w1_ref, w2_ref, gpost_ref, o_ref, acc_ref):
    hf = hf_ref[...]
    for c in range(D_FF // FF_CHUNK):
        lo = c * FF_CHUNK
        a = _dot(hf, w1_ref[:, lo:lo + FF_CHUNK])
        b = _dot(hf, w1_ref[:, D_FF + lo:D_FF + lo + FF_CHUNK])
        f = (_silu(a) * b).astype(BF16)
        part = _dot(f, w2_ref[lo:lo + FF_CHUNK, :])
        if c == 0:
            acc_ref[...] = part
        else:
            acc_ref[...] += part
    o_ref[...] = x1_ref[...] + gf_ref[...] * _rms(acc_ref[...], gpost_ref[...])


def _ffn(hf, x1, mod_arr, mod_spec, w1, w2, g_post, tm):
    rows = x1.shape[0]
    return pl.pallas_call(
        _ffn_kernel,
        grid=(rows // tm,),
        in_specs=[pl.BlockSpec((tm, D_MODEL), lambda i: (i, 0)),
                  pl.BlockSpec((tm, D_MODEL), lambda i: (i, 0)),
                  mod_spec(5),
                  _resident((D_MODEL, 2 * D_FF)),
                  _resident((D_FF, D_MODEL)),
                  _resident((1, D_MODEL))],
        out_specs=pl.BlockSpec((tm, D_MODEL), lambda i: (i, 0)),
        out_shape=jax.ShapeDtypeStruct((rows, D_MODEL), F32),
        scratch_shapes=[pltpu.VMEM((tm, D_MODEL), F32)],
        compiler_params=_params(("parallel",)),
        name="ffn",
    )(hf, x1, mod_arr, w1, w2, g_post)


def _layer(x2, nbatch, t_len, mod_arr, mod_specs, p, past, is_prompt, layer):
    rows = nbatch * t_len
    tm = 512
    conv_prev, h0r, h0i, caches_k, caches_v, prev_kvt = past
    if is_prompt:
        z2, kvs = _in_proj_prompt(x2, mod_arr, mod_specs(tm), p["g_pre_mix"], p["w_in"], tm,
                                  nbatch, t_len, layer, prev_kvt)
    else:
        z2, kv = _in_proj(x2, mod_arr, mod_specs(tm), p["g_pre_mix"], p["w_in"], tm)
        kv3 = kv.reshape(nbatch, t_len, 2 * D_ATT)
        kvs = []
        for g in range(len(ATT_PATTERNS)):
            for off in (0, D_ATT):
                cols = kv3[:, :, off + g * D_ATT_OUT:off + (g + 1) * D_ATT_OUT]
                kvs.append(cols.reshape(nbatch, t_len, HEADS_PER_GROUP, HEAD_DIM))
    z3 = z2.reshape(nbatch, t_len, N_IN)

    if is_prompt:
        cy, conv_state = _conv_branch(z3, conv_prev, p["conv_w"], p["conv_b"], p["conv_ln_g"],
                                      p["conv_ln_b"], bb=1, tq=256)
        zs, s_re, s_im = _ssm_branch(z3, h0r, h0i, p["a8"], p["bw"], p["cw"], p["ssm_d"],
                                     nb=nbatch, t_len=t_len, tc=64, flat=False)
        att = _att_prompt(z3).reshape(rows, D_ATT_OUT)
    else:
        cy, conv_state = _conv_branch(z3, conv_prev, p["conv_w"], p["conv_b"], p["conv_ln_g"],
                                      p["conv_ln_b"], bb=8, tq=t_len)
        zs, s_re, s_im = _ssm_branch(z2, h0r, h0i, p["a8"], p["bw"], p["cw"], p["ssm_d"],
                                     nb=nbatch, t_len=t_len, tc=t_len, flat=True)
        att = _att_sample(z3, caches_k, caches_v, layer, nbatch, t_len)

    x1, hf = _merge(cy.reshape(rows, D_CONV), zs.reshape(rows, D_SSM), att, z2, x2,
                    mod_arr, mod_specs(tm), p["w_conv_out"], p["w_ssm_glu"], p["w_att"],
                    p["w_out"], p["g_post_mix"], p["g_pre_ffn"], tm)
    x_out = _ffn(hf, x1, mod_arr, mod_specs(tm), p["w_ffn_in"], p["w_ffn_out"], p["g_post_ffn"], tm)

    states = (*kvs, conv_state,
              s_re.reshape(nbatch, N_SSM_GROUPS, SSM_STATE), s_im.reshape(nbatch, N_SSM_GROUPS, SSM_STATE))
    return x_out, states


def kernel(x_prompt, x_sample, cache_k0, cache_v0, cache_k1, cache_v1, cache_k2, cache_v2, state_conv, state_ssm_re, state_ssm_im, c_prompt, c_sample, w_mod, b_mod, g_pre_mix, g_post_mix, g_pre_ffn, g_post_ffn, w_in, conv_w, conv_b, conv_ln_g, conv_ln_b, w_conv_out, ssm_a_re, ssm_a_im, ssm_log_dt, ssm_b_re, ssm_b_im, ssm_c_re, ssm_c_im, ssm_d, w_ssm_glu, w_att, w_out, w_ffn_in, w_ffn_out):
    nbp, t_p, _ = x_prompt.shape
    nbs, t_s, _ = x_sample.shape
    caches_k = (cache_k0, cache_k1, cache_k2)
    caches_v = (cache_v0, cache_v1, cache_v2)

    mod = _modulation(jnp.concatenate([c_prompt, c_sample], axis=0), w_mod, b_mod)
    abr, abi, bbr, bbi = _ssm_discretise(ssm_a_re, ssm_a_im, ssm_log_dt, ssm_b_re, ssm_b_im)

    xp = x_prompt.reshape(nbp * t_p, D_MODEL)
    xs = x_sample.reshape(nbs * t_s, D_MODEL)
    n_rot = N_IN - 3 * D_MODEL
    prompt_states, sample_states = [], []
    kvt_p = None
    for l in range(DEPTH):
        a8, bw, cw = _ssm_pack(abr[l], abi[l], bbr[l], bbi[l], ssm_c_re[l], ssm_c_im[l])
        p = dict(
            g_pre_mix=g_pre_mix[l][None], g_post_mix=g_post_mix[l][None],
            g_pre_ffn=g_pre_ffn[l][None], g_post_ffn=g_post_ffn[l][None],
            w_in=jnp.concatenate([w_in[l][:, n_rot:], w_in[l][:, :n_rot]], axis=1).astype(BF16),
            conv_w=jnp.broadcast_to(conv_w[l][:, None, :], (CONV_WIDTH, SUBLANES, D_CONV)),
            conv_b=conv_b[l][None], conv_ln_g=conv_ln_g[l][None], conv_ln_b=conv_ln_b[l][None],
            w_conv_out=w_conv_out[l].astype(BF16), a8=a8, bw=bw, cw=cw,
            ssm_d=ssm_d[l].reshape(1, D_SSM), w_ssm_glu=w_ssm_glu[l].astype(BF16),
            w_att=w_att[l].astype(BF16), w_out=w_out[l].astype(BF16),
            w_ffn_in=w_ffn_in[l].astype(BF16), w_ffn_out=w_ffn_out[l].astype(BF16))

        mod_p = mod[l, :nbp].reshape(nbp, 1, 6 * D_MODEL)

        def specs_p(tm, t_p=t_p):
            per_seq = t_p // tm
            return lambda c: pl.BlockSpec((None, 1, D_MODEL), lambda i: (i // per_seq, 0, c))

        past_p = (jnp.zeros((nbp, HALO, D_CONV), F32), jnp.zeros((nbp, N_STATE), F32),
                  jnp.zeros((nbp, N_STATE), F32), None, None, kvt_p)
        xp, st_p = _layer(xp, nbp, t_p, mod_p, specs_p, p, past_p, True, l)
        kvt_p = list(st_p[:6])

        mod_s = jnp.repeat(mod[l, nbp:], t_s, axis=0)

        def specs_s(tm):
            return lambda c: pl.BlockSpec((tm, D_MODEL), lambda i: (i, c))

        past_s = (jnp.pad(state_conv[l], ((0, 0), (HALO - CONV_WIDTH + 1, 0), (0, 0))),
                  state_ssm_re[l].reshape(nbs, N_STATE), state_ssm_im[l].reshape(nbs, N_STATE),
                  caches_k, caches_v, None)
        xs, st_s = _layer(xs, nbs, t_s, mod_s, specs_s, p, past_s, False, l)
        prompt_states.append(st_p)
        sample_states.append(st_s)

    ps = [jnp.transpose(a.reshape(DEPTH, nbp, HEADS_PER_GROUP, HEAD_DIM, a.shape[-1]), (0, 1, 4, 2, 3))
          for a in kvt_p]
    ps += [jnp.stack([s[i] for s in prompt_states]) for i in range(6, 9)]
    ss = [jnp.stack([s[i] for s in sample_states]) for i in range(9)]
    return (xp.reshape(nbp, t_p, D_MODEL), xs.reshape(nbs, t_s, D_MODEL), *ps, *ss)
```

```python
import functools

import numpy as np
import jax
import jax.numpy as jnp
from jax import lax
from jax.experimental import pallas as pl
from jax.experimental.pallas import tpu as pltpu

F32 = jnp.float32
BF16 = jnp.bfloat16

D_MODEL = 1024
DEPTH = 2
D_CONV = 512
CONV_WIDTH = 31
D_SSM = 512
SSM_GROUP = 16
N_SSM_GROUPS = 32
SSM_STATE = 64
HEAD_DIM = 64
HEADS_PER_GROUP = 4
ATT_PATTERNS = ((128, 1), (512, 4), (2048, 16))
N_ATT_HEADS = 12
D_ATT = 768
D_ATT_OUT = 256
ATT_SCALE = HEAD_DIM ** -0.5
D_FF = 2816
N_IN = 6912
RMS_EPS = 1e-6
LN_EPS = 1e-5
NEG = -1e30
BAND = 128

OFF_GATES = 0
OFF_CONV = 3072
OFF_SSM = 4096
OFF_Q = 4608
OFF_K = 5376
OFF_V = 6144
IN_CHUNK = 768

LANES = 128
SUBLANES = 8
HALO = 32
N_STATE = N_SSM_GROUPS * SSM_STATE
OCTETS = 4
OCT_IN = D_SSM // OCTETS
OCT_ST = N_STATE // OCTETS
VMEM_LIMIT = 56 * 1024 * 1024


def _params(sem):
    return pltpu.CompilerParams(dimension_semantics=sem, vmem_limit_bytes=VMEM_LIMIT)


def _resident(shape, layer=None):
    if layer is None:
        return pl.BlockSpec(shape, lambda *_: (0,) * len(shape), pipeline_mode=pl.Buffered(1))
    return pl.BlockSpec((None,) + tuple(shape), lambda *_: (layer,) + (0,) * len(shape),
                        pipeline_mode=pl.Buffered(1))


def _slopes():
    return [2.0 ** (-8.0 * (i + 1) / N_ATT_HEADS) for i in range(N_ATT_HEADS)]


def _rms(x, g):
    return x * lax.rsqrt(jnp.mean(x * x, axis=-1, keepdims=True) + RMS_EPS) * g


def _sigmoid(x):
    return 0.5 * jnp.tanh(0.5 * x) + 0.5


def _silu(x):
    return x * _sigmoid(x)


def _dot(a, b):
    return jnp.dot(a, b, preferred_element_type=F32)


def _dot_nt(a, b):
    return lax.dot_general(a, b, (((1,), (1,)), ((), ())), preferred_element_type=F32)


def _mod_kernel(cp_ref, cs_ref, w_ref, b_ref, op_ref, os_ref):
    w = w_ref[...].astype(BF16)
    op_ref[...] = _dot(_silu(cp_ref[...]).astype(BF16), w) + b_ref[...]
    os_ref[...] = _dot(_silu(cs_ref[...]).astype(BF16), w) + b_ref[...]


def _modulation(c_p, c_s_rows, w_mod, b_mod):
    np_, ns = c_p.shape[0], c_s_rows.shape[0]
    tn = 512
    return pl.pallas_call(
        _mod_kernel,
        grid=(DEPTH, 6 * D_MODEL // tn),
        in_specs=[pl.BlockSpec((np_, D_MODEL), lambda l, j: (0, 0)),
                  pl.BlockSpec((ns, D_MODEL), lambda l, j: (0, 0)),
                  pl.BlockSpec((None, D_MODEL, tn), lambda l, j: (l, 0, j)),
                  pl.BlockSpec((None, 1, tn), lambda l, j: (l, 0, j))],
        out_specs=[pl.BlockSpec((None, np_, tn), lambda l, j: (l, 0, j)),
                   pl.BlockSpec((None, ns, tn), lambda l, j: (l, 0, j))],
        out_shape=[jax.ShapeDtypeStruct((DEPTH, np_, 6 * D_MODEL), F32),
                   jax.ShapeDtypeStruct((DEPTH, ns, 6 * D_MODEL), F32)],
        compiler_params=_params(("parallel", "parallel")),
        name="modulation",
    )(c_p, c_s_rows, w_mod, b_mod.reshape(DEPTH, 1, 6 * D_MODEL))


def _ssm_disc_kernel(ar_ref, ai_ref, ldt_ref, br_ref, bi_ref, abr_ref, abi_ref, bbr_ref, bbi_ref):
    ar = ar_ref[...]
    ai = ai_ref[...]
    dt = jnp.exp(ldt_ref[...])
    mag = jnp.exp(dt * ar)
    abr = mag * jnp.cos(dt * ai)
    abi = mag * jnp.sin(dt * ai)
    den = ar * ar + ai * ai
    fr = ((abr - 1.0) * ar + abi * ai) / den
    fi = (abi * ar - (abr - 1.0) * ai) / den
    br = br_ref[...]
    bi = bi_ref[...]
    abr_ref[...] = abr
    abi_ref[...] = abi
    bbr_ref[...] = fr * br - fi * bi
    bbi_ref[...] = fr * bi + fi * br


def _ssm_discretise(a_re, a_im, log_dt, b_re, b_im):
    ar = a_re.reshape(DEPTH, 1, N_STATE)
    ai = a_im.reshape(DEPTH, 1, N_STATE)
    ldt = jnp.repeat(log_dt, SSM_STATE, axis=-1).reshape(DEPTH, 1, N_STATE)
    br = jnp.transpose(b_re, (0, 3, 1, 2)).reshape(DEPTH, SSM_GROUP, N_STATE)
    bi = jnp.transpose(b_im, (0, 3, 1, 2)).reshape(DEPTH, SSM_GROUP, N_STATE)
    row = pl.BlockSpec((None, 1, N_STATE), lambda l: (l, 0, 0))
    mat = pl.BlockSpec((None, SSM_GROUP, N_STATE), lambda l: (l, 0, 0))
    return pl.pallas_call(
        _ssm_disc_kernel,
        grid=(DEPTH,),
        in_specs=[row, row, row, mat, mat],
        out_specs=[row, row, mat, mat],
        out_shape=[jax.ShapeDtypeStruct((DEPTH, 1, N_STATE), F32)] * 2
        + [jax.ShapeDtypeStruct((DEPTH, SSM_GROUP, N_STATE), F32)] * 2,
        compiler_params=_params(("parallel",)),
        name="ssm_discretise",
    )(ar, ai, ldt, br, bi)


def _ssm_pack(abr, abi, bbr, bbi, c_re, c_im):
    gpo = N_SSM_GROUPS // OCTETS
    eye = jnp.eye(gpo, dtype=F32)

    def oct_state(v):
        return v.reshape(OCTETS, OCT_ST)

    a8 = jnp.concatenate([oct_state(abr), oct_state(abi)], axis=1).reshape(1, 2 * N_STATE)
    a8 = jnp.broadcast_to(a8, (SUBLANES, 2 * N_STATE))

    def b_tiles(bb):
        t = bb.reshape(SSM_GROUP, OCTETS, gpo, SSM_STATE)
        t = jnp.transpose(t, (1, 2, 0, 3))
        t = t[:, :, :, None, :] * eye[None, :, None, :, None]
        return t.reshape(OCTETS, OCT_IN, OCT_ST)

    bw = jnp.concatenate([b_tiles(bbr), b_tiles(bbi)], axis=2).astype(BF16)

    def c_tiles(cc):
        t = cc.reshape(OCTETS, gpo, SSM_GROUP, SSM_STATE)
        t = jnp.transpose(t, (0, 1, 3, 2))
        t = t[:, :, :, None, :] * eye[None, :, None, :, None]
        return t.reshape(OCTETS, OCT_ST, OCT_IN)

    cw = jnp.concatenate([c_tiles(c_re), -c_tiles(c_im)], axis=1).astype(BF16)
    return a8, bw, cw


def _in_proj_kernel(x_ref, sh_ref, sc_ref, g_ref, w_ref, z_ref, kv_ref):
    y = _rms(x_ref[...], g_ref[...])
    h = (y * (1.0 + sc_ref[...]) + sh_ref[...]).astype(BF16)
    for c in range(N_IN // IN_CHUNK):
        lo = c * IN_CHUNK
        r = _dot(h, w_ref[:, lo:lo + IN_CHUNK])
        z_ref[:, lo:lo + IN_CHUNK] = r.astype(BF16)
        if lo >= OFF_K:
            kv_ref[:, lo - OFF_K:lo - OFF_K + IN_CHUNK] = r


def _in_proj(x2, mod_arr, mod_spec, g_pre, w_bf, tm, layer):
    rows = x2.shape[0]
    return pl.pallas_call(
        _in_proj_kernel,
        grid=(rows // tm,),
        in_specs=[pl.BlockSpec((tm, D_MODEL), lambda i: (i, 0)),
                  mod_spec(0), mod_spec(1),
                  _resident((1, D_MODEL), layer),
                  _resident((D_MODEL, N_IN), layer)],
        out_specs=[pl.BlockSpec((tm, N_IN), lambda i: (i, 0)),
                   pl.BlockSpec((tm, 2 * D_ATT), lambda i: (i, 0))],
        out_shape=[jax.ShapeDtypeStruct((rows, N_IN), BF16),
                   jax.ShapeDtypeStruct((rows, 2 * D_ATT), F32)],
        compiler_params=_params(("parallel",)),
        name="in_proj",
    )(x2, mod_arr, mod_arr, g_pre, w_bf)


def _in_proj_prompt_kernel(*refs, keeps, tm, n_alias):
    x_ref, sh_ref, sc_ref, g_ref, w_ref = refs[:5]
    z_ref = refs[5 + n_alias]
    kvt_refs = refs[6 + n_alias:]
    y = _rms(x_ref[...], g_ref[...])
    h = (y * (1.0 + sc_ref[...]) + sh_ref[...]).astype(BF16)
    for c in range(N_IN // IN_CHUNK):
        lo = c * IN_CHUNK
        r = _dot(h, w_ref[:, lo:lo + IN_CHUNK])
        z_ref[:, lo:lo + IN_CHUNK] = r.astype(BF16)
        if lo >= OFF_K:
            which = (lo - OFF_K) // D_ATT
            for g, keep in enumerate(keeps):
                kept = min(keep, tm)
                kvt_refs[2 * g + which][...] = r[tm - kept:, g * D_ATT_OUT:(g + 1) * D_ATT_OUT].T


def _in_proj_prompt(x2, mod_arr, mod_spec, g_pre, w_bf, tm, nbatch, t_len, layer, prev_kvt):
    rows = x2.shape[0]
    per_seq = t_len // tm
    keeps = tuple(min(window, t_len) for window, _ in ATT_PATTERNS)
    n_alias = 0 if prev_kvt is None else len(prev_kvt)
    kern = functools.partial(_in_proj_prompt_kernel, keeps=keeps, tm=tm, n_alias=n_alias)
    kvt_specs, kvt_shapes = [], []
    for keep in keeps:
        first = per_seq - max(keep // tm, 1)
        kept = min(keep, tm)
        for _ in range(2):
            kvt_specs.append(pl.BlockSpec(
                (None, None, D_ATT_OUT, kept),
                lambda i, first=first: (layer, i // per_seq, 0, jnp.maximum(i % per_seq - first, 0))))
            kvt_shapes.append(jax.ShapeDtypeStruct((DEPTH, nbatch, D_ATT_OUT, keep), F32))
    outs = pl.pallas_call(
        kern,
        grid=(rows // tm,),
        in_specs=[pl.BlockSpec((tm, D_MODEL), lambda i: (i, 0)),
                  mod_spec(0), mod_spec(1),
                  _resident((1, D_MODEL), layer),
                  _resident((D_MODEL, N_IN), layer)] + [pl.BlockSpec(memory_space=pl.ANY)] * n_alias,
        out_specs=[pl.BlockSpec((tm, N_IN), lambda i: (i, 0))] + kvt_specs,
        out_shape=[jax.ShapeDtypeStruct((rows, N_IN), BF16)] + kvt_shapes,
        input_output_aliases={5 + k: 1 + k for k in range(n_alias)},
        compiler_params=_params(("arbitrary",)),
        name="in_proj_prompt",
    )(x2, mod_arr, mod_arr, g_pre, w_bf, *(prev_kvt or ()))
    return outs[0], list(outs[1:])


def _conv_kernel(a_ref, prev_ref, w_ref, b_ref, lg_ref, lb_ref, y_ref, st_ref, ubuf, sh_scr,
                 *, bb, tq, nt, sub, flat):
    i = pl.program_id(1)
    base = HALO - (CONV_WIDTH - 1)
    span = tq + HALO - SUBLANES
    nrt = max(sub // SUBLANES, 1)
    rpt = min(sub, SUBLANES)
    for b in range(bb):
        @pl.when(i == 0)
        def _():
            ubuf[b, 0:HALO, :] = prev_ref[b]

        @pl.when(i > 0)
        def _():
            ubuf[b, 0:HALO, :] = ubuf[b, tq:tq + HALO, :]

        a = (a_ref[b * tq:(b + 1) * tq, :] if flat else a_ref[b]).astype(F32)
        ubuf[b, HALO:HALO + tq, :] = a[:, :D_CONV] * _sigmoid(a[:, D_CONV:])
        for r in range(1, SUBLANES):
            sh_scr[r - 1, 0:span, :] = ubuf[b, r:r + span, :]
        for r0 in range(0, tq, sub):
            acc = jnp.zeros((nrt, rpt, D_CONV), F32) + b_ref[...]
            for j in range(CONV_WIDTH):
                off = base + j
                al = r0 + (off // SUBLANES) * SUBLANES
                if off % SUBLANES == 0:
                    src = ubuf[b, al:al + sub, :]
                else:
                    src = sh_scr[off % SUBLANES - 1, al:al + sub, :]
                acc = acc + w_ref[j, 0:rpt, :][None] * src.reshape(nrt, rpt, D_CONV)
            acc = acc.reshape(sub, D_CONV)
            mu = jnp.mean(acc, axis=-1, keepdims=True)
            xc = acc - mu
            yn = xc * lax.rsqrt(jnp.mean(xc * xc, axis=-1, keepdims=True) + LN_EPS)
            yv = _silu(yn * lg_ref[...] + lb_ref[...]).astype(y_ref.dtype)
            if flat:
                y_ref[b * tq + r0:b * tq + r0 + sub, :] = yv
            else:
                y_ref[b, r0:r0 + sub, :] = yv

        @pl.when(i == nt - 1)
        def _():
            st_ref[b] = ubuf[b, base + tq:base + tq + CONV_WIDTH - 1, :]


def _conv_branch(z, prev_pad, prev_layer, w_b8, b, lg, lb, layer, nbatch, t_len, bb, tq, flat):
    nt = t_len // tq
    sub = min(tq, 32)
    kern = functools.partial(_conv_kernel, bb=bb, tq=tq, nt=nt, sub=sub, flat=flat)
    tq_pad = max(tq, SUBLANES)
    col = OFF_CONV // (2 * D_CONV)
    if flat:
        a_spec = pl.BlockSpec((bb * tq, 2 * D_CONV), lambda i, j: (i, col))
        y_spec = pl.BlockSpec((bb * tq, D_CONV), lambda i, j: (i, 0))
        y_shape = jax.ShapeDtypeStruct((nbatch * t_len, D_CONV), BF16)
    else:
        a_spec = pl.BlockSpec((bb, tq, 2 * D_CONV), lambda i, j: (i, j, col))
        y_spec = pl.BlockSpec((bb, tq, D_CONV), lambda i, j: (i, j, 0))
        y_shape = jax.ShapeDtypeStruct((nbatch, t_len, D_CONV), BF16)
    if prev_layer is None:
        prev_spec = pl.BlockSpec((bb, HALO, D_CONV), lambda i, j: (i, 0, 0))
    else:
        prev_spec = pl.BlockSpec((None, bb, HALO, D_CONV), lambda i, j: (prev_layer, i, 0, 0))
    return pl.pallas_call(
        kern,
        grid=(nbatch // bb, nt),
        in_specs=[a_spec, prev_spec,
                  _resident((CONV_WIDTH, SUBLANES, D_CONV), layer),
                  _resident((1, D_CONV), layer), _resident((1, D_CONV), layer),
                  _resident((1, D_CONV), layer)],
        out_specs=[y_spec, pl.BlockSpec((bb, CONV_WIDTH - 1, D_CONV), lambda i, j: (i, 0, 0))],
        out_shape=[y_shape, jax.ShapeDtypeStruct((nbatch, CONV_WIDTH - 1, D_CONV), F32)],
        scratch_shapes=[pltpu.VMEM((bb, HALO + tq_pad + SUBLANES, D_CONV), F32),
                        pltpu.VMEM((SUBLANES - 1, HALO + tq_pad, D_CONV), F32)],
        compiler_params=_params(("parallel", "arbitrary")),
        name="conv_branch",
    )(z, prev_pad, w_b8, b, lg, lb)


def _ssm_kernel(u_ref, h0r_ref, h0i_ref, a_ref, bw_ref, cw_ref, d_ref, y_ref, hr_ref, hi_ref,
                ub_scr, u_scr, s_scr, y_scr, h_scr, *, nb, tc, nt, flat):
    i = pl.program_id(0)
    nseq = 1 if flat else nb
    rows_seq = nb * tc // nseq
    st2 = 2 * OCT_ST

    @pl.when(i == 0)
    def _():
        for o in range(OCTETS):
            h_scr[:, st2 * o:st2 * o + OCT_ST] = h0r_ref[:, OCT_ST * o:OCT_ST * (o + 1)]
            h_scr[:, st2 * o + OCT_ST:st2 * (o + 1)] = h0i_ref[:, OCT_ST * o:OCT_ST * (o + 1)]

    for o in range(OCTETS):
        cols = slice(OCT_IN * o, OCT_IN * (o + 1))
        for b in range(nseq):
            src = u_ref[:, cols] if flat else u_ref[b, :, cols]
            ub_scr[o, b * rows_seq:(b + 1) * rows_seq, :] = src.astype(F32)
        for t in range(tc):
            u_scr[t * nb:(t + 1) * nb, cols] = ub_scr[o, pl.ds(t, nb, stride=tc), :]

    for o in range(OCTETS):
        ub = u_scr[:, OCT_IN * o:OCT_IN * (o + 1)].astype(BF16)
        s_scr[:, st2 * o:st2 * (o + 1)] = _dot(ub, bw_ref[o])

    for o in range(OCTETS):
        c0 = st2 * o
        re = slice(c0, c0 + OCT_ST)
        im = slice(c0 + OCT_ST, c0 + st2)
        ar = a_ref[:, re]
        ai = a_ref[:, im]

        def run(r0, hr, hi, re=re, im=im, ar=ar, ai=ai):
            for t in range(tc):
                row = r0 + t * nb
                if isinstance(row, int):
                    rows = slice(row, row + SUBLANES)
                else:
                    rows = pl.ds(pl.multiple_of(row, SUBLANES), SUBLANES)
                nr = ar * hr - ai * hi + s_scr[rows, re]
                ni = ar * hi + ai * hr + s_scr[rows, im]
                s_scr[rows, re] = nr
                s_scr[rows, im] = ni
                hr, hi = nr, ni
            return hr, hi

        if nb == SUBLANES:
            hr, hi = run(0, h_scr[:, re], h_scr[:, im])
            h_scr[:, re] = hr
            h_scr[:, im] = hi
        else:
            def rows_body(rg, carry, re=re, im=im, run=run):
                r0 = pl.multiple_of(rg * SUBLANES, SUBLANES)
                hr, hi = run(r0, h_scr[pl.ds(r0, SUBLANES), re], h_scr[pl.ds(r0, SUBLANES), im])
                h_scr[pl.ds(r0, SUBLANES), re] = hr
                h_scr[pl.ds(r0, SUBLANES), im] = hi
                return carry

            lax.fori_loop(0, nb // SUBLANES, rows_body, 0)

    for o in range(OCTETS):
        hb = s_scr[:, st2 * o:st2 * (o + 1)].astype(BF16)
        y_scr[:, OCT_IN * o:OCT_IN * (o + 1)] = _dot(hb, cw_ref[o])

    y_scr[...] = jax.nn.gelu(y_scr[...] + d_ref[...] * u_scr[...])
    for o in range(OCTETS):
        cols = slice(OCT_IN * o, OCT_IN * (o + 1))
        for t in range(tc):
            ub_scr[o, pl.ds(t, nb, stride=tc), :] = y_scr[t * nb:(t + 1) * nb, cols]
        for b in range(nseq):
            blk = ub_scr[o, b * rows_seq:(b + 1) * rows_seq, :].astype(y_ref.dtype)
            if flat:
                y_ref[:, cols] = blk
            else:
                y_ref[b, :, cols] = blk

    @pl.when(i == nt - 1)
    def _():
        for o in range(OCTETS):
            hr_ref[:, OCT_ST * o:OCT_ST * (o + 1)] = h_scr[:, st2 * o:st2 * o + OCT_ST]
            hi_ref[:, OCT_ST * o:OCT_ST * (o + 1)] = h_scr[:, st2 * o + OCT_ST:st2 * (o + 1)]


def _ssm_branch(z, h0r, h0i, h0_layer, a8, bw, cw, dvec, layer, nb, t_len, tc, flat):
    nt = t_len // tc
    m = nb * tc
    kern = functools.partial(_ssm_kernel, nb=nb, tc=tc, nt=nt, flat=flat)
    if flat:
        u_spec = pl.BlockSpec((m, D_SSM), lambda i: (0, OFF_SSM // D_SSM))
        y_spec = pl.BlockSpec((m, D_SSM), lambda i: (0, 0))
        y_shape = jax.ShapeDtypeStruct((m, D_SSM), BF16)
    else:
        u_spec = pl.BlockSpec((nb, tc, D_SSM), lambda i: (0, i, OFF_SSM // D_SSM))
        y_spec = pl.BlockSpec((nb, tc, D_SSM), lambda i: (0, i, 0))
        y_shape = jax.ShapeDtypeStruct((nb, t_len, D_SSM), BF16)
    st_spec = pl.BlockSpec((nb, N_STATE), lambda i: (0, 0))
    if h0_layer is None:
        h0_spec = st_spec
    else:
        h0_spec = pl.BlockSpec((None, nb, N_STATE), lambda i: (h0_layer, 0, 0))
    return pl.pallas_call(
        kern,
        grid=(nt,),
        in_specs=[u_spec, h0_spec, h0_spec,
                  _resident((SUBLANES, 2 * N_STATE)),
                  _resident((OCTETS, OCT_IN, 2 * OCT_ST)),
                  _resident((OCTETS, 2 * OCT_ST, OCT_IN)),
                  _resident((1, D_SSM), layer)],
        out_specs=[y_spec, st_spec, st_spec],
        out_shape=[y_shape, jax.ShapeDtypeStruct((nb, N_STATE), F32),
                   jax.ShapeDtypeStruct((nb, N_STATE), F32)],
        scratch_shapes=[pltpu.VMEM((OCTETS, m, OCT_IN), F32), pltpu.VMEM((m, D_SSM), F32),
                        pltpu.VMEM((m, 2 * N_STATE), F32), pltpu.VMEM((m, D_SSM), F32),
                        pltpu.VMEM((nb, 2 * N_STATE), F32)],
        compiler_params=_params(("arbitrary",)),
        name="ssm_branch",
    )(z, h0r, h0i, a8, bw, cw, dvec)


def _att_prompt_kernel(q0_ref, q1_ref, k0_ref, k1_ref, v0_ref, v1_ref, o_ref,
                       qkv_scr, o_scr, l_scr, bp_scr, *, t_len):
    g_id = pl.program_id(1)
    slopes = _slopes()
    lane = lax.broadcasted_iota(jnp.int32, (BAND, LANES), 1)
    low = lane < HEAD_DIM

    for n, r in enumerate((q0_ref, q1_ref, k0_ref, k1_ref, v0_ref, v1_ref)):
        qkv_scr[n] = r[...].astype(F32)

    def group(g, dil):
        nblk = t_len // dil // BAND
        qi = lax.broadcasted_iota(jnp.int32, (BAND, 2 * BAND), 0)
        kj = lax.broadcasted_iota(jnp.int32, (BAND, 2 * BAND), 1)
        dist = qi + BAND - kj
        valid = (dist >= 0) & (dist <= BAND)
        valid0 = valid & (kj >= BAND)
        distf = dist.astype(F32)
        for h in range(HEADS_PER_GROUP):
            bias = (-slopes[g * HEADS_PER_GROUP + h] * dil) * distf
            bp_scr[0, h] = jnp.where(valid0, bias, NEG)
            bp_scr[1, h] = jnp.where(valid, bias, NEG)

        def rows(start):
            if dil == 1:
                return pl.ds(start, BAND)
            return pl.ds(start, BAND, stride=dil)

        def block(blk, carry):
            r = blk // nblk
            n = blk % nblk
            start_c = r + dil * BAND * n
            start_p = r + dil * BAND * jnp.maximum(n - 1, 0)
            sel = jnp.minimum(n, 1)
            for pair in range(2):
                qp = qkv_scr[pair, rows(start_c), :] * ATT_SCALE
                kp = jnp.concatenate([qkv_scr[2 + pair, rows(start_p), :],
                                      qkv_scr[2 + pair, rows(start_c), :]], axis=0).astype(BF16)
                vp = jnp.concatenate([qkv_scr[4 + pair, rows(start_p), :],
                                      qkv_scr[4 + pair, rows(start_c), :]], axis=0).astype(BF16)
                o_h, l_h = [], []
                for hh in range(2):
                    h = 2 * pair + hh
                    qm = jnp.where(low if hh == 0 else jnp.logical_not(low), qp, 0.0).astype(BF16)
                    t = _dot_nt(qm, kp) + bp_scr[sel, h]
                    mx = jnp.max(t, axis=-1, keepdims=True)
                    p = jnp.exp(t - mx)
                    den = jnp.sum(p, axis=-1, keepdims=True)
                    pv = _dot(p.astype(BF16), vp)
                    o_h.append(pv / den)
                    l_h.append(jnp.broadcast_to(mx + jnp.log(den), (BAND, LANES)))
                o_scr[2 * g + pair, rows(start_c), :] = jnp.where(low, o_h[0], o_h[1])
                l_scr[2 * g + pair, rows(start_c), :] = jnp.where(low, l_h[0], l_h[1])
            return carry

        lax.fori_loop(0, dil * nblk, block, 0, unroll=4)

    for g, (window, dil) in enumerate(ATT_PATTERNS):
        @pl.when(g_id == g)
        def _(g=g, dil=dil):
            group(g, dil)

    @pl.when(g_id == len(ATT_PATTERNS) - 1)
    def _():
        tr = 256

        def comb(i, carry):
            r0 = pl.multiple_of(i * tr, tr)
            for pair in range(2):
                l0 = l_scr[pair, pl.ds(r0, tr), :]
                l1 = l_scr[2 + pair, pl.ds(r0, tr), :]
                l2 = l_scr[4 + pair, pl.ds(r0, tr), :]
                mx = jnp.maximum(jnp.maximum(l0, l1), l2)
                w0 = jnp.exp(l0 - mx)
                w1 = jnp.exp(l1 - mx)
                w2 = jnp.exp(l2 - mx)
                num = (w0 * o_scr[pair, pl.ds(r0, tr), :] + w1 * o_scr[2 + pair, pl.ds(r0, tr), :]
                       + w2 * o_scr[4 + pair, pl.ds(r0, tr), :])
                o_ref[pl.ds(r0, tr), LANES * pair:LANES * (pair + 1)] = (
                    num / (w0 + w1 + w2)).astype(o_ref.dtype)
            return carry

        lax.fori_loop(0, t_len // tr, comb, 0)


def _att_prompt(z3):
    nbatch, t_len, _ = z3.shape
    ng = len(ATT_PATTERNS)
    kern = functools.partial(_att_prompt_kernel, t_len=t_len)

    def spec(off, pair):
        return pl.BlockSpec((None, t_len, LANES), lambda b, g: (b, 0, off // LANES + 2 * g + pair))

    return pl.pallas_call(
        kern,
        grid=(nbatch, ng),
        in_specs=[spec(off, pair) for off in (OFF_Q, OFF_K, OFF_V) for pair in range(2)],
        out_specs=pl.BlockSpec((None, t_len, D_ATT_OUT), lambda b, g: (b, 0, 0)),
        out_shape=jax.ShapeDtypeStruct((nbatch, t_len, D_ATT_OUT), BF16),
        scratch_shapes=[pltpu.VMEM((6, t_len, LANES), F32),
                        pltpu.VMEM((2 * ng, t_len, LANES), F32),
                        pltpu.VMEM((2 * ng, t_len, LANES), F32),
                        pltpu.VMEM((2, HEADS_PER_GROUP, BAND, 2 * BAND), F32)],
        compiler_params=_params(("parallel", "arbitrary")),
        name="att_prompt",
    )(z3, z3, z3, z3, z3, z3)


QROWS = 8
ATT_S_BB = 2


def _att_sample_stages(q_ref, kn_ref, vn_ref, k0_ref, k1_ref, k2_ref, v0_ref, v1_ref, v2_ref,
                       b0_ref, b1_ref, b2_ref, bn_ref, o_ref):
    kc_refs = (k0_ref, k1_ref, k2_ref)
    vc_refs = (v0_ref, v1_ref, v2_ref)
    bc_refs = (b0_ref, b1_ref, b2_ref)
    w = D_ATT_OUT
    ng = len(ATT_PATTERNS)
    state = {}

    def stage(b, g):
        def run():
            head = lax.broadcasted_iota(jnp.int32, (QROWS, w), 1) // HEAD_DIM
            zpad = jnp.zeros((LANES - QROWS, w), BF16)
            rows = slice(QROWS * b, QROWS * (b + 1))
            outs, lses = state.setdefault(b, ([], []))
            cols = slice(g * w, (g + 1) * w)
            qg = q_ref[rows, cols] * ATT_SCALE
            qblk = jnp.concatenate([jnp.where(head == h, qg, 0.0) for h in range(HEADS_PER_GROUP)],
                                   axis=0).astype(BF16)
            s_c = _dot(qblk, kc_refs[g][b].astype(BF16))
            bias_c = bc_refs[g][...]
            t_c = jnp.where(bias_c > 0.5 * NEG, s_c + bias_c, NEG)
            kn = jnp.concatenate([kn_ref[rows, cols].astype(BF16), zpad], axis=0)
            vn = jnp.concatenate([vn_ref[rows, cols].astype(BF16), zpad], axis=0)
            bias_n = bn_ref[g]
            t_n = jnp.where(bias_n > 0.5 * NEG, _dot_nt(qblk, kn) + bias_n, NEG)
            mx = jnp.maximum(jnp.max(t_c, axis=-1, keepdims=True), jnp.max(t_n, axis=-1, keepdims=True))
            p_c = jnp.exp(t_c - mx)
            p_n = jnp.exp(t_n - mx)
            den = jnp.sum(p_c, axis=-1, keepdims=True) + jnp.sum(p_n, axis=-1, keepdims=True)
            o = _dot_nt(p_c.astype(BF16), vc_refs[g][b].astype(BF16))
            o = (o + _dot(p_n.astype(BF16), vn)) / den
            lse = jnp.broadcast_to(mx + jnp.log(den), o.shape)
            og = jnp.zeros((QROWS, w), F32)
            lg = jnp.zeros((QROWS, w), F32)
            for h in range(HEADS_PER_GROUP):
                og = jnp.where(head == h, o[QROWS * h:QROWS * (h + 1)], og)
                lg = jnp.where(head == h, lse[QROWS * h:QROWS * (h + 1)], lg)
            outs.append(og)
            lses.append(lg)
            if g == ng - 1:
                mx = jnp.maximum(jnp.maximum(lses[0], lses[1]), lses[2])
                ws = [jnp.exp(l - mx) for l in lses]
                o_ref[rows, :] = ((ws[0] * outs[0] + ws[1] * outs[1] + ws[2] * outs[2])
                                  / (ws[0] + ws[1] + ws[2])).astype(o_ref.dtype)

        return run

    return [stage(b, g) for b in range(ATT_S_BB) for g in range(ng)]


def _att_sample_consts(t_new):
    slopes = _slopes()
    rows = HEADS_PER_GROUP * QROWS
    bcs = []
    bn = np.full((3, rows, LANES), NEG, np.float32)
    for g, (window, dil) in enumerate(ATT_PATTERNS):
        bc = np.zeros((rows, window), np.float32)
        pos = np.arange(window)
        for h in range(HEADS_PER_GROUP):
            s = slopes[g * HEADS_PER_GROUP + h]
            for j in range(QROWS):
                r = QROWS * h + j
                if j >= t_new:
                    bn[g, r, 0] = 0.0
                    continue
                dist = window + j - pos
                ok = (dist % dil == 0) & (dist <= window)
                bc[r] = np.where(ok, -s * dist, NEG)
                for i_ in range(j + 1):
                    if (j - i_) % dil == 0:
                        bn[g, r, i_] = -s * (j - i_)
        bcs.append(jnp.asarray(bc))
    return bcs, jnp.asarray(bn)


def _cache_t(c):
    d, nb_, length, nh, hd = c.shape
    return jnp.transpose(c, (0, 1, 3, 4, 2)).reshape(d, nb_, nh * hd, length)


class _AttSampleHost:
    def __init__(self, qkv, kts, vts, consts, layer, seq0, nsteps):
        bb = ATT_S_BB
        w = D_ATT_OUT
        blk0 = seq0 // bb
        bcs, bn = consts
        rows = HEADS_PER_GROUP * QROWS

        def new_spec(part):
            return pl.BlockSpec((bb * QROWS, D_ATT), lambda i: (blk0 + i, part))

        def cache_spec(length):
            return pl.BlockSpec((None, bb, w, length), lambda i: (layer, blk0 + i, 0, 0))

        self.inputs = [qkv, qkv, qkv] + list(kts) + list(vts) + list(bcs) + [bn]
        self.in_specs = ([new_spec(0), new_spec(1), new_spec(2)]
                         + [cache_spec(win) for win, _ in ATT_PATTERNS] * 2
                         + [_resident((rows, win)) for win, _ in ATT_PATTERNS]
                         + [_resident((3, rows, LANES))])
        self.out_spec = pl.BlockSpec((bb * QROWS, w), lambda i: (i, 0))
        self.out_shape = jax.ShapeDtypeStruct((nsteps * bb * QROWS, w), BF16)


N_ATT_HOST_INPUTS = 3 + 3 * len(ATT_PATTERNS) + 1


MERGE_CHUNK = 256

def _merge_kernel(*refs, n_host):
    (cy_ref, zs_ref, at_ref, gt_ref, x_ref, gm_ref, shf_ref, scf_ref,
     wco_ref, wsg_ref, wat_ref, wo_ref, gpost_ref, gpre_ref) = refs[:14]
    x1_ref, hf_ref = refs[14 + n_host:16 + n_host]
    mg_scr = refs[-1]
    if n_host:
        for run in _att_sample_stages(*refs[14:14 + n_host], refs[16 + n_host]):
            run()
    cy = cy_ref[...]
    zs = zs_ref[...]
    at = at_ref[...]
    for c in range(D_MODEL // MERGE_CHUNK):
        lo = c * MERGE_CHUNK
        cols = slice(lo, lo + MERGE_CHUNK)
        a = _dot(cy, wco_ref[:, cols])
        bm = _dot(zs, wsg_ref[:, cols]) * _sigmoid(_dot(zs, wsg_ref[:, D_MODEL + lo:D_MODEL + lo + MERGE_CHUNK]))
        cc = _dot(at, wat_ref[:, cols])
        merged = (_sigmoid(gt_ref[:, cols].astype(F32)) * a
                  + _sigmoid(gt_ref[:, D_MODEL + lo:D_MODEL + lo + MERGE_CHUNK].astype(F32)) * bm
                  + _sigmoid(gt_ref[:, 2 * D_MODEL + lo:2 * D_MODEL + lo + MERGE_CHUNK].astype(F32)) * cc)
        mg_scr[:, cols] = merged.astype(BF16)
    mo = _dot(mg_scr[...], wo_ref[...])
    x1 = x_ref[...] + gm_ref[...] * _rms(mo, gpost_ref[...])
    x1_ref[...] = x1
    hf_ref[...] = (_rms(x1, gpre_ref[...]) * (1.0 + scf_ref[...]) + shf_ref[...]).astype(BF16)


def _merge(cy, zs, at, z2, x2, mod_arr, mod_spec, w, layer, tm, host=None):
    rows = x2.shape[0]

    def rowspec(width):
        return pl.BlockSpec((tm, width), lambda i: (i, 0))

    n_host = 0 if host is None else len(host.inputs)
    outs = pl.pallas_call(
        functools.partial(_merge_kernel, n_host=n_host),
        grid=(rows // tm,),
        in_specs=[rowspec(D_CONV), rowspec(D_SSM), rowspec(D_ATT_OUT), rowspec(3 * D_MODEL),
                  rowspec(D_MODEL), mod_spec(2), mod_spec(3), mod_spec(4),
                  _resident((D_CONV, D_MODEL), layer), _resident((D_SSM, 2 * D_MODEL), layer),
                  _resident((D_ATT_OUT, D_MODEL), layer), _resident((D_MODEL, D_MODEL), layer),
                  _resident((1, D_MODEL), layer), _resident((1, D_MODEL), layer)]
        + ([] if host is None else host.in_specs),
        out_specs=[rowspec(D_MODEL), rowspec(D_MODEL)] + ([] if host is None else [host.out_spec]),
        out_shape=[jax.ShapeDtypeStruct((rows, D_MODEL), F32),
                   jax.ShapeDtypeStruct((rows, D_MODEL), BF16)]
        + ([] if host is None else [host.out_shape]),
        scratch_shapes=[pltpu.VMEM((tm, D_MODEL), BF16)],
        compiler_params=_params(("parallel",)),
        name="merge",
    )(cy, zs, at, z2, x2, mod_arr, mod_arr, mod_arr, w["w_conv_out"], w["w_ssm_glu"], w["w_att"],
      w["w_out"], w["g_post_mix"], w["g_pre_ffn"], *([] if host is None else host.inputs))
    return outs


FF_CHUNK = 256


def _ffn_kernel(*refs, n_host):
    hf_ref, x1_ref, gf_ref, w1_ref, w2_ref, gpost_ref = refs[:6]
    o_ref = refs[6 + n_host]
    acc_ref = refs[-1]
    if n_host:
        for run in _att_sample_stages(*refs[6:6 + n_host], refs[7 + n_host]):
            run()
    hf = hf_ref[...]
    for c in range(D_FF // FF_CHUNK):
        lo = c * FF_CHUNK
        a = _dot(hf, w1_ref[:, lo:lo + FF_CHUNK])
        b = _dot(hf, w1_ref[:, D_FF + lo:D_FF + lo + FF_CHUNK])
        f = (_silu(a) * b).astype(BF16)
        part = _dot(f, w2_ref[lo:lo + FF_CHUNK, :])
        if c == 0:
            acc_ref[...] = part
        else:
            acc_ref[...] += part
    o_ref[...] = x1_ref[...] + gf_ref[...] * _rms(acc_ref[...], gpost_ref[...])


def _ffn(hf, x1, mod_arr, mod_spec, w, layer, tm, host=None):
    rows = x1.shape[0]
    n_host = 0 if host is None else len(host.inputs)
    return pl.pallas_call(
        functools.partial(_ffn_kernel, n_host=n_host),
        grid=(rows // tm,),
        in_specs=[pl.BlockSpec((tm, D_MODEL), lambda i: (i, 0)),
                  pl.BlockSpec((tm, D_MODEL), lambda i: (i, 0)),
                  mod_spec(5),
                  _resident((D_MODEL, 2 * D_FF), layer),
                  _resident((D_FF, D_MODEL), layer),
                  _resident((1, D_MODEL), layer)] + ([] if host is None else host.in_specs),
        out_specs=[pl.BlockSpec((tm, D_MODEL), lambda i: (i, 0))]
        + ([] if host is None else [host.out_spec]),
        out_shape=[jax.ShapeDtypeStruct((rows, D_MODEL), F32)]
        + ([] if host is None else [host.out_shape]),
        scratch_shapes=[pltpu.VMEM((tm, D_MODEL), F32)],
        compiler_params=_params(("parallel",)),
        name="ffn",
    )(hf, x1, mod_arr, w["w_ffn_in"], w["w_ffn_out"], w["g_post_ffn"],
      *([] if host is None else host.inputs))


ROW_TILE = 512


def kernel(x_prompt, x_sample, cache_k0, cache_v0, cache_k1, cache_v1, cache_k2, cache_v2, state_conv, state_ssm_re, state_ssm_im, c_prompt, c_sample, w_mod, b_mod, g_pre_mix, g_post_mix, g_pre_ffn, g_post_ffn, w_in, conv_w, conv_b, conv_ln_g, conv_ln_b, w_conv_out, ssm_a_re, ssm_a_im, ssm_log_dt, ssm_b_re, ssm_b_im, ssm_c_re, ssm_c_im, ssm_d, w_ssm_glu, w_att, w_out, w_ffn_in, w_ffn_out):
    nbp, t_p, _ = x_prompt.shape
    nbs, t_s, _ = x_sample.shape
    caches_k = (cache_k0, cache_k1, cache_k2)
    caches_v = (cache_v0, cache_v1, cache_v2)

    tm = ROW_TILE
    rows_p, rows_s = nbp * t_p, nbs * t_s
    per_seq = t_p // tm
    n_rot = N_IN - 3 * D_MODEL

    mod_p, mod_s = _modulation(c_prompt, jnp.repeat(c_sample, t_s, axis=0), w_mod, b_mod)
    mod_p = mod_p.reshape(DEPTH, nbp, 1, 6 * D_MODEL)
    abr, abi, bbr, bbi = _ssm_discretise(ssm_a_re, ssm_a_im, ssm_log_dt, ssm_b_re, ssm_b_im)

    w = dict(
        g_pre_mix=g_pre_mix[:, None], g_post_mix=g_post_mix[:, None],
        g_pre_ffn=g_pre_ffn[:, None], g_post_ffn=g_post_ffn[:, None],
        w_in=jnp.concatenate([w_in[:, :, n_rot:], w_in[:, :, :n_rot]], axis=2).astype(BF16),
        conv_w=jnp.broadcast_to(conv_w[:, :, None, :], (DEPTH, CONV_WIDTH, SUBLANES, D_CONV)),
        conv_b=conv_b[:, None], conv_ln_g=conv_ln_g[:, None], conv_ln_b=conv_ln_b[:, None],
        ssm_d=ssm_d.reshape(DEPTH, 1, D_SSM),
        w_conv_out=w_conv_out.astype(BF16), w_ssm_glu=w_ssm_glu.astype(BF16),
        w_att=w_att.astype(BF16), w_out=w_out.astype(BF16),
        w_ffn_in=w_ffn_in.astype(BF16), w_ffn_out=w_ffn_out.astype(BF16))
    prev_s = jnp.pad(state_conv, ((0, 0), (0, 0), (HALO - CONV_WIDTH + 1, 0), (0, 0)))
    h0r_s = state_ssm_re.reshape(DEPTH, nbs, N_STATE)
    h0i_s = state_ssm_im.reshape(DEPTH, nbs, N_STATE)
    prev_p = jnp.zeros((nbp, HALO, D_CONV), F32)
    h0_p = jnp.zeros((nbp, N_STATE), F32)
    kts = [_cache_t(c) for c in caches_k]
    vts = [_cache_t(c) for c in caches_v]
    att_consts = _att_sample_consts(t_s)
    n_host_steps = rows_p // tm
    seqs_per_host = n_host_steps * ATT_S_BB
    assert 2 * seqs_per_host == nbs

    xp = x_prompt.reshape(rows_p, D_MODEL)
    xs = x_sample.reshape(rows_s, D_MODEL)
    prompt_states, sample_states = [], []
    kvt_p = None
    for l in range(DEPTH):
        a8, bw, cw = _ssm_pack(abr[l], abi[l], bbr[l], bbi[l], ssm_c_re[l], ssm_c_im[l])

        def spec_p(c, l=l):
            return pl.BlockSpec((None, None, 1, D_MODEL), lambda i: (l, i // per_seq, 0, c))

        def spec_s(c, l=l):
            return pl.BlockSpec((None, tm, D_MODEL), lambda i: (l, i, c))

        z_s, kv_s = _in_proj(xs, mod_s, spec_s, w["g_pre_mix"], w["w_in"], tm, l)
        cy_s, conv_s = _conv_branch(z_s, prev_s, l, w["conv_w"], w["conv_b"], w["conv_ln_g"],
                                    w["conv_ln_b"], l, nbs, t_s, bb=8, tq=t_s, flat=True)
        zs_s, sre_s, sim_s = _ssm_branch(z_s, h0r_s, h0i_s, l, a8, bw, cw, w["ssm_d"], l,
                                         nb=nbs, t_len=t_s, tc=t_s, flat=True)
        qkv_s = jnp.pad(z_s[:, OFF_Q:].astype(F32).reshape(nbs, t_s, 3 * D_ATT),
                        ((0, 0), (0, QROWS - t_s), (0, 0))).reshape(nbs * QROWS, 3 * D_ATT)

        z_p, kvt_p = _in_proj_prompt(xp, mod_p, spec_p, w["g_pre_mix"], w["w_in"], tm,
                                     nbp, t_p, l, kvt_p)
        z3 = z_p.reshape(nbp, t_p, N_IN)
        cy_p, conv_p = _conv_branch(z3, prev_p, None, w["conv_w"], w["conv_b"], w["conv_ln_g"],
                                    w["conv_ln_b"], l, nbp, t_p, bb=1, tq=256, flat=False)
        zs_p, sre_p, sim_p = _ssm_branch(z3, h0_p, h0_p, None, a8, bw, cw, w["ssm_d"], l,
                                         nb=nbp, t_len=t_p, tc=64, flat=False)
        att_p = _att_prompt(z3)
        host_a = _AttSampleHost(qkv_s, kts, vts, att_consts, l, 0, n_host_steps)
        host_b = _AttSampleHost(qkv_s, kts, vts, att_consts, l, seqs_per_host, n_host_steps)
        x1_p, hf_p, att_a = _merge(cy_p.reshape(rows_p, D_CONV), zs_p.reshape(rows_p, D_SSM),
                                   att_p.reshape(rows_p, D_ATT_OUT), z_p, xp, mod_p, spec_p, w, l, tm,
                                   host=host_a)
        xp, att_b = _ffn(hf_p, x1_p, mod_p, spec_p, w, l, tm, host=host_b)

        att_s = jnp.concatenate([att_a, att_b], axis=0).reshape(nbs, QROWS, D_ATT_OUT)[:, :t_s]
        x1_s, hf_s = _merge(cy_s, zs_s, att_s.reshape(rows_s, D_ATT_OUT), z_s, xs, mod_s, spec_s, w, l, tm)
        xs, = _ffn(hf_s, x1_s, mod_s, spec_s, w, l, tm)

        kv3 = kv_s.reshape(nbs, t_s, 2 * D_ATT)
        kvs = [kv3[:, :, off + g * D_ATT_OUT:off + (g + 1) * D_ATT_OUT].reshape(
            nbs, t_s, HEADS_PER_GROUP, HEAD_DIM) for g in range(len(ATT_PATTERNS)) for off in (0, D_ATT)]
        prompt_states.append((conv_p, sre_p.reshape(nbp, N_SSM_GROUPS, SSM_STATE),
                              sim_p.reshape(nbp, N_SSM_GROUPS, SSM_STATE)))
        sample_states.append((*kvs, conv_s, sre_s.reshape(nbs, N_SSM_GROUPS, SSM_STATE),
                              sim_s.reshape(nbs, N_SSM_GROUPS, SSM_STATE)))

    ps = [jnp.transpose(a.reshape(DEPTH, nbp, HEADS_PER_GROUP, HEAD_DIM, a.shape[-1]), (0, 1, 4, 2, 3))
          for a in kvt_p]
    ps += [jnp.stack([s[i] for s in prompt_states]) for i in range(3)]
    ss = [jnp.stack([s[i] for s in sample_states]) for i in range(9)]
    return (xp.reshape(nbp, t_p, D_MODEL), xs.reshape(nbs, t_s, D_MODEL), *ps, *ss)
```

```python
import functools

import numpy as np
import jax
import jax.numpy as jnp
from jax import lax
from jax.experimental import pallas as pl
from jax.experimental.pallas import tpu as pltpu

F32 = jnp.float32
BF16 = jnp.bfloat16

D_MODEL = 1024
DEPTH = 2
D_CONV = 512
CONV_WIDTH = 31
D_SSM = 512
SSM_GROUP = 16
N_SSM_GROUPS = 32
SSM_STATE = 64
HEAD_DIM = 64
HEADS_PER_GROUP = 4
ATT_PATTERNS = ((128, 1), (512, 4), (2048, 16))
N_ATT_HEADS = 12
D_ATT = 768
D_ATT_OUT = 256
ATT_SCALE = HEAD_DIM ** -0.5
D_FF = 2816
N_IN = 6912
RMS_EPS = 1e-6
LN_EPS = 1e-5
NEG = -1e30
BAND = 128

OFF_CONV = 0
OFF_SSM = 1024
OFF_Q = 1536
OFF_K = 2304
OFF_V = 3072
OFF_GATES = 3840
GATE_BLOCK = 768
IN_CHUNK = 768

LANES = 128
SUBLANES = 8
HALO = 32
N_STATE = N_SSM_GROUPS * SSM_STATE
OCTETS = 4
OCT_IN = D_SSM // OCTETS
OCT_ST = N_STATE // OCTETS
VMEM_LIMIT = 56 * 1024 * 1024


def _params(sem):
    return pltpu.CompilerParams(dimension_semantics=sem, vmem_limit_bytes=VMEM_LIMIT)


def _resident(shape, layer=None):
    if layer is None:
        return pl.BlockSpec(shape, lambda *_: (0,) * len(shape), pipeline_mode=pl.Buffered(1))
    return pl.BlockSpec((None,) + tuple(shape), lambda *_: (layer,) + (0,) * len(shape),
                        pipeline_mode=pl.Buffered(1))


def _slopes():
    return [2.0 ** (-8.0 * (i + 1) / N_ATT_HEADS) for i in range(N_ATT_HEADS)]


def _rms(x, g):
    return x * lax.rsqrt(jnp.mean(x * x, axis=-1, keepdims=True) + RMS_EPS) * g


def _sigmoid(x):
    return 0.5 * jnp.tanh(0.5 * x) + 0.5


def _silu(x):
    return x * _sigmoid(x)


def _dot(a, b):
    return jnp.dot(a, b, preferred_element_type=F32)


def _dot_nt(a, b):
    return lax.dot_general(a, b, (((1,), (1,)), ((), ())), preferred_element_type=F32)


def _mod_kernel(cp_ref, cs_ref, w_ref, b_ref, op_ref, os_ref):
    w = w_ref[...].astype(BF16)
    op_ref[...] = _dot(_silu(cp_ref[...]).astype(BF16), w) + b_ref[...]
    os_ref[...] = _dot(_silu(cs_ref[...]).astype(BF16), w) + b_ref[...]


def _modulation(c_p, c_s_rows, w_mod, b_mod):
    np_, ns = c_p.shape[0], c_s_rows.shape[0]
    tn = 512
    return pl.pallas_call(
        _mod_kernel,
        grid=(DEPTH, 6 * D_MODEL // tn),
        in_specs=[pl.BlockSpec((np_, D_MODEL), lambda l, j: (0, 0)),
                  pl.BlockSpec((ns, D_MODEL), lambda l, j: (0, 0)),
                  pl.BlockSpec((None, D_MODEL, tn), lambda l, j: (l, 0, j)),
                  pl.BlockSpec((None, 1, tn), lambda l, j: (l, 0, j))],
        out_specs=[pl.BlockSpec((None, np_, tn), lambda l, j: (l, 0, j)),
                   pl.BlockSpec((None, ns, tn), lambda l, j: (l, 0, j))],
        out_shape=[jax.ShapeDtypeStruct((DEPTH, np_, 6 * D_MODEL), F32),
                   jax.ShapeDtypeStruct((DEPTH, ns, 6 * D_MODEL), F32)],
        compiler_params=_params(("parallel", "parallel")),
        name="modulation",
    )(c_p, c_s_rows, w_mod, b_mod.reshape(DEPTH, 1, 6 * D_MODEL))


def _ssm_disc_kernel(ar_ref, ai_ref, ldt_ref, br_ref, bi_ref, abr_ref, abi_ref, bbr_ref, bbi_ref):
    ar = ar_ref[...]
    ai = ai_ref[...]
    dt = jnp.exp(ldt_ref[...])
    mag = jnp.exp(dt * ar)
    abr = mag * jnp.cos(dt * ai)
    abi = mag * jnp.sin(dt * ai)
    den = ar * ar + ai * ai
    fr = ((abr - 1.0) * ar + abi * ai) / den
    fi = (abi * ar - (abr - 1.0) * ai) / den
    br = br_ref[...]
    bi = bi_ref[...]
    abr_ref[...] = abr
    abi_ref[...] = abi
    bbr_ref[...] = fr * br - fi * bi
    bbi_ref[...] = fr * bi + fi * br


def _ssm_discretise(a_re, a_im, log_dt, b_re, b_im):
    ar = a_re.reshape(DEPTH, 1, N_STATE)
    ai = a_im.reshape(DEPTH, 1, N_STATE)
    ldt = jnp.repeat(log_dt, SSM_STATE, axis=-1).reshape(DEPTH, 1, N_STATE)
    br = jnp.transpose(b_re, (0, 3, 1, 2)).reshape(DEPTH, SSM_GROUP, N_STATE)
    bi = jnp.transpose(b_im, (0, 3, 1, 2)).reshape(DEPTH, SSM_GROUP, N_STATE)
    row = pl.BlockSpec((None, 1, N_STATE), lambda l: (l, 0, 0))
    mat = pl.BlockSpec((None, SSM_GROUP, N_STATE), lambda l: (l, 0, 0))
    return pl.pallas_call(
        _ssm_disc_kernel,
        grid=(DEPTH,),
        in_specs=[row, row, row, mat, mat],
        out_specs=[row, row, mat, mat],
        out_shape=[jax.ShapeDtypeStruct((DEPTH, 1, N_STATE), F32)] * 2
        + [jax.ShapeDtypeStruct((DEPTH, SSM_GROUP, N_STATE), F32)] * 2,
        compiler_params=_params(("parallel",)),
        name="ssm_discretise",
    )(ar, ai, ldt, br, bi)


def _ssm_pack(abr, abi, bbr, bbi, c_re, c_im):
    gpo = N_SSM_GROUPS // OCTETS
    eye = jnp.eye(gpo, dtype=F32)

    def oct_state(v):
        return v.reshape(OCTETS, OCT_ST)

    a8 = jnp.concatenate([oct_state(abr), oct_state(abi)], axis=1).reshape(1, 2 * N_STATE)
    a8 = jnp.broadcast_to(a8, (SUBLANES, 2 * N_STATE))

    def b_tiles(bb):
        t = bb.reshape(SSM_GROUP, OCTETS, gpo, SSM_STATE)
        t = jnp.transpose(t, (1, 2, 0, 3))
        t = t[:, :, :, None, :] * eye[None, :, None, :, None]
        return t.reshape(OCTETS, OCT_IN, OCT_ST)

    bw = jnp.concatenate([b_tiles(bbr), b_tiles(bbi)], axis=2).astype(BF16)

    def c_tiles(cc):
        t = cc.reshape(OCTETS, gpo, SSM_GROUP, SSM_STATE)
        t = jnp.transpose(t, (0, 1, 3, 2))
        t = t[:, :, :, None, :] * eye[None, :, None, :, None]
        return t.reshape(OCTETS, OCT_ST, OCT_IN)

    cw = jnp.concatenate([c_tiles(c_re), -c_tiles(c_im)], axis=1).astype(BF16)
    return a8, bw, cw


def _is_kv_chunk(lo):
    return OFF_K <= lo < OFF_K + 2 * D_ATT


def _in_proj_sample_kernel(*refs, nb, t_len, n_alias):
    x_ref, sh_ref, sc_ref, g_ref, w_ref = refs[:5]
    z_ref = refs[5 + n_alias]
    kvt_refs = refs[6 + n_alias:]
    y = _rms(x_ref[...], g_ref[...])
    h = (y * (1.0 + sc_ref[...]) + sh_ref[...]).astype(BF16)
    for c in range(N_IN // IN_CHUNK):
        lo = c * IN_CHUNK
        r = _dot(h, w_ref[:, lo:lo + IN_CHUNK])
        z_ref[:, lo:lo + IN_CHUNK] = r.astype(BF16)
        if _is_kv_chunk(lo):
            which = (lo - OFF_K) // D_ATT
            for g in range(len(ATT_PATTERNS)):
                for t in range(t_len):
                    kvt_refs[2 * g + which][t] = r[t * nb:(t + 1) * nb, g * D_ATT_OUT:(g + 1) * D_ATT_OUT].T


def _in_proj_sample(x2, mod_arr, mod_spec, g_pre, w_bf, nb, t_len, layer, prev_kvt):
    rows = x2.shape[0]
    n_alias = 0 if prev_kvt is None else len(prev_kvt)
    kern = functools.partial(_in_proj_sample_kernel, nb=nb, t_len=t_len, n_alias=n_alias)
    n_kv = 2 * len(ATT_PATTERNS)
    outs = pl.pallas_call(
        kern,
        grid=(1,),
        in_specs=[pl.BlockSpec((rows, D_MODEL), lambda i: (0, 0)),
                  mod_spec(0), mod_spec(1),
                  _resident((1, D_MODEL), layer),
                  _resident((D_MODEL, N_IN), layer)] + [pl.BlockSpec(memory_space=pl.ANY)] * n_alias,
        out_specs=[pl.BlockSpec((rows, N_IN), lambda i: (0, 0))]
        + [pl.BlockSpec((None, t_len, D_ATT_OUT, nb), lambda i: (layer, 0, 0, 0))] * n_kv,
        out_shape=[jax.ShapeDtypeStruct((rows, N_IN), BF16)]
        + [jax.ShapeDtypeStruct((DEPTH, t_len, D_ATT_OUT, nb), F32)] * n_kv,
        input_output_aliases={5 + k: 1 + k for k in range(n_alias)},
        compiler_params=_params(("arbitrary",)),
        name="in_proj_sample",
    )(x2, mod_arr, mod_arr, g_pre, w_bf, *(prev_kvt or ()))
    return outs[0], list(outs[1:])


def _in_proj_prompt_kernel(*refs, keeps, tm, n_alias):
    x_ref, sh_ref, sc_ref, g_ref, w_ref = refs[:5]
    z_ref = refs[5 + n_alias]
    kvt_refs = refs[6 + n_alias:]
    y = _rms(x_ref[...], g_ref[...])
    h = (y * (1.0 + sc_ref[...]) + sh_ref[...]).astype(BF16)
    for c in range(N_IN // IN_CHUNK):
        lo = c * IN_CHUNK
        r = _dot(h, w_ref[:, lo:lo + IN_CHUNK])
        z_ref[:, lo:lo + IN_CHUNK] = r.astype(BF16)
        if _is_kv_chunk(lo):
            which = (lo - OFF_K) // D_ATT
            for g, keep in enumerate(keeps):
                kept = min(keep, tm)
                kvt_refs[2 * g + which][...] = r[tm - kept:, g * D_ATT_OUT:(g + 1) * D_ATT_OUT].T


def _in_proj_prompt(x2, mod_arr, mod_spec, g_pre, w_bf, tm, nbatch, t_len, layer, prev_kvt):
    rows = x2.shape[0]
    per_seq = t_len // tm
    keeps = tuple(min(window, t_len) for window, _ in ATT_PATTERNS)
    n_alias = 0 if prev_kvt is None else len(prev_kvt)
    kern = functools.partial(_in_proj_prompt_kernel, keeps=keeps, tm=tm, n_alias=n_alias)
    kvt_specs, kvt_shapes = [], []
    for keep in keeps:
        first = per_seq - max(keep // tm, 1)
        kept = min(keep, tm)
        for _ in range(2):
            kvt_specs.append(pl.BlockSpec(
                (None, None, D_ATT_OUT, kept),
                lambda i, first=first: (layer, i // per_seq, 0, jnp.maximum(i % per_seq - first, 0))))
            kvt_shapes.append(jax.ShapeDtypeStruct((DEPTH, nbatch, D_ATT_OUT, keep), F32))
    outs = pl.pallas_call(
        kern,
        grid=(rows // tm,),
        in_specs=[pl.BlockSpec((tm, D_MODEL), lambda i: (i, 0)),
                  mod_spec(0), mod_spec(1),
                  _resident((1, D_MODEL), layer),
                  _resident((D_MODEL, N_IN), layer)] + [pl.BlockSpec(memory_space=pl.ANY)] * n_alias,
        out_specs=[pl.BlockSpec((tm, N_IN), lambda i: (i, 0))] + kvt_specs,
        out_shape=[jax.ShapeDtypeStruct((rows, N_IN), BF16)] + kvt_shapes,
        input_output_aliases={5 + k: 1 + k for k in range(n_alias)},
        compiler_params=_params(("arbitrary",)),
        name="in_proj_prompt",
    )(x2, mod_arr, mod_arr, g_pre, w_bf, *(prev_kvt or ()))
    return outs[0], list(outs[1:])


def _conv_kernel(a_ref, prev_ref, w_ref, b_ref, lg_ref, lb_ref, y_ref, st_ref, ubuf, sh_scr,
                 *, bb, tq, nt, sub, flat):
    i = pl.program_id(1)
    base = HALO - (CONV_WIDTH - 1)
    span = tq + HALO - SUBLANES
    nrt = max(sub // SUBLANES, 1)
    rpt = min(sub, SUBLANES)
    for b in range(bb):
        @pl.when(i == 0)
        def _():
            ubuf[b, 0:HALO, :] = prev_ref[b]

        @pl.when(i > 0)
        def _():
            ubuf[b, 0:HALO, :] = ubuf[b, tq:tq + HALO, :]

        a = (a_ref[b * tq:(b + 1) * tq, :] if flat else a_ref[b]).astype(F32)
        ubuf[b, HALO:HALO + tq, :] = a[:, :D_CONV] * _sigmoid(a[:, D_CONV:])
        for r in range(1, SUBLANES):
            sh_scr[r - 1, 0:span, :] = ubuf[b, r:r + span, :]
        for r0 in range(0, tq, sub):
            acc = jnp.zeros((nrt, rpt, D_CONV), F32) + b_ref[...]
            for j in range(CONV_WIDTH):
                off = base + j
                al = r0 + (off // SUBLANES) * SUBLANES
                if off % SUBLANES == 0:
                    src = ubuf[b, al:al + sub, :]
                else:
                    src = sh_scr[off % SUBLANES - 1, al:al + sub, :]
                acc = acc + w_ref[j, 0:rpt, :][None] * src.reshape(nrt, rpt, D_CONV)
            acc = acc.reshape(sub, D_CONV)
            mu = jnp.mean(acc, axis=-1, keepdims=True)
            xc = acc - mu
            yn = xc * lax.rsqrt(jnp.mean(xc * xc, axis=-1, keepdims=True) + LN_EPS)
            yv = _silu(yn * lg_ref[...] + lb_ref[...]).astype(y_ref.dtype)
            if flat:
                y_ref[b * tq + r0:b * tq + r0 + sub, :] = yv
            else:
                y_ref[b, r0:r0 + sub, :] = yv

        @pl.when(i == nt - 1)
        def _():
            st_ref[b] = ubuf[b, base + tq:base + tq + CONV_WIDTH - 1, :]


def _conv_branch(z, prev_pad, prev_layer, w_b8, b, lg, lb, layer, nbatch, t_len, bb, tq, flat):
    nt = t_len // tq
    sub = min(tq, 32)
    kern = functools.partial(_conv_kernel, bb=bb, tq=tq, nt=nt, sub=sub, flat=flat)
    tq_pad = max(tq, SUBLANES)
    col = OFF_CONV // (2 * D_CONV)
    if flat:
        a_spec = pl.BlockSpec((bb * tq, 2 * D_CONV), lambda i, j: (i, col))
        y_spec = pl.BlockSpec((bb * tq, D_CONV), lambda i, j: (i, 0))
        y_shape = jax.ShapeDtypeStruct((nbatch * t_len, D_CONV), BF16)
    else:
        a_spec = pl.BlockSpec((bb, tq, 2 * D_CONV), lambda i, j: (i, j, col))
        y_spec = pl.BlockSpec((bb, tq, D_CONV), lambda i, j: (i, j, 0))
        y_shape = jax.ShapeDtypeStruct((nbatch, t_len, D_CONV), BF16)
    if prev_layer is None:
        prev_spec = pl.BlockSpec((bb, HALO, D_CONV), lambda i, j: (i, 0, 0))
    else:
        prev_spec = pl.BlockSpec((None, bb, HALO, D_CONV), lambda i, j: (prev_layer, i, 0, 0))
    return pl.pallas_call(
        kern,
        grid=(nbatch // bb, nt),
        in_specs=[a_spec, prev_spec,
                  _resident((CONV_WIDTH, SUBLANES, D_CONV), layer),
                  _resident((1, D_CONV), layer), _resident((1, D_CONV), layer),
                  _resident((1, D_CONV), layer)],
        out_specs=[y_spec, pl.BlockSpec((bb, CONV_WIDTH - 1, D_CONV), lambda i, j: (i, 0, 0))],
        out_shape=[y_shape, jax.ShapeDtypeStruct((nbatch, CONV_WIDTH - 1, D_CONV), F32)],
        scratch_shapes=[pltpu.VMEM((bb, HALO + tq_pad + SUBLANES, D_CONV), F32),
                        pltpu.VMEM((SUBLANES - 1, HALO + tq_pad, D_CONV), F32)],
        compiler_params=_params(("parallel", "arbitrary")),
        name="conv_branch",
    )(z, prev_pad, w_b8, b, lg, lb)


def _conv_sample_kernel(*refs, t_len, nbb, n_alias):
    a_ref, prev_ref, w_ref, b_ref, lg_ref, lb_ref = refs[:6]
    y_ref, st_ref = refs[6 + n_alias:]
    hist = CONV_WIDTH - 1
    for k in range(hist - t_len):
        st_ref[k] = prev_ref[k + t_len]
    for t in range(t_len):
        a = a_ref[t].astype(F32)
        st_ref[hist - t_len + t] = a[:, :D_CONV] * _sigmoid(a[:, D_CONV:])
    nrt = nbb // SUBLANES
    for t in range(t_len):
        acc = jnp.zeros((nrt, SUBLANES, D_CONV), F32) + b_ref[...]
        for j in range(CONV_WIDTH):
            k = t + j
            src = prev_ref[k] if k < hist else st_ref[k - t_len]
            acc = acc + w_ref[j][None] * src.reshape(nrt, SUBLANES, D_CONV)
        acc = acc.reshape(nbb, D_CONV)
        mu = jnp.mean(acc, axis=-1, keepdims=True)
        xc = acc - mu
        yn = xc * lax.rsqrt(jnp.mean(xc * xc, axis=-1, keepdims=True) + LN_EPS)
        y_ref[t] = _silu(yn * lg_ref[...] + lb_ref[...]).astype(y_ref.dtype)


def _conv_sample(z_t, state_t, w_b8, b, lg, lb, layer, prev_out):
    t_len, nb, _ = z_t.shape
    nbb = 32
    hist = CONV_WIDTH - 1
    n_alias = 0 if prev_out is None else 1
    kern = functools.partial(_conv_sample_kernel, t_len=t_len, nbb=nbb, n_alias=n_alias)
    st_spec = pl.BlockSpec((None, hist, nbb, D_CONV), lambda i: (layer, 0, i, 0))
    return pl.pallas_call(
        kern,
        grid=(nb // nbb,),
        in_specs=[pl.BlockSpec((t_len, nbb, 2 * D_CONV), lambda i: (0, i, OFF_CONV // (2 * D_CONV))),
                  st_spec,
                  _resident((CONV_WIDTH, SUBLANES, D_CONV), layer),
                  _resident((1, D_CONV), layer), _resident((1, D_CONV), layer),
                  _resident((1, D_CONV), layer)] + [pl.BlockSpec(memory_space=pl.ANY)] * n_alias,
        out_specs=[pl.BlockSpec((t_len, nbb, D_CONV), lambda i: (0, i, 0)), st_spec],
        out_shape=[jax.ShapeDtypeStruct((t_len, nb, D_CONV), BF16),
                   jax.ShapeDtypeStruct((DEPTH, hist, nb, D_CONV), F32)],
        input_output_aliases={6: 1} if n_alias else {},
        compiler_params=_params(("parallel",)),
        name="conv_sample",
    )(z_t, state_t, w_b8, b, lg, lb, *(() if prev_out is None else (prev_out,)))


def _ssm_kernel(u_ref, h0r_ref, h0i_ref, a_ref, bw_ref, cw_ref, d_ref, y_ref, hr_ref, hi_ref,
                ub_scr, u_scr, s_scr, y_scr, h_scr, *, nb, tc, nt, flat):
    i = pl.program_id(0)
    st2 = 2 * OCT_ST

    @pl.when(i == 0)
    def _():
        for o in range(OCTETS):
            h_scr[:, st2 * o:st2 * o + OCT_ST] = h0r_ref[:, OCT_ST * o:OCT_ST * (o + 1)]
            h_scr[:, st2 * o + OCT_ST:st2 * (o + 1)] = h0i_ref[:, OCT_ST * o:OCT_ST * (o + 1)]

    if flat:
        u_scr[...] = u_ref[...].astype(F32)
    else:
        for o in range(OCTETS):
            cols = slice(OCT_IN * o, OCT_IN * (o + 1))
            for b in range(nb):
                ub_scr[o, b * tc:(b + 1) * tc, :] = u_ref[b, :, cols].astype(F32)
            for t in range(tc):
                u_scr[t * nb:(t + 1) * nb, cols] = ub_scr[o, pl.ds(t, nb, stride=tc), :]

    for o in range(OCTETS):
        ub = u_scr[:, OCT_IN * o:OCT_IN * (o + 1)].astype(BF16)
        s_scr[:, st2 * o:st2 * (o + 1)] = _dot(ub, bw_ref[o])

    for o in range(OCTETS):
        c0 = st2 * o
        re = slice(c0, c0 + OCT_ST)
        im = slice(c0 + OCT_ST, c0 + st2)
        ar = a_ref[:, re]
        ai = a_ref[:, im]

        def run(r0, hr, hi, re=re, im=im, ar=ar, ai=ai):
            for t in range(tc):
                row = r0 + t * nb
                if isinstance(row, int):
                    rows = slice(row, row + SUBLANES)
                else:
                    rows = pl.ds(pl.multiple_of(row, SUBLANES), SUBLANES)
                nr = ar * hr - ai * hi + s_scr[rows, re]
                ni = ar * hi + ai * hr + s_scr[rows, im]
                s_scr[rows, re] = nr
                s_scr[rows, im] = ni
                hr, hi = nr, ni
            return hr, hi

        if nb == SUBLANES:
            hr, hi = run(0, h_scr[:, re], h_scr[:, im])
            h_scr[:, re] = hr
            h_scr[:, im] = hi
        else:
            def rows_body(rg, carry, re=re, im=im, run=run):
                r0 = pl.multiple_of(rg * SUBLANES, SUBLANES)
                hr, hi = run(r0, h_scr[pl.ds(r0, SUBLANES), re], h_scr[pl.ds(r0, SUBLANES), im])
                h_scr[pl.ds(r0, SUBLANES), re] = hr
                h_scr[pl.ds(r0, SUBLANES), im] = hi
                return carry

            lax.fori_loop(0, nb // SUBLANES, rows_body, 0)

    for o in range(OCTETS):
        hb = s_scr[:, st2 * o:st2 * (o + 1)].astype(BF16)
        y_scr[:, OCT_IN * o:OCT_IN * (o + 1)] = _dot(hb, cw_ref[o])

    yv = jax.nn.gelu(y_scr[...] + d_ref[...] * u_scr[...])
    if flat:
        y_ref[...] = yv.astype(y_ref.dtype)
    else:
        y_scr[...] = yv
        for o in range(OCTETS):
            cols = slice(OCT_IN * o, OCT_IN * (o + 1))
            for t in range(tc):
                ub_scr[o, pl.ds(t, nb, stride=tc), :] = y_scr[t * nb:(t + 1) * nb, cols]
            for b in range(nb):
                y_ref[b, :, cols] = ub_scr[o, b * tc:(b + 1) * tc, :].astype(y_ref.dtype)

    @pl.when(i == nt - 1)
    def _():
        for o in range(OCTETS):
            hr_ref[:, OCT_ST * o:OCT_ST * (o + 1)] = h_scr[:, st2 * o:st2 * o + OCT_ST]
            hi_ref[:, OCT_ST * o:OCT_ST * (o + 1)] = h_scr[:, st2 * o + OCT_ST:st2 * (o + 1)]


def _ssm_branch(z, h0r, h0i, h0_layer, a8, bw, cw, dvec, layer, nb, t_len, tc, flat):
    nt = t_len // tc
    m = nb * tc
    kern = functools.partial(_ssm_kernel, nb=nb, tc=tc, nt=nt, flat=flat)
    if flat:
        u_spec = pl.BlockSpec((m, D_SSM), lambda i: (0, OFF_SSM // D_SSM))
        y_spec = pl.BlockSpec((m, D_SSM), lambda i: (0, 0))
        y_shape = jax.ShapeDtypeStruct((m, D_SSM), BF16)
    else:
        u_spec = pl.BlockSpec((nb, tc, D_SSM), lambda i: (0, i, OFF_SSM // D_SSM))
        y_spec = pl.BlockSpec((nb, tc, D_SSM), lambda i: (0, i, 0))
        y_shape = jax.ShapeDtypeStruct((nb, t_len, D_SSM), BF16)
    st_spec = pl.BlockSpec((nb, N_STATE), lambda i: (0, 0))
    if h0_layer is None:
        h0_spec = st_spec
    else:
        h0_spec = pl.BlockSpec((None, nb, N_STATE), lambda i: (h0_layer, 0, 0))
    return pl.pallas_call(
        kern,
        grid=(nt,),
        in_specs=[u_spec, h0_spec, h0_spec,
                  _resident((SUBLANES, 2 * N_STATE)),
                  _resident((OCTETS, OCT_IN, 2 * OCT_ST)),
                  _resident((OCTETS, 2 * OCT_ST, OCT_IN)),
                  _resident((1, D_SSM), layer)],
        out_specs=[y_spec, st_spec, st_spec],
        out_shape=[y_shape, jax.ShapeDtypeStruct((nb, N_STATE), F32),
                   jax.ShapeDtypeStruct((nb, N_STATE), F32)],
        scratch_shapes=[pltpu.VMEM((OCTETS, m, OCT_IN), F32), pltpu.VMEM((m, D_SSM), F32),
                        pltpu.VMEM((m, 2 * N_STATE), F32), pltpu.VMEM((m, D_SSM), F32),
                        pltpu.VMEM((nb, 2 * N_STATE), F32)],
        compiler_params=_params(("arbitrary",)),
        name="ssm_branch",
    )(z, h0r, h0i, a8, bw, cw, dvec)


def _att_prompt_kernel(q0_ref, q1_ref, k0_ref, k1_ref, v0_ref, v1_ref, o_ref,
                       qkv_scr, o_scr, l_scr, bp_scr, *, t_len):
    g_id = pl.program_id(1)
    slopes = _slopes()
    lane = lax.broadcasted_iota(jnp.int32, (BAND, LANES), 1)
    low = lane < HEAD_DIM

    for n, r in enumerate((q0_ref, q1_ref, k0_ref, k1_ref, v0_ref, v1_ref)):
        qkv_scr[n] = r[...].astype(F32)

    def group(g, dil):
        nblk = t_len // dil // BAND
        qi = lax.broadcasted_iota(jnp.int32, (BAND, 2 * BAND), 0)
        kj = lax.broadcasted_iota(jnp.int32, (BAND, 2 * BAND), 1)
        dist = qi + BAND - kj
        valid = (dist >= 0) & (dist <= BAND)
        valid0 = valid & (kj >= BAND)
        distf = dist.astype(F32)
        for h in range(HEADS_PER_GROUP):
            bias = (-slopes[g * HEADS_PER_GROUP + h] * dil) * distf
            bp_scr[0, h] = jnp.where(valid0, bias, NEG)
            bp_scr[1, h] = jnp.where(valid, bias, NEG)

        def rows(start):
            if dil == 1:
                return pl.ds(start, BAND)
            return pl.ds(start, BAND, stride=dil)

        def block(blk, carry):
            r = blk // nblk
            n = blk % nblk
            start_c = r + dil * BAND * n
            start_p = r + dil * BAND * jnp.maximum(n - 1, 0)
            sel = jnp.minimum(n, 1)
            for pair in range(2):
                qp = qkv_scr[pair, rows(start_c), :] * ATT_SCALE
                kp = jnp.concatenate([qkv_scr[2 + pair, rows(start_p), :],
                                      qkv_scr[2 + pair, rows(start_c), :]], axis=0).astype(BF16)
                vp = jnp.concatenate([qkv_scr[4 + pair, rows(start_p), :],
                                      qkv_scr[4 + pair, rows(start_c), :]], axis=0).astype(BF16)
                o_h, l_h = [], []
                for hh in range(2):
                    h = 2 * pair + hh
                    qm = jnp.where(low if hh == 0 else jnp.logical_not(low), qp, 0.0).astype(BF16)
                    t = _dot_nt(qm, kp) + bp_scr[sel, h]
                    mx = jnp.max(t, axis=-1, keepdims=True)
                    p = jnp.exp(t - mx)
                    den = jnp.sum(p, axis=-1, keepdims=True)
                    pv = _dot(p.astype(BF16), vp)
                    o_h.append(pv / den)
                    l_h.append(jnp.broadcast_to(mx + jnp.log(den), (BAND, LANES)))
                o_scr[2 * g + pair, rows(start_c), :] = jnp.where(low, o_h[0], o_h[1])
                l_scr[2 * g + pair, rows(start_c), :] = jnp.where(low, l_h[0], l_h[1])
            return carry

        lax.fori_loop(0, dil * nblk, block, 0, unroll=4)

    for g, (window, dil) in enumerate(ATT_PATTERNS):
        @pl.when(g_id == g)
        def _(g=g, dil=dil):
            group(g, dil)

    @pl.when(g_id == len(ATT_PATTERNS) - 1)
    def _():
        tr = 256

        def comb(i, carry):
            r0 = pl.multiple_of(i * tr, tr)
            for pair in range(2):
                l0 = l_scr[pair, pl.ds(r0, tr), :]
                l1 = l_scr[2 + pair, pl.ds(r0, tr), :]
                l2 = l_scr[4 + pair, pl.ds(r0, tr), :]
                mx = jnp.maximum(jnp.maximum(l0, l1), l2)
                w0 = jnp.exp(l0 - mx)
                w1 = jnp.exp(l1 - mx)
                w2 = jnp.exp(l2 - mx)
                num = (w0 * o_scr[pair, pl.ds(r0, tr), :] + w1 * o_scr[2 + pair, pl.ds(r0, tr), :]
                       + w2 * o_scr[4 + pair, pl.ds(r0, tr), :])
                o_ref[pl.ds(r0, tr), LANES * pair:LANES * (pair + 1)] = (
                    num / (w0 + w1 + w2)).astype(o_ref.dtype)
            return carry

        lax.fori_loop(0, t_len // tr, comb, 0)


def _att_prompt(z3):
    nbatch, t_len, _ = z3.shape
    ng = len(ATT_PATTERNS)
    kern = functools.partial(_att_prompt_kernel, t_len=t_len)

    def spec(off, pair):
        return pl.BlockSpec((None, t_len, LANES), lambda b, g: (b, 0, off // LANES + 2 * g + pair))

    return pl.pallas_call(
        kern,
        grid=(nbatch, ng),
        in_specs=[spec(off, pair) for off in (OFF_Q, OFF_K, OFF_V) for pair in range(2)],
        out_specs=pl.BlockSpec((None, t_len, D_ATT_OUT), lambda b, g: (b, 0, 0)),
        out_shape=jax.ShapeDtypeStruct((nbatch, t_len, D_ATT_OUT), BF16),
        scratch_shapes=[pltpu.VMEM((6, t_len, LANES), F32),
                        pltpu.VMEM((2 * ng, t_len, LANES), F32),
                        pltpu.VMEM((2 * ng, t_len, LANES), F32),
                        pltpu.VMEM((2, HEADS_PER_GROUP, BAND, 2 * BAND), F32)],
        compiler_params=_params(("parallel", "arbitrary")),
        name="att_prompt",
    )(z3, z3, z3, z3, z3, z3)


QROWS = 8
ATT_S_BB = 2


def _att_sample_stages(q_ref, kn_ref, vn_ref, k0_ref, k1_ref, k2_ref, v0_ref, v1_ref, v2_ref,
                       b0_ref, b1_ref, b2_ref, bn_ref, o_ref):
    kc_refs = (k0_ref, k1_ref, k2_ref)
    vc_refs = (v0_ref, v1_ref, v2_ref)
    bc_refs = (b0_ref, b1_ref, b2_ref)
    w = D_ATT_OUT
    ng = len(ATT_PATTERNS)
    state = {}

    def stage(b, g):
        def run():
            head = lax.broadcasted_iota(jnp.int32, (QROWS, w), 1) // HEAD_DIM
            zpad = jnp.zeros((LANES - QROWS, w), BF16)
            rows = slice(QROWS * b, QROWS * (b + 1))
            outs, lses = state.setdefault(b, ([], []))
            cols = slice(g * w, (g + 1) * w)
            qg = q_ref[rows, cols] * ATT_SCALE
            qblk = jnp.concatenate([jnp.where(head == h, qg, 0.0) for h in range(HEADS_PER_GROUP)],
                                   axis=0).astype(BF16)
            s_c = _dot(qblk, kc_refs[g][b].astype(BF16))
            bias_c = bc_refs[g][...]
            t_c = jnp.where(bias_c > 0.5 * NEG, s_c + bias_c, NEG)
            kn = jnp.concatenate([kn_ref[rows, cols].astype(BF16), zpad], axis=0)
            vn = jnp.concatenate([vn_ref[rows, cols].astype(BF16), zpad], axis=0)
            bias_n = bn_ref[g]
            t_n = jnp.where(bias_n > 0.5 * NEG, _dot_nt(qblk, kn) + bias_n, NEG)
            mx = jnp.maximum(jnp.max(t_c, axis=-1, keepdims=True), jnp.max(t_n, axis=-1, keepdims=True))
            p_c = jnp.exp(t_c - mx)
            p_n = jnp.exp(t_n - mx)
            den = jnp.sum(p_c, axis=-1, keepdims=True) + jnp.sum(p_n, axis=-1, keepdims=True)
            o = _dot_nt(p_c.astype(BF16), vc_refs[g][b].astype(BF16))
            o = (o + _dot(p_n.astype(BF16), vn)) / den
            lse = jnp.broadcast_to(mx + jnp.log(den), o.shape)
            og = jnp.zeros((QROWS, w), F32)
            lg = jnp.zeros((QROWS, w), F32)
            for h in range(HEADS_PER_GROUP):
                og = jnp.where(head == h, o[QROWS * h:QROWS * (h + 1)], og)
                lg = jnp.where(head == h, lse[QROWS * h:QROWS * (h + 1)], lg)
            outs.append(og)
            lses.append(lg)
            if g == ng - 1:
                mx = jnp.maximum(jnp.maximum(lses[0], lses[1]), lses[2])
                ws = [jnp.exp(l - mx) for l in lses]
                o_ref[rows, :] = ((ws[0] * outs[0] + ws[1] * outs[1] + ws[2] * outs[2])
                                  / (ws[0] + ws[1] + ws[2])).astype(o_ref.dtype)

        return run

    return [stage(b, g) for b in range(ATT_S_BB) for g in range(ng)]


def _att_sample_consts(t_new):
    slopes = _slopes()
    rows = HEADS_PER_GROUP * QROWS
    bcs = []
    bn = np.full((3, rows, LANES), NEG, np.float32)
    for g, (window, dil) in enumerate(ATT_PATTERNS):
        bc = np.zeros((rows, window), np.float32)
        pos = np.arange(window)
        for h in range(HEADS_PER_GROUP):
            s = slopes[g * HEADS_PER_GROUP + h]
            for j in range(QROWS):
                r = QROWS * h + j
                if j >= t_new:
                    bn[g, r, 0] = 0.0
                    continue
                dist = window + j - pos
                ok = (dist % dil == 0) & (dist <= window)
                bc[r] = np.where(ok, -s * dist, NEG)
                for i_ in range(j + 1):
                    if (j - i_) % dil == 0:
                        bn[g, r, i_] = -s * (j - i_)
        bcs.append(jnp.asarray(bc))
    return bcs, jnp.asarray(bn)


def _cache_t(c):
    d, nb_, length, nh, hd = c.shape
    return jnp.transpose(c, (0, 1, 3, 4, 2)).reshape(d, nb_, nh * hd, length)


class _AttSampleHost:
    def __init__(self, qkv, kts, vts, consts, layer, seq0, nsteps):
        bb = ATT_S_BB
        w = D_ATT_OUT
        blk0 = seq0 // bb
        bcs, bn = consts
        rows = HEADS_PER_GROUP * QROWS

        def new_spec(part):
            return pl.BlockSpec((bb * QROWS, D_ATT), lambda i: (blk0 + i, part))

        def cache_spec(length):
            return pl.BlockSpec((None, bb, w, length), lambda i: (layer, blk0 + i, 0, 0))

        self.inputs = [qkv, qkv, qkv] + list(kts) + list(vts) + list(bcs) + [bn]
        self.in_specs = ([new_spec(0), new_spec(1), new_spec(2)]
                         + [cache_spec(win) for win, _ in ATT_PATTERNS] * 2
                         + [_resident((rows, win)) for win, _ in ATT_PATTERNS]
                         + [_resident((3, rows, LANES))])
        self.out_spec = pl.BlockSpec((bb * QROWS, w), lambda i: (i, 0))
        self.out_shape = jax.ShapeDtypeStruct((nsteps * bb * QROWS, w), BF16)


N_ATT_HOST_INPUTS = 3 + 3 * len(ATT_PATTERNS) + 1


MERGE_CHUNK = 256

def _merge_kernel(*refs, n_host):
    n_gate = 3 * D_MODEL // GATE_BLOCK
    cy_ref, zs_ref, at_ref = refs[:3]
    gate_refs = refs[3:3 + n_gate]
    n_in = 13 + n_gate
    (x_ref, gm_ref, shf_ref, scf_ref, wco_ref, wsg_ref, wat_ref, wo_ref, gpost_ref,
     gpre_ref) = refs[3 + n_gate:n_in]
    x1_ref, hf_ref = refs[n_in + n_host:n_in + n_host + 2]
    mg_scr = refs[-1]
    if n_host:
        for run in _att_sample_stages(*refs[n_in:n_in + n_host], refs[n_in + n_host + 2]):
            run()

    def gate(branch, lo):
        col = branch * D_MODEL + lo
        ref = gate_refs[col // GATE_BLOCK]
        return _sigmoid(ref[:, col % GATE_BLOCK:col % GATE_BLOCK + MERGE_CHUNK].astype(F32))

    cy = cy_ref[...]
    zs = zs_ref[...]
    at = at_ref[...]
    for c in range(D_MODEL // MERGE_CHUNK):
        lo = c * MERGE_CHUNK
        cols = slice(lo, lo + MERGE_CHUNK)
        a = _dot(cy, wco_ref[:, cols])
        bm = _dot(zs, wsg_ref[:, cols]) * _sigmoid(_dot(zs, wsg_ref[:, D_MODEL + lo:D_MODEL + lo + MERGE_CHUNK]))
        cc = _dot(at, wat_ref[:, cols])
        merged = gate(0, lo) * a + gate(1, lo) * bm + gate(2, lo) * cc
        mg_scr[:, cols] = merged.astype(BF16)
    mo = _dot(mg_scr[...], wo_ref[...])
    x1 = x_ref[...] + gm_ref[...] * _rms(mo, gpost_ref[...])
    x1_ref[...] = x1
    hf_ref[...] = (_rms(x1, gpre_ref[...]) * (1.0 + scf_ref[...]) + shf_ref[...]).astype(BF16)


def _merge(cy, zs, at, z2, x2, mod_arr, mod_spec, w, layer, tm, host=None):
    rows = x2.shape[0]

    def rowspec(width):
        return pl.BlockSpec((tm, width), lambda i: (i, 0))

    n_host = 0 if host is None else len(host.inputs)
    n_gate = 3 * D_MODEL // GATE_BLOCK
    gate_specs = [pl.BlockSpec((tm, GATE_BLOCK), lambda i, k=k: (i, OFF_GATES // GATE_BLOCK + k))
                  for k in range(n_gate)]
    outs = pl.pallas_call(
        functools.partial(_merge_kernel, n_host=n_host),
        grid=(rows // tm,),
        in_specs=[rowspec(D_CONV), rowspec(D_SSM), rowspec(D_ATT_OUT)] + gate_specs
        + [rowspec(D_MODEL), mod_spec(2), mod_spec(3), mod_spec(4),
                  _resident((D_CONV, D_MODEL), layer), _resident((D_SSM, 2 * D_MODEL), layer),
                  _resident((D_ATT_OUT, D_MODEL), layer), _resident((D_MODEL, D_MODEL), layer),
                  _resident((1, D_MODEL), layer), _resident((1, D_MODEL), layer)]
        + ([] if host is None else host.in_specs),
        out_specs=[rowspec(D_MODEL), rowspec(D_MODEL)] + ([] if host is None else [host.out_spec]),
        out_shape=[jax.ShapeDtypeStruct((rows, D_MODEL), F32),
                   jax.ShapeDtypeStruct((rows, D_MODEL), BF16)]
        + ([] if host is None else [host.out_shape]),
        scratch_shapes=[pltpu.VMEM((tm, D_MODEL), BF16)],
        compiler_params=_params(("parallel",)),
        name="merge",
    )(cy, zs, at, *([z2] * n_gate), x2, mod_arr, mod_arr, mod_arr, w["w_conv_out"], w["w_ssm_glu"], w["w_att"],
      w["w_out"], w["g_post_mix"], w["g_pre_ffn"], *([] if host is None else host.inputs))
    return outs


FF_CHUNK = 256


def _ffn_kernel(*refs, n_host):
    hf_ref, x1_ref, gf_ref, w1_ref, w2_ref, gpost_ref = refs[:6]
    o_ref = refs[6 + n_host]
    acc_ref = refs[-1]
    if n_host:
        for run in _att_sample_stages(*refs[6:6 + n_host], refs[7 + n_host]):
            run()
    hf = hf_ref[...]
    for c in range(D_FF // FF_CHUNK):
        lo = c * FF_CHUNK
        a = _dot(hf, w1_ref[:, lo:lo + FF_CHUNK])
        b = _dot(hf, w1_ref[:, D_FF + lo:D_FF + lo + FF_CHUNK])
        f = (_silu(a) * b).astype(BF16)
        part = _dot(f, w2_ref[lo:lo + FF_CHUNK, :])
        if c == 0:
            acc_ref[...] = part
        else:
            acc_ref[...] += part
    o_ref[...] = x1_ref[...] + gf_ref[...] * _rms(acc_ref[...], gpost_ref[...])


def _ffn(hf, x1, mod_arr, mod_spec, w, layer, tm, host=None):
    rows = x1.shape[0]
    n_host = 0 if host is None else len(host.inputs)
    return pl.pallas_call(
        functools.partial(_ffn_kernel, n_host=n_host),
        grid=(rows // tm,),
        in_specs=[pl.BlockSpec((tm, D_MODEL), lambda i: (i, 0)),
                  pl.BlockSpec((tm, D_MODEL), lambda i: (i, 0)),
                  mod_spec(5),
                  _resident((D_MODEL, 2 * D_FF), layer),
                  _resident((D_FF, D_MODEL), layer),
                  _resident((1, D_MODEL), layer)] + ([] if host is None else host.in_specs),
        out_specs=[pl.BlockSpec((tm, D_MODEL), lambda i: (i, 0))]
        + ([] if host is None else [host.out_spec]),
        out_shape=[jax.ShapeDtypeStruct((rows, D_MODEL), F32)]
        + ([] if host is None else [host.out_shape]),
        scratch_shapes=[pltpu.VMEM((tm, D_MODEL), F32)],
        compiler_params=_params(("parallel",)),
        name="ffn",
    )(hf, x1, mod_arr, w["w_ffn_in"], w["w_ffn_out"], w["g_post_ffn"],
      *([] if host is None else host.inputs))


ROW_TILE = 512


def kernel(x_prompt, x_sample, cache_k0, cache_v0, cache_k1, cache_v1, cache_k2, cache_v2, state_conv, state_ssm_re, state_ssm_im, c_prompt, c_sample, w_mod, b_mod, g_pre_mix, g_post_mix, g_pre_ffn, g_post_ffn, w_in, conv_w, conv_b, conv_ln_g, conv_ln_b, w_conv_out, ssm_a_re, ssm_a_im, ssm_log_dt, ssm_b_re, ssm_b_im, ssm_c_re, ssm_c_im, ssm_d, w_ssm_glu, w_att, w_out, w_ffn_in, w_ffn_out):
    nbp, t_p, _ = x_prompt.shape
    nbs, t_s, _ = x_sample.shape
    caches_k = (cache_k0, cache_k1, cache_k2)
    caches_v = (cache_v0, cache_v1, cache_v2)

    tm = ROW_TILE
    rows_p, rows_s = nbp * t_p, nbs * t_s
    per_seq = t_p // tm
    assert rows_s == tm

    mod_p, mod_s = _modulation(c_prompt, jnp.tile(c_sample, (t_s, 1)), w_mod, b_mod)
    mod_p = mod_p.reshape(DEPTH, nbp, 1, 6 * D_MODEL)
    abr, abi, bbr, bbi = _ssm_discretise(ssm_a_re, ssm_a_im, ssm_log_dt, ssm_b_re, ssm_b_im)

    w = dict(
        g_pre_mix=g_pre_mix[:, None], g_post_mix=g_post_mix[:, None],
        g_pre_ffn=g_pre_ffn[:, None], g_post_ffn=g_post_ffn[:, None],
        w_in=w_in.astype(BF16),
        conv_w=jnp.broadcast_to(conv_w[:, :, None, :], (DEPTH, CONV_WIDTH, SUBLANES, D_CONV)),
        conv_b=conv_b[:, None], conv_ln_g=conv_ln_g[:, None], conv_ln_b=conv_ln_b[:, None],
        ssm_d=ssm_d.reshape(DEPTH, 1, D_SSM),
        w_conv_out=w_conv_out.astype(BF16), w_ssm_glu=w_ssm_glu.astype(BF16),
        w_att=w_att.astype(BF16), w_out=w_out.astype(BF16),
        w_ffn_in=w_ffn_in.astype(BF16), w_ffn_out=w_ffn_out.astype(BF16))
    hist_s = jnp.transpose(state_conv, (0, 2, 1, 3))
    h0r_s = state_ssm_re.reshape(DEPTH, nbs, N_STATE)
    h0i_s = state_ssm_im.reshape(DEPTH, nbs, N_STATE)
    prev_p = jnp.zeros((nbp, HALO, D_CONV), F32)
    h0_p = jnp.zeros((nbp, N_STATE), F32)
    kts = [_cache_t(c) for c in caches_k]
    vts = [_cache_t(c) for c in caches_v]
    att_consts = _att_sample_consts(t_s)
    n_host_steps = rows_p // tm
    seqs_per_host = n_host_steps * ATT_S_BB
    assert 2 * seqs_per_host == nbs

    xp = x_prompt.reshape(rows_p, D_MODEL)
    xs = jnp.transpose(x_sample, (1, 0, 2)).reshape(rows_s, D_MODEL)
    prompt_states, sample_states = [], []
    kvt_p = None
    kvt_s = None
    hist_out = None
    for l in range(DEPTH):
        a8, bw, cw = _ssm_pack(abr[l], abi[l], bbr[l], bbi[l], ssm_c_re[l], ssm_c_im[l])

        def spec_p(c, l=l):
            return pl.BlockSpec((None, None, 1, D_MODEL), lambda i: (l, i // per_seq, 0, c))

        def spec_s(c, l=l):
            return pl.BlockSpec((None, tm, D_MODEL), lambda i: (l, i, c))

        z_s, kvt_s = _in_proj_sample(xs, mod_s, spec_s, w["g_pre_mix"], w["w_in"], nbs, t_s, l, kvt_s)
        cy_s, hist_out = _conv_sample(z_s.reshape(t_s, nbs, N_IN), hist_s, w["conv_w"], w["conv_b"],
                                      w["conv_ln_g"], w["conv_ln_b"], l, hist_out)
        zs_s, sre_s, sim_s = _ssm_branch(z_s, h0r_s, h0i_s, l, a8, bw, cw, w["ssm_d"], l,
                                         nb=nbs, t_len=t_s, tc=t_s, flat=True)
        qkv_s = z_s[:, OFF_Q:OFF_Q + 3 * D_ATT].astype(F32).reshape(t_s, nbs, 3 * D_ATT)
        qkv_s = jnp.pad(jnp.transpose(qkv_s, (1, 0, 2)), ((0, 0), (0, QROWS - t_s), (0, 0)))
        qkv_s = qkv_s.reshape(nbs * QROWS, 3 * D_ATT)

        z_p, kvt_p = _in_proj_prompt(xp, mod_p, spec_p, w["g_pre_mix"], w["w_in"], tm,
                                     nbp, t_p, l, kvt_p)
        z3 = z_p.reshape(nbp, t_p, N_IN)
        cy_p, conv_p = _conv_branch(z3, prev_p, None, w["conv_w"], w["conv_b"], w["conv_ln_g"],
                                    w["conv_ln_b"], l, nbp, t_p, bb=1, tq=256, flat=False)
        zs_p, sre_p, sim_p = _ssm_branch(z3, h0_p, h0_p, None, a8, bw, cw, w["ssm_d"], l,
                                         nb=nbp, t_len=t_p, tc=64, flat=False)
        att_p = _att_prompt(z3)
        host_a = _AttSampleHost(qkv_s, kts, vts, att_consts, l, 0, n_host_steps)
        host_b = _AttSampleHost(qkv_s, kts, vts, att_consts, l, seqs_per_host, n_host_steps)
        x1_p, hf_p, att_a = _merge(cy_p.reshape(rows_p, D_CONV), zs_p.reshape(rows_p, D_SSM),
                                   att_p.reshape(rows_p, D_ATT_OUT), z_p, xp, mod_p, spec_p, w, l, tm,
                                   host=host_a)
        xp, att_b = _ffn(hf_p, x1_p, mod_p, spec_p, w, l, tm, host=host_b)

        att_s = jnp.concatenate([att_a, att_b], axis=0).reshape(nbs, QROWS, D_ATT_OUT)[:, :t_s]
        att_s = jnp.transpose(att_s, (1, 0, 2)).reshape(rows_s, D_ATT_OUT)
        x1_s, hf_s = _merge(cy_s.reshape(rows_s, D_CONV), zs_s, att_s, z_s, xs, mod_s, spec_s, w, l, tm)
        xs, = _ffn(hf_s, x1_s, mod_s, spec_s, w, l, tm)

        prompt_states.append((conv_p, sre_p.reshape(nbp, N_SSM_GROUPS, SSM_STATE),
                              sim_p.reshape(nbp, N_SSM_GROUPS, SSM_STATE)))
        sample_states.append((sre_s.reshape(nbs, N_SSM_GROUPS, SSM_STATE),
                              sim_s.reshape(nbs, N_SSM_GROUPS, SSM_STATE)))

    ps = [jnp.transpose(a.reshape(DEPTH, nbp, HEADS_PER_GROUP, HEAD_DIM, a.shape[-1]), (0, 1, 4, 2, 3))
          for a in kvt_p]
    ps += [jnp.stack([s[i] for s in prompt_states]) for i in range(3)]
    ss = [jnp.transpose(a.reshape(DEPTH, t_s, HEADS_PER_GROUP, HEAD_DIM, nbs), (0, 4, 1, 2, 3))
          for a in kvt_s]
    ss.append(jnp.transpose(hist_out, (0, 2, 1, 3)))
    ss += [jnp.stack([s[i] for s in sample_states]) for i in range(2)]
    xs = jnp.transpose(xs.reshape(t_s, nbs, D_MODEL), (1, 0, 2))
    return (xp.reshape(nbp, t_p, D_MODEL), xs, *ps, *ss)
```

```python
import functools

import numpy as np
import jax
import jax.numpy as jnp
from jax import lax
from jax.experimental import pallas as pl
from jax.experimental.pallas import tpu as pltpu

F32 = jnp.float32
BF16 = jnp.bfloat16

D_MODEL = 1024
DEPTH = 2
D_CONV = 512
CONV_WIDTH = 31
D_SSM = 512
SSM_GROUP = 16
N_SSM_GROUPS = 32
SSM_STATE = 64
HEAD_DIM = 64
HEADS_PER_GROUP = 4
ATT_PATTERNS = ((128, 1), (512, 4), (2048, 16))
N_ATT_HEADS = 12
D_ATT = 768
D_ATT_OUT = 256
ATT_SCALE = HEAD_DIM ** -0.5
D_FF = 2816
N_IN = 6912
RMS_EPS = 1e-6
LN_EPS = 1e-5
NEG = -1e30
BAND = 128

OFF_CONV = 0
OFF_SSM = 1024
OFF_Q = 1536
OFF_K = 2304
OFF_V = 3072
OFF_GATES = 3840
GATE_BLOCK = 768
IN_CHUNK = 768

LANES = 128
SUBLANES = 8
HALO = 32
N_STATE = N_SSM_GROUPS * SSM_STATE
OCTETS = 4
OCT_IN = D_SSM // OCTETS
OCT_ST = N_STATE // OCTETS
VMEM_LIMIT = 56 * 1024 * 1024


def _params(sem):
    return pltpu.CompilerParams(dimension_semantics=sem, vmem_limit_bytes=VMEM_LIMIT)


def _resident(shape, layer=None):
    if layer is None:
        return pl.BlockSpec(shape, lambda *_: (0,) * len(shape), pipeline_mode=pl.Buffered(1))
    return pl.BlockSpec((None,) + tuple(shape), lambda *_: (layer,) + (0,) * len(shape),
                        pipeline_mode=pl.Buffered(1))


def _slopes():
    return [2.0 ** (-8.0 * (i + 1) / N_ATT_HEADS) for i in range(N_ATT_HEADS)]


def _rms(x, g):
    return x * lax.rsqrt(jnp.mean(x * x, axis=-1, keepdims=True) + RMS_EPS) * g


def _sigmoid(x):
    return 0.5 * jnp.tanh(0.5 * x) + 0.5


def _silu(x):
    return x * _sigmoid(x)


def _dot(a, b):
    return jnp.dot(a, b, preferred_element_type=F32)


def _dot_nt(a, b):
    return lax.dot_general(a, b, (((1,), (1,)), ((), ())), preferred_element_type=F32)


def _mod_kernel(cp_ref, cs_ref, w_ref, b_ref, op_ref, os_ref):
    w = w_ref[...].astype(BF16)
    op_ref[...] = _dot(_silu(cp_ref[...]).astype(BF16), w) + b_ref[...]
    os_ref[...] = _dot(_silu(cs_ref[...]).astype(BF16), w) + b_ref[...]


def _modulation(c_p, c_s_rows, w_mod, b_mod):
    np_, ns = c_p.shape[0], c_s_rows.shape[0]
    tn = 2048
    return pl.pallas_call(
        _mod_kernel,
        grid=(DEPTH, 6 * D_MODEL // tn),
        in_specs=[pl.BlockSpec((np_, D_MODEL), lambda l, j: (0, 0)),
                  pl.BlockSpec((ns, D_MODEL), lambda l, j: (0, 0)),
                  pl.BlockSpec((None, D_MODEL, tn), lambda l, j: (l, 0, j)),
                  pl.BlockSpec((None, 1, tn), lambda l, j: (l, 0, j))],
        out_specs=[pl.BlockSpec((None, np_, tn), lambda l, j: (l, 0, j)),
                   pl.BlockSpec((None, ns, tn), lambda l, j: (l, 0, j))],
        out_shape=[jax.ShapeDtypeStruct((DEPTH, np_, 6 * D_MODEL), F32),
                   jax.ShapeDtypeStruct((DEPTH, ns, 6 * D_MODEL), F32)],
        compiler_params=_params(("parallel", "parallel")),
        name="modulation",
    )(c_p, c_s_rows, w_mod, b_mod.reshape(DEPTH, 1, 6 * D_MODEL))


def _ssm_disc_kernel(ar_ref, ai_ref, ldt_ref, br_ref, bi_ref, abr_ref, abi_ref, bbr_ref, bbi_ref):
    ar = ar_ref[...]
    ai = ai_ref[...]
    dt = jnp.exp(ldt_ref[...])
    mag = jnp.exp(dt * ar)
    abr = mag * jnp.cos(dt * ai)
    abi = mag * jnp.sin(dt * ai)
    den = ar * ar + ai * ai
    fr = ((abr - 1.0) * ar + abi * ai) / den
    fi = (abi * ar - (abr - 1.0) * ai) / den
    br = br_ref[...]
    bi = bi_ref[...]
    abr_ref[...] = abr
    abi_ref[...] = abi
    bbr_ref[...] = fr * br - fi * bi
    bbi_ref[...] = fr * bi + fi * br


def _ssm_discretise(a_re, a_im, log_dt, b_re, b_im):
    ar = a_re.reshape(DEPTH, 1, N_STATE)
    ai = a_im.reshape(DEPTH, 1, N_STATE)
    ldt = jnp.repeat(log_dt, SSM_STATE, axis=-1).reshape(DEPTH, 1, N_STATE)
    br = jnp.transpose(b_re, (0, 3, 1, 2)).reshape(DEPTH, SSM_GROUP, N_STATE)
    bi = jnp.transpose(b_im, (0, 3, 1, 2)).reshape(DEPTH, SSM_GROUP, N_STATE)
    row = pl.BlockSpec((None, 1, N_STATE), lambda l: (l, 0, 0))
    mat = pl.BlockSpec((None, SSM_GROUP, N_STATE), lambda l: (l, 0, 0))
    return pl.pallas_call(
        _ssm_disc_kernel,
        grid=(DEPTH,),
        in_specs=[row, row, row, mat, mat],
        out_specs=[row, row, mat, mat],
        out_shape=[jax.ShapeDtypeStruct((DEPTH, 1, N_STATE), F32)] * 2
        + [jax.ShapeDtypeStruct((DEPTH, SSM_GROUP, N_STATE), F32)] * 2,
        compiler_params=_params(("parallel",)),
        name="ssm_discretise",
    )(ar, ai, ldt, br, bi)


def _ssm_pack(abr, abi, bbr, bbi, c_re, c_im):
    gpo = N_SSM_GROUPS // OCTETS
    eye = jnp.eye(gpo, dtype=F32)

    def oct_state(v):
        return v.reshape(OCTETS, OCT_ST)

    a8 = jnp.concatenate([oct_state(abr), oct_state(abi)], axis=1).reshape(1, 2 * N_STATE)
    a8 = jnp.broadcast_to(a8, (SUBLANES, 2 * N_STATE))

    def b_tiles(bb):
        t = bb.reshape(SSM_GROUP, OCTETS, gpo, SSM_STATE)
        t = jnp.transpose(t, (1, 2, 0, 3))
        t = t[:, :, :, None, :] * eye[None, :, None, :, None]
        return t.reshape(OCTETS, OCT_IN, OCT_ST)

    bw = jnp.concatenate([b_tiles(bbr), b_tiles(bbi)], axis=2).astype(BF16)

    def c_tiles(cc):
        t = cc.reshape(OCTETS, gpo, SSM_GROUP, SSM_STATE)
        t = jnp.transpose(t, (0, 1, 3, 2))
        t = t[:, :, :, None, :] * eye[None, :, None, :, None]
        return t.reshape(OCTETS, OCT_ST, OCT_IN)

    cw = jnp.concatenate([c_tiles(c_re), -c_tiles(c_im)], axis=1).astype(BF16)
    return a8, bw, cw


def _is_kv_chunk(lo):
    return OFF_K <= lo < OFF_K + 2 * D_ATT


def _in_proj_sample_kernel(*refs, nb, t_len, layer, n_alias):
    x_ref, sh_ref, sc_ref, g_ref, w_ref = refs[:5]
    z_ref = refs[5 + n_alias]
    kvt_refs = refs[6 + n_alias:]
    y = _rms(x_ref[...], g_ref[...])
    h = (y * (1.0 + sc_ref[...]) + sh_ref[...]).astype(BF16)
    for c in range(N_IN // IN_CHUNK):
        lo = c * IN_CHUNK
        r = _dot(h, w_ref[:, lo:lo + IN_CHUNK])
        z_ref[:, lo:lo + IN_CHUNK] = r.astype(BF16)
        if _is_kv_chunk(lo):
            which = (lo - OFF_K) // D_ATT
            for g in range(len(ATT_PATTERNS)):
                for t in range(t_len):
                    tile = r[t * nb:(t + 1) * nb, g * D_ATT_OUT:(g + 1) * D_ATT_OUT].T
                    if n_alias:
                        kvt_refs[2 * g + which][t] = tile
                    else:
                        for l in range(DEPTH):
                            kvt_refs[2 * g + which][l, t] = tile if l == layer else jnp.zeros_like(tile)


def _in_proj_sample(x2, mod_arr, mod_spec, g_pre, w_bf, nb, t_len, layer, prev_kvt):
    rows = x2.shape[0]
    n_alias = 0 if prev_kvt is None else len(prev_kvt)
    kern = functools.partial(_in_proj_sample_kernel, nb=nb, t_len=t_len, layer=layer, n_alias=n_alias)
    n_kv = 2 * len(ATT_PATTERNS)
    if n_alias:
        kv_spec = pl.BlockSpec((None, t_len, D_ATT_OUT, nb), lambda i: (layer, 0, 0, 0))
    else:
        kv_spec = pl.BlockSpec((DEPTH, t_len, D_ATT_OUT, nb), lambda i: (0, 0, 0, 0))
    outs = pl.pallas_call(
        kern,
        grid=(1,),
        in_specs=[pl.BlockSpec((rows, D_MODEL), lambda i: (0, 0)),
                  mod_spec(0), mod_spec(1),
                  _resident((1, D_MODEL), layer),
                  _resident((D_MODEL, N_IN), layer)] + [pl.BlockSpec(memory_space=pl.ANY)] * n_alias,
        out_specs=[pl.BlockSpec((rows, N_IN), lambda i: (0, 0))]
        + [kv_spec] * n_kv,
        out_shape=[jax.ShapeDtypeStruct((rows, N_IN), BF16)]
        + [jax.ShapeDtypeStruct((DEPTH, t_len, D_ATT_OUT, nb), F32)] * n_kv,
        input_output_aliases={5 + k: 1 + k for k in range(n_alias)},
        compiler_params=_params(("arbitrary",)),
        name="in_proj_sample",
    )(x2, mod_arr, mod_arr, g_pre, w_bf, *(prev_kvt or ()))
    return outs[0], list(outs[1:])


def _in_proj_prompt_kernel(*refs, keeps, tm, layer, n_alias):
    x_ref, sh_ref, sc_ref, g_ref, w_ref = refs[:5]
    z_ref = refs[5 + n_alias]
    kvt_refs = refs[6 + n_alias:]
    y = _rms(x_ref[...], g_ref[...])
    h = (y * (1.0 + sc_ref[...]) + sh_ref[...]).astype(BF16)
    for c in range(N_IN // IN_CHUNK):
        lo = c * IN_CHUNK
        r = _dot(h, w_ref[:, lo:lo + IN_CHUNK])
        z_ref[:, lo:lo + IN_CHUNK] = r.astype(BF16)
        if _is_kv_chunk(lo):
            which = (lo - OFF_K) // D_ATT
            for g, keep in enumerate(keeps):
                kept = min(keep, tm)
                tile = r[tm - kept:, g * D_ATT_OUT:(g + 1) * D_ATT_OUT].T
                if n_alias:
                    kvt_refs[2 * g + which][...] = tile
                else:
                    for l in range(DEPTH):
                        kvt_refs[2 * g + which][l] = tile if l == layer else jnp.zeros_like(tile)


def _in_proj_prompt(x2, mod_arr, mod_spec, g_pre, w_bf, tm, nbatch, t_len, layer, prev_kvt):
    rows = x2.shape[0]
    per_seq = t_len // tm
    keeps = tuple(min(window, t_len) for window, _ in ATT_PATTERNS)
    n_alias = 0 if prev_kvt is None else len(prev_kvt)
    kern = functools.partial(_in_proj_prompt_kernel, keeps=keeps, tm=tm, layer=layer, n_alias=n_alias)
    kvt_specs, kvt_shapes = [], []
    for keep in keeps:
        first = per_seq - max(keep // tm, 1)
        kept = min(keep, tm)
        for _ in range(2):
            kvt_specs.append(pl.BlockSpec(
                (None if n_alias else DEPTH, None, D_ATT_OUT, kept),
                lambda i, first=first: (layer if n_alias else 0, i // per_seq, 0,
                                        jnp.maximum(i % per_seq - first, 0))))
            kvt_shapes.append(jax.ShapeDtypeStruct((DEPTH, nbatch, D_ATT_OUT, keep), F32))
    outs = pl.pallas_call(
        kern,
        grid=(rows // tm,),
        in_specs=[pl.BlockSpec((tm, D_MODEL), lambda i: (i, 0)),
                  mod_spec(0), mod_spec(1),
                  _resident((1, D_MODEL), layer),
                  _resident((D_MODEL, N_IN), layer)] + [pl.BlockSpec(memory_space=pl.ANY)] * n_alias,
        out_specs=[pl.BlockSpec((tm, N_IN), lambda i: (i, 0))] + kvt_specs,
        out_shape=[jax.ShapeDtypeStruct((rows, N_IN), BF16)] + kvt_shapes,
        input_output_aliases={5 + k: 1 + k for k in range(n_alias)},
        compiler_params=_params(("arbitrary",)),
        name="in_proj_prompt",
    )(x2, mod_arr, mod_arr, g_pre, w_bf, *(prev_kvt or ()))
    return outs[0], list(outs[1:])


def _conv_kernel(a_ref, prev_ref, w_ref, b_ref, lg_ref, lb_ref, y_ref, st_ref, ubuf, sh_scr,
                 *, bb, tq, nt, sub, flat):
    i = pl.program_id(1)
    base = HALO - (CONV_WIDTH - 1)
    span = tq + HALO - SUBLANES
    nrt = max(sub // SUBLANES, 1)
    rpt = min(sub, SUBLANES)
    for b in range(bb):
        @pl.when(i == 0)
        def _():
            ubuf[b, 0:HALO, :] = prev_ref[b]

        @pl.when(i > 0)
        def _():
            ubuf[b, 0:HALO, :] = ubuf[b, tq:tq + HALO, :]

        a = (a_ref[b * tq:(b + 1) * tq, :] if flat else a_ref[b]).astype(F32)
        ubuf[b, HALO:HALO + tq, :] = a[:, :D_CONV] * _sigmoid(a[:, D_CONV:])
        for r in range(1, SUBLANES):
            sh_scr[r - 1, 0:span, :] = ubuf[b, r:r + span, :]
        for r0 in range(0, tq, sub):
            acc = jnp.zeros((nrt, rpt, D_CONV), F32) + b_ref[...]
            for j in range(CONV_WIDTH):
                off = base + j
                al = r0 + (off // SUBLANES) * SUBLANES
                if off % SUBLANES == 0:
                    src = ubuf[b, al:al + sub, :]
                else:
                    src = sh_scr[off % SUBLANES - 1, al:al + sub, :]
                acc = acc + w_ref[j, 0:rpt, :][None] * src.reshape(nrt, rpt, D_CONV)
            acc = acc.reshape(sub, D_CONV)
            mu = jnp.mean(acc, axis=-1, keepdims=True)
            xc = acc - mu
            yn = xc * lax.rsqrt(jnp.mean(xc * xc, axis=-1, keepdims=True) + LN_EPS)
            yv = _silu(yn * lg_ref[...] + lb_ref[...]).astype(y_ref.dtype)
            if flat:
                y_ref[b * tq + r0:b * tq + r0 + sub, :] = yv
            else:
                y_ref[b, r0:r0 + sub, :] = yv

        @pl.when(i == nt - 1)
        def _():
            st_ref[b] = ubuf[b, base + tq:base + tq + CONV_WIDTH - 1, :]


def _conv_branch(z, prev_pad, prev_layer, w_b8, b, lg, lb, layer, nbatch, t_len, bb, tq, flat):
    nt = t_len // tq
    sub = min(tq, 32)
    kern = functools.partial(_conv_kernel, bb=bb, tq=tq, nt=nt, sub=sub, flat=flat)
    tq_pad = max(tq, SUBLANES)
    col = OFF_CONV // (2 * D_CONV)
    if flat:
        a_spec = pl.BlockSpec((bb * tq, 2 * D_CONV), lambda i, j: (i, col))
        y_spec = pl.BlockSpec((bb * tq, D_CONV), lambda i, j: (i, 0))
        y_shape = jax.ShapeDtypeStruct((nbatch * t_len, D_CONV), BF16)
    else:
        a_spec = pl.BlockSpec((bb, tq, 2 * D_CONV), lambda i, j: (i, j, col))
        y_spec = pl.BlockSpec((bb, tq, D_CONV), lambda i, j: (i, j, 0))
        y_shape = jax.ShapeDtypeStruct((nbatch, t_len, D_CONV), BF16)
    if prev_layer is None:
        prev_spec = pl.BlockSpec((bb, HALO, D_CONV), lambda i, j: (i, 0, 0))
    else:
        prev_spec = pl.BlockSpec((None, bb, HALO, D_CONV), lambda i, j: (prev_layer, i, 0, 0))
    return pl.pallas_call(
        kern,
        grid=(nbatch // bb, nt),
        in_specs=[a_spec, prev_spec,
                  _resident((CONV_WIDTH, SUBLANES, D_CONV), layer),
                  _resident((1, D_CONV), layer), _resident((1, D_CONV), layer),
                  _resident((1, D_CONV), layer)],
        out_specs=[y_spec, pl.BlockSpec((bb, CONV_WIDTH - 1, D_CONV), lambda i, j: (i, 0, 0))],
        out_shape=[y_shape, jax.ShapeDtypeStruct((nbatch, CONV_WIDTH - 1, D_CONV), F32)],
        scratch_shapes=[pltpu.VMEM((bb, HALO + tq_pad + SUBLANES, D_CONV), F32),
                        pltpu.VMEM((SUBLANES - 1, HALO + tq_pad, D_CONV), F32)],
        compiler_params=_params(("parallel", "arbitrary")),
        name="conv_branch",
    )(z, prev_pad, w_b8, b, lg, lb)


def _conv_sample_kernel(*refs, t_len, nbb, layer, n_alias):
    a_ref, prev_ref, w_ref, b_ref, lg_ref, lb_ref = refs[:6]
    y_ref, st_ref = refs[6 + n_alias:]
    hist = CONV_WIDTH - 1
    if not n_alias:
        for l in range(DEPTH):
            if l != layer:
                st_ref[l] = jnp.zeros((hist, nbb, D_CONV), F32)
        st_ref = st_ref.at[layer]
    for k in range(hist - t_len):
        st_ref[k] = prev_ref[k + t_len]
    for t in range(t_len):
        a = a_ref[t].astype(F32)
        st_ref[hist - t_len + t] = a[:, :D_CONV] * _sigmoid(a[:, D_CONV:])
    nrt = nbb // SUBLANES
    for t in range(t_len):
        acc = jnp.zeros((nrt, SUBLANES, D_CONV), F32) + b_ref[...]
        for j in range(CONV_WIDTH):
            k = t + j
            src = prev_ref[k] if k < hist else st_ref[k - t_len]
            acc = acc + w_ref[j][None] * src.reshape(nrt, SUBLANES, D_CONV)
        acc = acc.reshape(nbb, D_CONV)
        mu = jnp.mean(acc, axis=-1, keepdims=True)
        xc = acc - mu
        yn = xc * lax.rsqrt(jnp.mean(xc * xc, axis=-1, keepdims=True) + LN_EPS)
        y_ref[t] = _silu(yn * lg_ref[...] + lb_ref[...]).astype(y_ref.dtype)


def _conv_sample(z_t, state_t, w_b8, b, lg, lb, layer, prev_out):
    t_len, nb, _ = z_t.shape
    nbb = 32
    hist = CONV_WIDTH - 1
    n_alias = 0 if prev_out is None else 1
    kern = functools.partial(_conv_sample_kernel, t_len=t_len, nbb=nbb, layer=layer, n_alias=n_alias)
    st_spec = pl.BlockSpec((None, hist, nbb, D_CONV), lambda i: (layer, 0, i, 0))
    if n_alias:
        st_out_spec = st_spec
    else:
        st_out_spec = pl.BlockSpec((DEPTH, hist, nbb, D_CONV), lambda i: (0, 0, i, 0))
    return pl.pallas_call(
        kern,
        grid=(nb // nbb,),
        in_specs=[pl.BlockSpec((t_len, nbb, 2 * D_CONV), lambda i: (0, i, OFF_CONV // (2 * D_CONV))),
                  st_spec,
                  _resident((CONV_WIDTH, SUBLANES, D_CONV), layer),
                  _resident((1, D_CONV), layer), _resident((1, D_CONV), layer),
                  _resident((1, D_CONV), layer)] + [pl.BlockSpec(memory_space=pl.ANY)] * n_alias,
        out_specs=[pl.BlockSpec((t_len, nbb, D_CONV), lambda i: (0, i, 0)), st_out_spec],
        out_shape=[jax.ShapeDtypeStruct((t_len, nb, D_CONV), BF16),
                   jax.ShapeDtypeStruct((DEPTH, hist, nb, D_CONV), F32)],
        input_output_aliases={6: 1} if n_alias else {},
        compiler_params=_params(("parallel",)),
        name="conv_sample",
    )(z_t, state_t, w_b8, b, lg, lb, *(() if prev_out is None else (prev_out,)))


def _ssm_kernel(u_ref, h0r_ref, h0i_ref, a_ref, bw_ref, cw_ref, d_ref, y_ref, hr_ref, hi_ref,
                ub_scr, u_scr, s_scr, y_scr, h_scr, *, nb, tc, nt, flat):
    i = pl.program_id(0)
    st2 = 2 * OCT_ST

    @pl.when(i == 0)
    def _():
        for o in range(OCTETS):
            h_scr[:, st2 * o:st2 * o + OCT_ST] = h0r_ref[:, OCT_ST * o:OCT_ST * (o + 1)]
            h_scr[:, st2 * o + OCT_ST:st2 * (o + 1)] = h0i_ref[:, OCT_ST * o:OCT_ST * (o + 1)]

    if flat:
        u_scr[...] = u_ref[...].astype(F32)
    else:
        for o in range(OCTETS):
            cols = slice(OCT_IN * o, OCT_IN * (o + 1))
            for b in range(nb):
                ub_scr[o, b * tc:(b + 1) * tc, :] = u_ref[b, :, cols].astype(F32)
            for t in range(tc):
                u_scr[t * nb:(t + 1) * nb, cols] = ub_scr[o, pl.ds(t, nb, stride=tc), :]

    for o in range(OCTETS):
        ub = u_scr[:, OCT_IN * o:OCT_IN * (o + 1)].astype(BF16)
        s_scr[:, st2 * o:st2 * (o + 1)] = _dot(ub, bw_ref[o])

    for o in range(OCTETS):
        c0 = st2 * o
        re = slice(c0, c0 + OCT_ST)
        im = slice(c0 + OCT_ST, c0 + st2)
        ar = a_ref[:, re]
        ai = a_ref[:, im]

        def run(r0, hr, hi, re=re, im=im, ar=ar, ai=ai):
            for t in range(tc):
                row = r0 + t * nb
                if isinstance(row, int):
                    rows = slice(row, row + SUBLANES)
                else:
                    rows = pl.ds(pl.multiple_of(row, SUBLANES), SUBLANES)
                nr = ar * hr - ai * hi + s_scr[rows, re]
                ni = ar * hi + ai * hr + s_scr[rows, im]
                s_scr[rows, re] = nr
                s_scr[rows, im] = ni
                hr, hi = nr, ni
            return hr, hi

        if nb == SUBLANES:
            hr, hi = run(0, h_scr[:, re], h_scr[:, im])
            h_scr[:, re] = hr
            h_scr[:, im] = hi
        else:
            def rows_body(rg, carry, re=re, im=im, run=run):
                r0 = pl.multiple_of(rg * SUBLANES, SUBLANES)
                hr, hi = run(r0, h_scr[pl.ds(r0, SUBLANES), re], h_scr[pl.ds(r0, SUBLANES), im])
                h_scr[pl.ds(r0, SUBLANES), re] = hr
                h_scr[pl.ds(r0, SUBLANES), im] = hi
                return carry

            lax.fori_loop(0, nb // SUBLANES, rows_body, 0)

    for o in range(OCTETS):
        hb = s_scr[:, st2 * o:st2 * (o + 1)].astype(BF16)
        y_scr[:, OCT_IN * o:OCT_IN * (o + 1)] = _dot(hb, cw_ref[o])

    yv = jax.nn.gelu(y_scr[...] + d_ref[...] * u_scr[...])
    if flat:
        y_ref[...] = yv.astype(y_ref.dtype)
    else:
        y_scr[...] = yv
        for o in range(OCTETS):
            cols = slice(OCT_IN * o, OCT_IN * (o + 1))
            for t in range(tc):
                ub_scr[o, pl.ds(t, nb, stride=tc), :] = y_scr[t * nb:(t + 1) * nb, cols]
            for b in range(nb):
                y_ref[b, :, cols] = ub_scr[o, b * tc:(b + 1) * tc, :].astype(y_ref.dtype)

    @pl.when(i == nt - 1)
    def _():
        for o in range(OCTETS):
            hr_ref[:, OCT_ST * o:OCT_ST * (o + 1)] = h_scr[:, st2 * o:st2 * o + OCT_ST]
            hi_ref[:, OCT_ST * o:OCT_ST * (o + 1)] = h_scr[:, st2 * o + OCT_ST:st2 * (o + 1)]


def _ssm_branch(z, h0r, h0i, h0_layer, a8, bw, cw, dvec, layer, nb, t_len, tc, flat):
    nt = t_len // tc
    m = nb * tc
    kern = functools.partial(_ssm_kernel, nb=nb, tc=tc, nt=nt, flat=flat)
    if flat:
        u_spec = pl.BlockSpec((m, D_SSM), lambda i: (0, OFF_SSM // D_SSM))
        y_spec = pl.BlockSpec((m, D_SSM), lambda i: (0, 0))
        y_shape = jax.ShapeDtypeStruct((m, D_SSM), BF16)
    else:
        u_spec = pl.BlockSpec((nb, tc, D_SSM), lambda i: (0, i, OFF_SSM // D_SSM))
        y_spec = pl.BlockSpec((nb, tc, D_SSM), lambda i: (0, i, 0))
        y_shape = jax.ShapeDtypeStruct((nb, t_len, D_SSM), BF16)
    st_spec = pl.BlockSpec((nb, N_STATE), lambda i: (0, 0))
    if h0_layer is None:
        h0_spec = st_spec
    else:
        h0_spec = pl.BlockSpec((None, nb, N_STATE), lambda i: (h0_layer, 0, 0))
    return pl.pallas_call(
        kern,
        grid=(nt,),
        in_specs=[u_spec, h0_spec, h0_spec,
                  _resident((SUBLANES, 2 * N_STATE)),
                  _resident((OCTETS, OCT_IN, 2 * OCT_ST)),
                  _resident((OCTETS, 2 * OCT_ST, OCT_IN)),
                  _resident((1, D_SSM), layer)],
        out_specs=[y_spec, st_spec, st_spec],
        out_shape=[y_shape, jax.ShapeDtypeStruct((nb, N_STATE), F32),
                   jax.ShapeDtypeStruct((nb, N_STATE), F32)],
        scratch_shapes=[pltpu.VMEM((OCTETS, m, OCT_IN), F32), pltpu.VMEM((m, D_SSM), F32),
                        pltpu.VMEM((m, 2 * N_STATE), F32), pltpu.VMEM((m, D_SSM), F32),
                        pltpu.VMEM((nb, 2 * N_STATE), F32)],
        compiler_params=_params(("arbitrary",)),
        name="ssm_branch",
    )(z, h0r, h0i, a8, bw, cw, dvec)


def _att_prompt_kernel(q0_ref, q1_ref, k0_ref, k1_ref, v0_ref, v1_ref, o_ref,
                       qkv_scr, o_scr, l_scr, bp_scr, *, t_len):
    g_id = pl.program_id(1)
    slopes = _slopes()
    lane = lax.broadcasted_iota(jnp.int32, (BAND, LANES), 1)
    low = lane < HEAD_DIM

    for n, r in enumerate((q0_ref, q1_ref, k0_ref, k1_ref, v0_ref, v1_ref)):
        qkv_scr[n] = r[...].astype(F32)

    def group(g, dil):
        nblk = t_len // dil // BAND
        qi = lax.broadcasted_iota(jnp.int32, (BAND, 2 * BAND), 0)
        kj = lax.broadcasted_iota(jnp.int32, (BAND, 2 * BAND), 1)
        dist = qi + BAND - kj
        valid = (dist >= 0) & (dist <= BAND)
        valid0 = valid & (kj >= BAND)
        distf = dist.astype(F32)
        for h in range(HEADS_PER_GROUP):
            bias = (-slopes[g * HEADS_PER_GROUP + h] * dil) * distf
            bp_scr[0, h] = jnp.where(valid0, bias, NEG)
            bp_scr[1, h] = jnp.where(valid, bias, NEG)

        def rows(start):
            if dil == 1:
                return pl.ds(start, BAND)
            return pl.ds(start, BAND, stride=dil)

        def block(blk, carry):
            r = blk // nblk
            n = blk % nblk
            start_c = r + dil * BAND * n
            start_p = r + dil * BAND * jnp.maximum(n - 1, 0)
            sel = jnp.minimum(n, 1)
            for pair in range(2):
                qp = qkv_scr[pair, rows(start_c), :] * ATT_SCALE
                kp = jnp.concatenate([qkv_scr[2 + pair, rows(start_p), :],
                                      qkv_scr[2 + pair, rows(start_c), :]], axis=0).astype(BF16)
                vp = jnp.concatenate([qkv_scr[4 + pair, rows(start_p), :],
                                      qkv_scr[4 + pair, rows(start_c), :]], axis=0).astype(BF16)
                o_h, l_h = [], []
                for hh in range(2):
                    h = 2 * pair + hh
                    qm = jnp.where(low if hh == 0 else jnp.logical_not(low), qp, 0.0).astype(BF16)
                    t = _dot_nt(qm, kp) + bp_scr[sel, h]
                    mx = jnp.max(t, axis=-1, keepdims=True)
                    p = jnp.exp(t - mx)
                    den = jnp.sum(p, axis=-1, keepdims=True)
                    pv = _dot(p.astype(BF16), vp)
                    o_h.append(pv / den)
                    l_h.append(jnp.broadcast_to(mx + jnp.log(den), (BAND, LANES)))
                o_scr[2 * g + pair, rows(start_c), :] = jnp.where(low, o_h[0], o_h[1])
                l_scr[2 * g + pair, rows(start_c), :] = jnp.where(low, l_h[0], l_h[1])
            return carry

        lax.fori_loop(0, dil * nblk, block, 0, unroll=4)

    for g, (window, dil) in enumerate(ATT_PATTERNS):
        @pl.when(g_id == g)
        def _(g=g, dil=dil):
            group(g, dil)

    @pl.when(g_id == len(ATT_PATTERNS) - 1)
    def _():
        tr = 256

        def comb(i, carry):
            r0 = pl.multiple_of(i * tr, tr)
            for pair in range(2):
                l0 = l_scr[pair, pl.ds(r0, tr), :]
                l1 = l_scr[2 + pair, pl.ds(r0, tr), :]
                l2 = l_scr[4 + pair, pl.ds(r0, tr), :]
                mx = jnp.maximum(jnp.maximum(l0, l1), l2)
                w0 = jnp.exp(l0 - mx)
                w1 = jnp.exp(l1 - mx)
                w2 = jnp.exp(l2 - mx)
                num = (w0 * o_scr[pair, pl.ds(r0, tr), :] + w1 * o_scr[2 + pair, pl.ds(r0, tr), :]
                       + w2 * o_scr[4 + pair, pl.ds(r0, tr), :])
                o_ref[pl.ds(r0, tr), LANES * pair:LANES * (pair + 1)] = (
                    num / (w0 + w1 + w2)).astype(o_ref.dtype)
            return carry

        lax.fori_loop(0, t_len // tr, comb, 0)


def _att_prompt(z3):
    nbatch, t_len, _ = z3.shape
    ng = len(ATT_PATTERNS)
    kern = functools.partial(_att_prompt_kernel, t_len=t_len)

    def spec(off, pair):
        return pl.BlockSpec((None, t_len, LANES), lambda b, g: (b, 0, off // LANES + 2 * g + pair))

    return pl.pallas_call(
        kern,
        grid=(nbatch, ng),
        in_specs=[spec(off, pair) for off in (OFF_Q, OFF_K, OFF_V) for pair in range(2)],
        out_specs=pl.BlockSpec((None, t_len, D_ATT_OUT), lambda b, g: (b, 0, 0)),
        out_shape=jax.ShapeDtypeStruct((nbatch, t_len, D_ATT_OUT), BF16),
        scratch_shapes=[pltpu.VMEM((6, t_len, LANES), F32),
                        pltpu.VMEM((2 * ng, t_len, LANES), F32),
                        pltpu.VMEM((2 * ng, t_len, LANES), F32),
                        pltpu.VMEM((2, HEADS_PER_GROUP, BAND, 2 * BAND), F32)],
        compiler_params=_params(("parallel", "arbitrary")),
        name="att_prompt",
    )(z3, z3, z3, z3, z3, z3)


QROWS = 8
ATT_S_BB = 2


def _att_sample_stages(q_ref, kn_ref, vn_ref, k0_ref, k1_ref, k2_ref, v0_ref, v1_ref, v2_ref,
                       b0_ref, b1_ref, b2_ref, bn_ref, o_ref):
    kc_refs = (k0_ref, k1_ref, k2_ref)
    vc_refs = (v0_ref, v1_ref, v2_ref)
    bc_refs = (b0_ref, b1_ref, b2_ref)
    w = D_ATT_OUT
    ng = len(ATT_PATTERNS)
    state = {}

    def stage(b, g):
        def run():
            head = lax.broadcasted_iota(jnp.int32, (QROWS, w), 1) // HEAD_DIM
            zpad = jnp.zeros((LANES - QROWS, w), BF16)
            rows = slice(QROWS * b, QROWS * (b + 1))
            outs, lses = state.setdefault(b, ([], []))
            cols = slice(g * w, (g + 1) * w)
            qg = q_ref[rows, cols] * ATT_SCALE
            qblk = jnp.concatenate([jnp.where(head == h, qg, 0.0) for h in range(HEADS_PER_GROUP)],
                                   axis=0).astype(BF16)
            s_c = _dot(qblk, kc_refs[g][b].astype(BF16))
            bias_c = bc_refs[g][...]
            t_c = jnp.where(bias_c > 0.5 * NEG, s_c + bias_c, NEG)
            kn = jnp.concatenate([kn_ref[rows, cols].astype(BF16), zpad], axis=0)
            vn = jnp.concatenate([vn_ref[rows, cols].astype(BF16), zpad], axis=0)
            bias_n = bn_ref[g]
            t_n = jnp.where(bias_n > 0.5 * NEG, _dot_nt(qblk, kn) + bias_n, NEG)
            mx = jnp.maximum(jnp.max(t_c, axis=-1, keepdims=True), jnp.max(t_n, axis=-1, keepdims=True))
            p_c = jnp.exp(t_c - mx)
            p_n = jnp.exp(t_n - mx)
            den = jnp.sum(p_c, axis=-1, keepdims=True) + jnp.sum(p_n, axis=-1, keepdims=True)
            o = _dot_nt(p_c.astype(BF16), vc_refs[g][b].astype(BF16))
            o = (o + _dot(p_n.astype(BF16), vn)) / den
            lse = jnp.broadcast_to(mx + jnp.log(den), o.shape)
            og = jnp.zeros((QROWS, w), F32)
            lg = jnp.zeros((QROWS, w), F32)
            for h in range(HEADS_PER_GROUP):
                og = jnp.where(head == h, o[QROWS * h:QROWS * (h + 1)], og)
                lg = jnp.where(head == h, lse[QROWS * h:QROWS * (h + 1)], lg)
            outs.append(og)
            lses.append(lg)
            if g == ng - 1:
                mx = jnp.maximum(jnp.maximum(lses[0], lses[1]), lses[2])
                ws = [jnp.exp(l - mx) for l in lses]
                o_ref[rows, :] = ((ws[0] * outs[0] + ws[1] * outs[1] + ws[2] * outs[2])
                                  / (ws[0] + ws[1] + ws[2])).astype(o_ref.dtype)

        return run

    return [stage(b, g) for b in range(ATT_S_BB) for g in range(ng)]


def _att_sample_consts(t_new):
    slopes = _slopes()
    rows = HEADS_PER_GROUP * QROWS
    bcs = []
    bn = np.full((3, rows, LANES), NEG, np.float32)
    for g, (window, dil) in enumerate(ATT_PATTERNS):
        bc = np.zeros((rows, window), np.float32)
        pos = np.arange(window)
        for h in range(HEADS_PER_GROUP):
            s = slopes[g * HEADS_PER_GROUP + h]
            for j in range(QROWS):
                r = QROWS * h + j
                if j >= t_new:
                    bn[g, r, 0] = 0.0
                    continue
                dist = window + j - pos
                ok = (dist % dil == 0) & (dist <= window)
                bc[r] = np.where(ok, -s * dist, NEG)
                for i_ in range(j + 1):
                    if (j - i_) % dil == 0:
                        bn[g, r, i_] = -s * (j - i_)
        bcs.append(jnp.asarray(bc))
    return bcs, jnp.asarray(bn)


def _cache_t(c):
    d, nb_, length, nh, hd = c.shape
    return jnp.transpose(c, (0, 1, 3, 4, 2)).reshape(d, nb_, nh * hd, length)


class _AttSampleHost:
    def __init__(self, qkv, kts, vts, consts, layer, seq0, nsteps):
        bb = ATT_S_BB
        w = D_ATT_OUT
        blk0 = seq0 // bb
        bcs, bn = consts
        rows = HEADS_PER_GROUP * QROWS

        def new_spec(part):
            return pl.BlockSpec((bb * QROWS, D_ATT), lambda i: (blk0 + i, part))

        def cache_spec(length):
            return pl.BlockSpec((None, bb, w, length), lambda i: (layer, blk0 + i, 0, 0))

        self.inputs = [qkv, qkv, qkv] + list(kts) + list(vts) + list(bcs) + [bn]
        self.in_specs = ([new_spec(0), new_spec(1), new_spec(2)]
                         + [cache_spec(win) for win, _ in ATT_PATTERNS] * 2
                         + [_resident((rows, win)) for win, _ in ATT_PATTERNS]
                         + [_resident((3, rows, LANES))])
        self.out_spec = pl.BlockSpec((bb * QROWS, w), lambda i: (i, 0))
        self.out_shape = jax.ShapeDtypeStruct((nsteps * bb * QROWS, w), BF16)


N_ATT_HOST_INPUTS = 3 + 3 * len(ATT_PATTERNS) + 1


MERGE_CHUNK = 256

def _merge_kernel(*refs, n_host):
    n_gate = 3 * D_MODEL // GATE_BLOCK
    cy_ref, zs_ref, at_ref = refs[:3]
    gate_refs = refs[3:3 + n_gate]
    n_in = 13 + n_gate
    (x_ref, gm_ref, shf_ref, scf_ref, wco_ref, wsg_ref, wat_ref, wo_ref, gpost_ref,
     gpre_ref) = refs[3 + n_gate:n_in]
    x1_ref, hf_ref = refs[n_in + n_host:n_in + n_host + 2]
    mg_scr = refs[-1]
    if n_host:
        for run in _att_sample_stages(*refs[n_in:n_in + n_host], refs[n_in + n_host + 2]):
            run()

    def gate(branch, lo):
        col = branch * D_MODEL + lo
        ref = gate_refs[col // GATE_BLOCK]
        return _sigmoid(ref[:, col % GATE_BLOCK:col % GATE_BLOCK + MERGE_CHUNK].astype(F32))

    cy = cy_ref[...]
    zs = zs_ref[...]
    at = at_ref[...]
    for c in range(D_MODEL // MERGE_CHUNK):
        lo = c * MERGE_CHUNK
        cols = slice(lo, lo + MERGE_CHUNK)
        a = _dot(cy, wco_ref[:, cols])
        bm = _dot(zs, wsg_ref[:, cols]) * _sigmoid(_dot(zs, wsg_ref[:, D_MODEL + lo:D_MODEL + lo + MERGE_CHUNK]))
        cc = _dot(at, wat_ref[:, cols])
        merged = gate(0, lo) * a + gate(1, lo) * bm + gate(2, lo) * cc
        mg_scr[:, cols] = merged.astype(BF16)
    mo = _dot(mg_scr[...], wo_ref[...])
    x1 = x_ref[...] + gm_ref[...] * _rms(mo, gpost_ref[...])
    x1_ref[...] = x1
    hf_ref[...] = (_rms(x1, gpre_ref[...]) * (1.0 + scf_ref[...]) + shf_ref[...]).astype(BF16)


def _merge(cy, zs, at, z2, x2, mod_arr, mod_spec, w, layer, tm, host=None):
    rows = x2.shape[0]

    def rowspec(width):
        return pl.BlockSpec((tm, width), lambda i: (i, 0))

    n_host = 0 if host is None else len(host.inputs)
    n_gate = 3 * D_MODEL // GATE_BLOCK
    gate_specs = [pl.BlockSpec((tm, GATE_BLOCK), lambda i, k=k: (i, OFF_GATES // GATE_BLOCK + k))
                  for k in range(n_gate)]
    outs = pl.pallas_call(
        functools.partial(_merge_kernel, n_host=n_host),
        grid=(rows // tm,),
        in_specs=[rowspec(D_CONV), rowspec(D_SSM), rowspec(D_ATT_OUT)] + gate_specs
        + [rowspec(D_MODEL), mod_spec(2), mod_spec(3), mod_spec(4),
                  _resident((D_CONV, D_MODEL), layer), _resident((D_SSM, 2 * D_MODEL), layer),
                  _resident((D_ATT_OUT, D_MODEL), layer), _resident((D_MODEL, D_MODEL), layer),
                  _resident((1, D_MODEL), layer), _resident((1, D_MODEL), layer)]
        + ([] if host is None else host.in_specs),
        out_specs=[rowspec(D_MODEL), rowspec(D_MODEL)] + ([] if host is None else [host.out_spec]),
        out_shape=[jax.ShapeDtypeStruct((rows, D_MODEL), F32),
                   jax.ShapeDtypeStruct((rows, D_MODEL), BF16)]
        + ([] if host is None else [host.out_shape]),
        scratch_shapes=[pltpu.VMEM((tm, D_MODEL), BF16)],
        compiler_params=_params(("parallel",)),
        name="merge",
    )(cy, zs, at, *([z2] * n_gate), x2, mod_arr, mod_arr, mod_arr, w["w_conv_out"], w["w_ssm_glu"], w["w_att"],
      w["w_out"], w["g_post_mix"], w["g_pre_ffn"], *([] if host is None else host.inputs))
    return outs


FF_CHUNK = 256


def _ffn_kernel(*refs, n_host):
    hf_ref, x1_ref, gf_ref, w1_ref, w2_ref, gpost_ref = refs[:6]
    o_ref = refs[6 + n_host]
    acc_ref = refs[-1]
    if n_host:
        for run in _att_sample_stages(*refs[6:6 + n_host], refs[7 + n_host]):
            run()
    hf = hf_ref[...]
    for c in range(D_FF // FF_CHUNK):
        lo = c * FF_CHUNK
        a = _dot(hf, w1_ref[:, lo:lo + FF_CHUNK])
        b = _dot(hf, w1_ref[:, D_FF + lo:D_FF + lo + FF_CHUNK])
        f = (_silu(a) * b).astype(BF16)
        part = _dot(f, w2_ref[lo:lo + FF_CHUNK, :])
        if c == 0:
            acc_ref[...] = part
        else:
            acc_ref[...] += part
    o_ref[...] = x1_ref[...] + gf_ref[...] * _rms(acc_ref[...], gpost_ref[...])


def _ffn(hf, x1, mod_arr, mod_spec, w, layer, tm, host=None):
    rows = x1.shape[0]
    n_host = 0 if host is None else len(host.inputs)
    return pl.pallas_call(
        functools.partial(_ffn_kernel, n_host=n_host),
        grid=(rows // tm,),
        in_specs=[pl.BlockSpec((tm, D_MODEL), lambda i: (i, 0)),
                  pl.BlockSpec((tm, D_MODEL), lambda i: (i, 0)),
                  mod_spec(5),
                  _resident((D_MODEL, 2 * D_FF), layer),
                  _resident((D_FF, D_MODEL), layer),
                  _resident((1, D_MODEL), layer)] + ([] if host is None else host.in_specs),
        out_specs=[pl.BlockSpec((tm, D_MODEL), lambda i: (i, 0))]
        + ([] if host is None else [host.out_spec]),
        out_shape=[jax.ShapeDtypeStruct((rows, D_MODEL), F32)]
        + ([] if host is None else [host.out_shape]),
        scratch_shapes=[pltpu.VMEM((tm, D_MODEL), F32)],
        compiler_params=_params(("parallel",)),
        name="ffn",
    )(hf, x1, mod_arr, w["w_ffn_in"], w["w_ffn_out"], w["g_post_ffn"],
      *([] if host is None else host.inputs))


ROW_TILE = 512


def kernel(x_prompt, x_sample, cache_k0, cache_v0, cache_k1, cache_v1, cache_k2, cache_v2, state_conv, state_ssm_re, state_ssm_im, c_prompt, c_sample, w_mod, b_mod, g_pre_mix, g_post_mix, g_pre_ffn, g_post_ffn, w_in, conv_w, conv_b, conv_ln_g, conv_ln_b, w_conv_out, ssm_a_re, ssm_a_im, ssm_log_dt, ssm_b_re, ssm_b_im, ssm_c_re, ssm_c_im, ssm_d, w_ssm_glu, w_att, w_out, w_ffn_in, w_ffn_out):
    nbp, t_p, _ = x_prompt.shape
    nbs, t_s, _ = x_sample.shape
    caches_k = (cache_k0, cache_k1, cache_k2)
    caches_v = (cache_v0, cache_v1, cache_v2)

    tm = ROW_TILE
    rows_p, rows_s = nbp * t_p, nbs * t_s
    per_seq = t_p // tm
    assert rows_s == tm

    mod_p, mod_s = _modulation(c_prompt, jnp.tile(c_sample, (t_s, 1)), w_mod, b_mod)
    mod_p = mod_p.reshape(DEPTH, nbp, 1, 6 * D_MODEL)
    abr, abi, bbr, bbi = _ssm_discretise(ssm_a_re, ssm_a_im, ssm_log_dt, ssm_b_re, ssm_b_im)

    w = dict(
        g_pre_mix=g_pre_mix[:, None], g_post_mix=g_post_mix[:, None],
        g_pre_ffn=g_pre_ffn[:, None], g_post_ffn=g_post_ffn[:, None],
        w_in=w_in.astype(BF16),
        conv_w=jnp.broadcast_to(conv_w[:, :, None, :], (DEPTH, CONV_WIDTH, SUBLANES, D_CONV)),
        conv_b=conv_b[:, None], conv_ln_g=conv_ln_g[:, None], conv_ln_b=conv_ln_b[:, None],
        ssm_d=ssm_d.reshape(DEPTH, 1, D_SSM),
        w_conv_out=w_conv_out.astype(BF16), w_ssm_glu=w_ssm_glu.astype(BF16),
        w_att=w_att.astype(BF16), w_out=w_out.astype(BF16),
        w_ffn_in=w_ffn_in.astype(BF16), w_ffn_out=w_ffn_out.astype(BF16))
    hist_s = jnp.transpose(state_conv, (0, 2, 1, 3))
    h0r_s = state_ssm_re.reshape(DEPTH, nbs, N_STATE)
    h0i_s = state_ssm_im.reshape(DEPTH, nbs, N_STATE)
    prev_p = jnp.zeros((nbp, HALO, D_CONV), F32)
    h0_p = jnp.zeros((nbp, N_STATE), F32)
    kts = [_cache_t(c) for c in caches_k]
    vts = [_cache_t(c) for c in caches_v]
    att_consts = _att_sample_consts(t_s)
    n_host_steps = rows_p // tm
    seqs_per_host = n_host_steps * ATT_S_BB
    assert 2 * seqs_per_host == nbs

    xp = x_prompt.reshape(rows_p, D_MODEL)
    xs = jnp.transpose(x_sample, (1, 0, 2)).reshape(rows_s, D_MODEL)
    prompt_states, sample_states = [], []
    kvt_p = None
    kvt_s = None
    hist_out = None
    for l in range(DEPTH):
        a8, bw, cw = _ssm_pack(abr[l], abi[l], bbr[l], bbi[l], ssm_c_re[l], ssm_c_im[l])

        def spec_p(c, l=l):
            return pl.BlockSpec((None, None, 1, D_MODEL), lambda i: (l, i // per_seq, 0, c))

        def spec_s(c, l=l):
            return pl.BlockSpec((None, tm, D_MODEL), lambda i: (l, i, c))

        z_s, kvt_s = _in_proj_sample(xs, mod_s, spec_s, w["g_pre_mix"], w["w_in"], nbs, t_s, l, kvt_s)
        cy_s, hist_out = _conv_sample(z_s.reshape(t_s, nbs, N_IN), hist_s, w["conv_w"], w["conv_b"],
                                      w["conv_ln_g"], w["conv_ln_b"], l, hist_out)
        zs_s, sre_s, sim_s = _ssm_branch(z_s, h0r_s, h0i_s, l, a8, bw, cw, w["ssm_d"], l,
                                         nb=nbs, t_len=t_s, tc=t_s, flat=True)
        qkv_s = z_s[:, OFF_Q:OFF_Q + 3 * D_ATT].astype(F32).reshape(t_s, nbs, 3 * D_ATT)
        qkv_s = jnp.pad(jnp.transpose(qkv_s, (1, 0, 2)), ((0, 0), (0, QROWS - t_s), (0, 0)))
        qkv_s = qkv_s.reshape(nbs * QROWS, 3 * D_ATT)

        z_p, kvt_p = _in_proj_prompt(xp, mod_p, spec_p, w["g_pre_mix"], w["w_in"], tm,
                                     nbp, t_p, l, kvt_p)
        z3 = z_p.reshape(nbp, t_p, N_IN)
        cy_p, conv_p = _conv_branch(z3, prev_p, None, w["conv_w"], w["conv_b"], w["conv_ln_g"],
                                    w["conv_ln_b"], l, nbp, t_p, bb=1, tq=256, flat=False)
        zs_p, sre_p, sim_p = _ssm_branch(z3, h0_p, h0_p, None, a8, bw, cw, w["ssm_d"], l,
                                         nb=nbp, t_len=t_p, tc=64, flat=False)
        att_p = _att_prompt(z3)
        host_a = _AttSampleHost(qkv_s, kts, vts, att_consts, l, 0, n_host_steps)
        host_b = _AttSampleHost(qkv_s, kts, vts, att_consts, l, seqs_per_host, n_host_steps)
        x1_p, hf_p, att_a = _merge(cy_p.reshape(rows_p, D_CONV), zs_p.reshape(rows_p, D_SSM),
                                   att_p.reshape(rows_p, D_ATT_OUT), z_p, xp, mod_p, spec_p, w, l, tm,
                                   host=host_a)
        xp, att_b = _ffn(hf_p, x1_p, mod_p, spec_p, w, l, tm, host=host_b)

        att_s = jnp.concatenate([att_a, att_b], axis=0).reshape(nbs, QROWS, D_ATT_OUT)[:, :t_s]
        att_s = jnp.transpose(att_s, (1, 0, 2)).reshape(rows_s, D_ATT_OUT)
        x1_s, hf_s = _merge(cy_s.reshape(rows_s, D_CONV), zs_s, att_s, z_s, xs, mod_s, spec_s, w, l, tm)
        xs, = _ffn(hf_s, x1_s, mod_s, spec_s, w, l, tm)

        prompt_states.append((conv_p, sre_p.reshape(nbp, N_SSM_GROUPS, SSM_STATE),
                              sim_p.reshape(nbp, N_SSM_GROUPS, SSM_STATE)))
        sample_states.append((sre_s.reshape(nbs, N_SSM_GROUPS, SSM_STATE),
                              sim_s.reshape(nbs, N_SSM_GROUPS, SSM_STATE)))

    ps = [jnp.transpose(a.reshape(DEPTH, nbp, HEADS_PER_GROUP, HEAD_DIM, a.shape[-1]), (0, 1, 4, 2, 3))
          for a in kvt_p]
    ps += [jnp.stack([s[i] for s in prompt_states]) for i in range(3)]
    ss = [jnp.transpose(a.reshape(DEPTH, t_s, HEADS_PER_GROUP, HEAD_DIM, nbs), (0, 4, 1, 2, 3))
          for a in kvt_s]
    ss.append(jnp.transpose(hist_out, (0, 2, 1, 3)))
    ss += [jnp.stack([s[i] for s in sample_states]) for i in range(2)]
    xs = jnp.transpose(xs.reshape(t_s, nbs, D_MODEL), (1, 0, 2))
    return (xp.reshape(nbp, t_p, D_MODEL), xs, *ps, *ss)
```

```python
import functools

import numpy as np
import jax
import jax.numpy as jnp
from jax import lax
from jax.experimental import pallas as pl
from jax.experimental.pallas import tpu as pltpu

F32 = jnp.float32
BF16 = jnp.bfloat16

D_MODEL = 1024
DEPTH = 2
D_CONV = 512
CONV_WIDTH = 31
D_SSM = 512
SSM_GROUP = 16
N_SSM_GROUPS = 32
SSM_STATE = 64
HEAD_DIM = 64
HEADS_PER_GROUP = 4
ATT_PATTERNS = ((128, 1), (512, 4), (2048, 16))
N_ATT_HEADS = 12
D_ATT = 768
D_ATT_OUT = 256
ATT_SCALE = HEAD_DIM ** -0.5
D_FF = 2816
N_IN = 6912
RMS_EPS = 1e-6
LN_EPS = 1e-5
NEG = -1e30
BAND = 128

OFF_CONV = 0
OFF_SSM = 1024
OFF_Q = 1536
OFF_K = 2304
OFF_V = 3072
OFF_GATES = 3840
GATE_BLOCK = 768
IN_CHUNK = 768

LANES = 128
SUBLANES = 8
HALO = 32
N_STATE = N_SSM_GROUPS * SSM_STATE
OCTETS = 4
OCT_IN = D_SSM // OCTETS
OCT_ST = N_STATE // OCTETS
VMEM_LIMIT = 56 * 1024 * 1024


def _params(sem):
    return pltpu.CompilerParams(dimension_semantics=sem, vmem_limit_bytes=VMEM_LIMIT)


def _resident(shape, layer=None):
    if layer is None:
        return pl.BlockSpec(shape, lambda *_: (0,) * len(shape), pipeline_mode=pl.Buffered(1))
    return pl.BlockSpec((None,) + tuple(shape), lambda *_: (layer,) + (0,) * len(shape),
                        pipeline_mode=pl.Buffered(1))


def _slopes():
    return [2.0 ** (-8.0 * (i + 1) / N_ATT_HEADS) for i in range(N_ATT_HEADS)]


def _rms(x, g):
    return x * lax.rsqrt(jnp.mean(x * x, axis=-1, keepdims=True) + RMS_EPS) * g


def _sigmoid(x):
    return 0.5 * jnp.tanh(0.5 * x) + 0.5


def _silu(x):
    return x * _sigmoid(x)


def _dot(a, b):
    return jnp.dot(a, b, preferred_element_type=F32)


def _dot_nt(a, b):
    return lax.dot_general(a, b, (((1,), (1,)), ((), ())), preferred_element_type=F32)


def _mod_kernel(cp_ref, cs_ref, w_ref, b_ref, op_ref, os_ref):
    w = w_ref[...].astype(BF16)
    op_ref[...] = _dot(_silu(cp_ref[...]).astype(BF16), w) + b_ref[...]
    os_ref[...] = _dot(_silu(cs_ref[...]).astype(BF16), w) + b_ref[...]


def _modulation(c_p, c_s_rows, w_mod, b_mod):
    np_, ns = c_p.shape[0], c_s_rows.shape[0]
    tn = 2048
    return pl.pallas_call(
        _mod_kernel,
        grid=(DEPTH, 6 * D_MODEL // tn),
        in_specs=[pl.BlockSpec((np_, D_MODEL), lambda l, j: (0, 0)),
                  pl.BlockSpec((ns, D_MODEL), lambda l, j: (0, 0)),
                  pl.BlockSpec((None, D_MODEL, tn), lambda l, j: (l, 0, j)),
                  pl.BlockSpec((None, 1, tn), lambda l, j: (l, 0, j))],
        out_specs=[pl.BlockSpec((None, np_, tn), lambda l, j: (l, 0, j)),
                   pl.BlockSpec((None, ns, tn), lambda l, j: (l, 0, j))],
        out_shape=[jax.ShapeDtypeStruct((DEPTH, np_, 6 * D_MODEL), F32),
                   jax.ShapeDtypeStruct((DEPTH, ns, 6 * D_MODEL), F32)],
        compiler_params=_params(("parallel", "parallel")),
        name="modulation",
    )(c_p, c_s_rows, w_mod, b_mod.reshape(DEPTH, 1, 6 * D_MODEL))


def _ssm_disc_kernel(ar_ref, ai_ref, ldt_ref, br_ref, bi_ref, abr_ref, abi_ref, bbr_ref, bbi_ref):
    ar = ar_ref[...]
    ai = ai_ref[...]
    dt = jnp.exp(ldt_ref[...])
    mag = jnp.exp(dt * ar)
    abr = mag * jnp.cos(dt * ai)
    abi = mag * jnp.sin(dt * ai)
    den = ar * ar + ai * ai
    fr = ((abr - 1.0) * ar + abi * ai) / den
    fi = (abi * ar - (abr - 1.0) * ai) / den
    br = br_ref[...]
    bi = bi_ref[...]
    abr_ref[...] = abr
    abi_ref[...] = abi
    bbr_ref[...] = fr * br - fi * bi
    bbi_ref[...] = fr * bi + fi * br


def _ssm_discretise(a_re, a_im, log_dt, b_re, b_im):
    ar = a_re.reshape(DEPTH, 1, N_STATE)
    ai = a_im.reshape(DEPTH, 1, N_STATE)
    ldt = jnp.repeat(log_dt, SSM_STATE, axis=-1).reshape(DEPTH, 1, N_STATE)
    br = jnp.transpose(b_re, (0, 3, 1, 2)).reshape(DEPTH, SSM_GROUP, N_STATE)
    bi = jnp.transpose(b_im, (0, 3, 1, 2)).reshape(DEPTH, SSM_GROUP, N_STATE)
    row = pl.BlockSpec((None, 1, N_STATE), lambda l: (l, 0, 0))
    mat = pl.BlockSpec((None, SSM_GROUP, N_STATE), lambda l: (l, 0, 0))
    return pl.pallas_call(
        _ssm_disc_kernel,
        grid=(DEPTH,),
        in_specs=[row, row, row, mat, mat],
        out_specs=[row, row, mat, mat],
        out_shape=[jax.ShapeDtypeStruct((DEPTH, 1, N_STATE), F32)] * 2
        + [jax.ShapeDtypeStruct((DEPTH, SSM_GROUP, N_STATE), F32)] * 2,
        compiler_params=_params(("parallel",)),
        name="ssm_discretise",
    )(ar, ai, ldt, br, bi)


def _ssm_pack(abr, abi, bbr, bbi, c_re, c_im):
    gpo = N_SSM_GROUPS // OCTETS
    eye = jnp.eye(gpo, dtype=F32)

    def oct_state(v):
        return v.reshape(OCTETS, OCT_ST)

    a8 = jnp.concatenate([oct_state(abr), oct_state(abi)], axis=1).reshape(1, 2 * N_STATE)
    a8 = jnp.broadcast_to(a8, (SUBLANES, 2 * N_STATE))

    def b_tiles(bb):
        t = bb.reshape(SSM_GROUP, OCTETS, gpo, SSM_STATE)
        t = jnp.transpose(t, (1, 2, 0, 3))
        t = t[:, :, :, None, :] * eye[None, :, None, :, None]
        return t.reshape(OCTETS, OCT_IN, OCT_ST)

    bw = jnp.concatenate([b_tiles(bbr), b_tiles(bbi)], axis=2).astype(BF16)

    def c_tiles(cc):
        t = cc.reshape(OCTETS, gpo, SSM_GROUP, SSM_STATE)
        t = jnp.transpose(t, (0, 1, 3, 2))
        t = t[:, :, :, None, :] * eye[None, :, None, :, None]
        return t.reshape(OCTETS, OCT_ST, OCT_IN)

    cw = jnp.concatenate([c_tiles(c_re), -c_tiles(c_im)], axis=1).astype(BF16)
    return a8, bw, cw


def _is_kv_chunk(lo):
    return OFF_K <= lo < OFF_K + 2 * D_ATT


def _in_proj_sample_kernel(*refs, nb, t_len, layer, n_alias):
    x_ref, sh_ref, sc_ref, g_ref, w_ref = refs[:5]
    z_ref = refs[5 + n_alias]
    kvt_refs = refs[6 + n_alias:]
    y = _rms(x_ref[...], g_ref[...])
    h = (y * (1.0 + sc_ref[...]) + sh_ref[...]).astype(BF16)
    for c in range(N_IN // IN_CHUNK):
        lo = c * IN_CHUNK
        r = _dot(h, w_ref[:, lo:lo + IN_CHUNK])
        z_ref[:, lo:lo + IN_CHUNK] = r.astype(BF16)
        if _is_kv_chunk(lo):
            which = (lo - OFF_K) // D_ATT
            for g in range(len(ATT_PATTERNS)):
                for t in range(t_len):
                    tile = r[t * nb:(t + 1) * nb, g * D_ATT_OUT:(g + 1) * D_ATT_OUT].T
                    if n_alias:
                        kvt_refs[2 * g + which][t] = tile
                    else:
                        for l in range(DEPTH):
                            kvt_refs[2 * g + which][l, t] = tile if l == layer else jnp.zeros_like(tile)


def _in_proj_sample(x2, mod_arr, mod_spec, g_pre, w_bf, nb, t_len, layer, prev_kvt):
    rows = x2.shape[0]
    n_alias = 0 if prev_kvt is None else len(prev_kvt)
    kern = functools.partial(_in_proj_sample_kernel, nb=nb, t_len=t_len, layer=layer, n_alias=n_alias)
    n_kv = 2 * len(ATT_PATTERNS)
    if n_alias:
        kv_spec = pl.BlockSpec((None, t_len, D_ATT_OUT, nb), lambda i: (layer, 0, 0, 0))
    else:
        kv_spec = pl.BlockSpec((DEPTH, t_len, D_ATT_OUT, nb), lambda i: (0, 0, 0, 0))
    outs = pl.pallas_call(
        kern,
        grid=(1,),
        in_specs=[pl.BlockSpec((rows, D_MODEL), lambda i: (0, 0)),
                  mod_spec(0), mod_spec(1),
                  _resident((1, D_MODEL), layer),
                  _resident((D_MODEL, N_IN), layer)] + [pl.BlockSpec(memory_space=pl.ANY)] * n_alias,
        out_specs=[pl.BlockSpec((rows, N_IN), lambda i: (0, 0))]
        + [kv_spec] * n_kv,
        out_shape=[jax.ShapeDtypeStruct((rows, N_IN), BF16)]
        + [jax.ShapeDtypeStruct((DEPTH, t_len, D_ATT_OUT, nb), F32)] * n_kv,
        input_output_aliases={5 + k: 1 + k for k in range(n_alias)},
        compiler_params=_params(("arbitrary",)),
        name="in_proj_sample",
    )(x2, mod_arr, mod_arr, g_pre, w_bf, *(prev_kvt or ()))
    return outs[0], list(outs[1:])


def _in_proj_prompt_kernel(*refs, keeps, tm, layer, n_alias):
    x_ref, sh_ref, sc_ref, g_ref, w_ref = refs[:5]
    z_ref = refs[5 + n_alias]
    kvt_refs = refs[6 + n_alias:]
    y = _rms(x_ref[...], g_ref[...])
    h = (y * (1.0 + sc_ref[...]) + sh_ref[...]).astype(BF16)
    for c in range(N_IN // IN_CHUNK):
        lo = c * IN_CHUNK
        r = _dot(h, w_ref[:, lo:lo + IN_CHUNK])
        z_ref[:, lo:lo + IN_CHUNK] = r.astype(BF16)
        if _is_kv_chunk(lo):
            which = (lo - OFF_K) // D_ATT
            for g, keep in enumerate(keeps):
                kept = min(keep, tm)
                tile = r[tm - kept:, g * D_ATT_OUT:(g + 1) * D_ATT_OUT].T
                if n_alias:
                    kvt_refs[2 * g + which][...] = tile
                else:
                    for l in range(DEPTH):
                        kvt_refs[2 * g + which][l] = tile if l == layer else jnp.zeros_like(tile)


def _in_proj_prompt(x2, mod_arr, mod_spec, g_pre, w_bf, tm, nbatch, t_len, layer, prev_kvt):
    rows = x2.shape[0]
    per_seq = t_len // tm
    keeps = tuple(min(window, t_len) for window, _ in ATT_PATTERNS)
    n_alias = 0 if prev_kvt is None else len(prev_kvt)
    kern = functools.partial(_in_proj_prompt_kernel, keeps=keeps, tm=tm, layer=layer, n_alias=n_alias)
    kvt_specs, kvt_shapes = [], []
    for keep in keeps:
        first = per_seq - max(keep // tm, 1)
        kept = min(keep, tm)
        for _ in range(2):
            kvt_specs.append(pl.BlockSpec(
                (None if n_alias else DEPTH, None, D_ATT_OUT, kept),
                lambda i, first=first: (layer if n_alias else 0, i // per_seq, 0,
                                        jnp.maximum(i % per_seq - first, 0))))
            kvt_shapes.append(jax.ShapeDtypeStruct((DEPTH, nbatch, D_ATT_OUT, keep), F32))
    outs = pl.pallas_call(
        kern,
        grid=(rows // tm,),
        in_specs=[pl.BlockSpec((tm, D_MODEL), lambda i: (i, 0)),
                  mod_spec(0), mod_spec(1),
                  _resident((1, D_MODEL), layer),
                  _resident((D_MODEL, N_IN), layer)] + [pl.BlockSpec(memory_space=pl.ANY)] * n_alias,
        out_specs=[pl.BlockSpec((tm, N_IN), lambda i: (i, 0))] + kvt_specs,
        out_shape=[jax.ShapeDtypeStruct((rows, N_IN), BF16)] + kvt_shapes,
        input_output_aliases={5 + k: 1 + k for k in range(n_alias)},
        compiler_params=_params(("arbitrary",)),
        name="in_proj_prompt",
    )(x2, mod_arr, mod_arr, g_pre, w_bf, *(prev_kvt or ()))
    return outs[0], list(outs[1:])


def _conv_kernel(a_ref, prev_ref, w_ref, b_ref, lg_ref, lb_ref, y_ref, st_ref, ubuf, sh_scr,
                 *, bb, tq, nt, sub):
    i = pl.program_id(1)
    base = HALO - (CONV_WIDTH - 1)
    span = tq + HALO - SUBLANES
    nrt = max(sub // SUBLANES, 1)
    rpt = min(sub, SUBLANES)
    for b in range(bb):
        @pl.when(i == 0)
        def _():
            ubuf[b, 0:HALO, :] = prev_ref[b]

        @pl.when(i > 0)
        def _():
            ubuf[b, 0:HALO, :] = ubuf[b, tq:tq + HALO, :]

        a = a_ref[b].astype(F32)
        ubuf[b, HALO:HALO + tq, :] = a[:, :D_CONV] * _sigmoid(a[:, D_CONV:])
        for r in range(1, SUBLANES):
            sh_scr[r - 1, 0:span, :] = ubuf[b, r:r + span, :]
        for r0 in range(0, tq, sub):
            acc = jnp.zeros((nrt, rpt, D_CONV), F32) + b_ref[...]
            for j in range(CONV_WIDTH):
                off = base + j
                al = r0 + (off // SUBLANES) * SUBLANES
                if off % SUBLANES == 0:
                    src = ubuf[b, al:al + sub, :]
                else:
                    src = sh_scr[off % SUBLANES - 1, al:al + sub, :]
                acc = acc + w_ref[j, 0:rpt, :][None] * src.reshape(nrt, rpt, D_CONV)
            acc = acc.reshape(sub, D_CONV)
            mu = jnp.mean(acc, axis=-1, keepdims=True)
            xc = acc - mu
            yn = xc * lax.rsqrt(jnp.mean(xc * xc, axis=-1, keepdims=True) + LN_EPS)
            y_ref[b, r0:r0 + sub, :] = _silu(yn * lg_ref[...] + lb_ref[...]).astype(y_ref.dtype)

        @pl.when(i == nt - 1)
        def _():
            st_ref[b] = ubuf[b, base + tq:base + tq + CONV_WIDTH - 1, :]


def _conv_branch(z3, prev_pad, w_b8, b, lg, lb, layer, bb, tq):
    nbatch, t_len, _ = z3.shape
    nt = t_len // tq
    sub = min(tq, 32)
    kern = functools.partial(_conv_kernel, bb=bb, tq=tq, nt=nt, sub=sub)
    col = OFF_CONV // (2 * D_CONV)
    return pl.pallas_call(
        kern,
        grid=(nbatch // bb, nt),
        in_specs=[pl.BlockSpec((bb, tq, 2 * D_CONV), lambda i, j: (i, j, col)),
                  pl.BlockSpec((bb, HALO, D_CONV), lambda i, j: (i, 0, 0)),
                  _resident((CONV_WIDTH, SUBLANES, D_CONV), layer),
                  _resident((1, D_CONV), layer), _resident((1, D_CONV), layer),
                  _resident((1, D_CONV), layer)],
        out_specs=[pl.BlockSpec((bb, tq, D_CONV), lambda i, j: (i, j, 0)),
                   pl.BlockSpec((bb, CONV_WIDTH - 1, D_CONV), lambda i, j: (i, 0, 0))],
        out_shape=[jax.ShapeDtypeStruct((nbatch, t_len, D_CONV), BF16),
                   jax.ShapeDtypeStruct((nbatch, CONV_WIDTH - 1, D_CONV), F32)],
        scratch_shapes=[pltpu.VMEM((bb, HALO + tq + SUBLANES, D_CONV), F32),
                        pltpu.VMEM((SUBLANES - 1, HALO + tq, D_CONV), F32)],
        compiler_params=_params(("parallel", "arbitrary")),
        name="conv_branch",
    )(z3, prev_pad, w_b8, b, lg, lb)


def _conv_sample_kernel(*refs, t_len, nbb, layer, n_alias):
    a_ref, prev_ref, w_ref, b_ref, lg_ref, lb_ref = refs[:6]
    y_ref, st_ref = refs[6 + n_alias:]
    hist = CONV_WIDTH - 1
    if not n_alias:
        for l in range(DEPTH):
            if l != layer:
                st_ref[l] = jnp.zeros((hist, nbb, D_CONV), F32)
        st_ref = st_ref.at[layer]
    for k in range(hist - t_len):
        st_ref[k] = prev_ref[k + t_len]
    for t in range(t_len):
        a = a_ref[t].astype(F32)
        st_ref[hist - t_len + t] = a[:, :D_CONV] * _sigmoid(a[:, D_CONV:])
    nrt = nbb // SUBLANES
    for t in range(t_len):
        acc = jnp.zeros((nrt, SUBLANES, D_CONV), F32) + b_ref[...]
        for j in range(CONV_WIDTH):
            k = t + j
            src = prev_ref[k] if k < hist else st_ref[k - t_len]
            acc = acc + w_ref[j][None] * src.reshape(nrt, SUBLANES, D_CONV)
        acc = acc.reshape(nbb, D_CONV)
        mu = jnp.mean(acc, axis=-1, keepdims=True)
        xc = acc - mu
        yn = xc * lax.rsqrt(jnp.mean(xc * xc, axis=-1, keepdims=True) + LN_EPS)
        y_ref[t] = _silu(yn * lg_ref[...] + lb_ref[...]).astype(y_ref.dtype)


def _conv_sample(z_t, state_t, w_b8, b, lg, lb, layer, prev_out):
    t_len, nb, _ = z_t.shape
    nbb = 32
    hist = CONV_WIDTH - 1
    n_alias = 0 if prev_out is None else 1
    kern = functools.partial(_conv_sample_kernel, t_len=t_len, nbb=nbb, layer=layer, n_alias=n_alias)
    st_spec = pl.BlockSpec((None, hist, nbb, D_CONV), lambda i: (layer, 0, i, 0))
    if n_alias:
        st_out_spec = st_spec
    else:
        st_out_spec = pl.BlockSpec((DEPTH, hist, nbb, D_CONV), lambda i: (0, 0, i, 0))
    return pl.pallas_call(
        kern,
        grid=(nb // nbb,),
        in_specs=[pl.BlockSpec((t_len, nbb, 2 * D_CONV), lambda i: (0, i, OFF_CONV // (2 * D_CONV))),
                  st_spec,
                  _resident((CONV_WIDTH, SUBLANES, D_CONV), layer),
                  _resident((1, D_CONV), layer), _resident((1, D_CONV), layer),
                  _resident((1, D_CONV), layer)] + [pl.BlockSpec(memory_space=pl.ANY)] * n_alias,
        out_specs=[pl.BlockSpec((t_len, nbb, D_CONV), lambda i: (0, i, 0)), st_out_spec],
        out_shape=[jax.ShapeDtypeStruct((t_len, nb, D_CONV), BF16),
                   jax.ShapeDtypeStruct((DEPTH, hist, nb, D_CONV), F32)],
        input_output_aliases={6: 1} if n_alias else {},
        compiler_params=_params(("parallel",)),
        name="conv_sample",
    )(z_t, state_t, w_b8, b, lg, lb, *(() if prev_out is None else (prev_out,)))


def _ssm_kernel(u_ref, h0r_ref, h0i_ref, a_ref, bw_ref, cw_ref, d_ref, y_ref, hr_ref, hi_ref,
                ub_scr, u_scr, s_scr, y_scr, h_scr, *, nb, tc, nt, flat):
    i = pl.program_id(0)
    st2 = 2 * OCT_ST

    @pl.when(i == 0)
    def _():
        for o in range(OCTETS):
            h_scr[:, st2 * o:st2 * o + OCT_ST] = h0r_ref[:, OCT_ST * o:OCT_ST * (o + 1)]
            h_scr[:, st2 * o + OCT_ST:st2 * (o + 1)] = h0i_ref[:, OCT_ST * o:OCT_ST * (o + 1)]

    if flat:
        u_scr[...] = u_ref[...].astype(F32)
    else:
        for o in range(OCTETS):
            cols = slice(OCT_IN * o, OCT_IN * (o + 1))
            for b in range(nb):
                ub_scr[o, b * tc:(b + 1) * tc, :] = u_ref[b, :, cols].astype(F32)
            for t in range(tc):
                u_scr[t * nb:(t + 1) * nb, cols] = ub_scr[o, pl.ds(t, nb, stride=tc), :]

    for o in range(OCTETS):
        ub = u_scr[:, OCT_IN * o:OCT_IN * (o + 1)].astype(BF16)
        s_scr[:, st2 * o:st2 * (o + 1)] = _dot(ub, bw_ref[o])

    for o in range(OCTETS):
        c0 = st2 * o
        re = slice(c0, c0 + OCT_ST)
        im = slice(c0 + OCT_ST, c0 + st2)
        ar = a_ref[:, re]
        ai = a_ref[:, im]

        def run(r0, hr, hi, re=re, im=im, ar=ar, ai=ai):
            for t in range(tc):
                row = r0 + t * nb
                if isinstance(row, int):
                    rows = slice(row, row + SUBLANES)
                else:
                    rows = pl.ds(pl.multiple_of(row, SUBLANES), SUBLANES)
                nr = ar * hr - ai * hi + s_scr[rows, re]
                ni = ar * hi + ai * hr + s_scr[rows, im]
                s_scr[rows, re] = nr
                s_scr[rows, im] = ni
                hr, hi = nr, ni
            return hr, hi

        if nb == SUBLANES:
            hr, hi = run(0, h_scr[:, re], h_scr[:, im])
            h_scr[:, re] = hr
            h_scr[:, im] = hi
        else:
            def rows_body(rg, carry, re=re, im=im, run=run):
                r0 = pl.multiple_of(rg * SUBLANES, SUBLANES)
                hr, hi = run(r0, h_scr[pl.ds(r0, SUBLANES), re], h_scr[pl.ds(r0, SUBLANES), im])
                h_scr[pl.ds(r0, SUBLANES), re] = hr
                h_scr[pl.ds(r0, SUBLANES), im] = hi
                return carry

            lax.fori_loop(0, nb // SUBLANES, rows_body, 0)

    for o in range(OCTETS):
        hb = s_scr[:, st2 * o:st2 * (o + 1)].astype(BF16)
        y_scr[:, OCT_IN * o:OCT_IN * (o + 1)] = _dot(hb, cw_ref[o])

    yv = jax.nn.gelu(y_scr[...] + d_ref[...] * u_scr[...])
    if flat:
        y_ref[...] = yv.astype(y_ref.dtype)
    else:
        y_scr[...] = yv
        for o in range(OCTETS):
            cols = slice(OCT_IN * o, OCT_IN * (o + 1))
            for t in range(tc):
                ub_scr[o, pl.ds(t, nb, stride=tc), :] = y_scr[t * nb:(t + 1) * nb, cols]
            for b in range(nb):
                y_ref[b, :, cols] = ub_scr[o, b * tc:(b + 1) * tc, :].astype(y_ref.dtype)

    @pl.when(i == nt - 1)
    def _():
        for o in range(OCTETS):
            hr_ref[:, OCT_ST * o:OCT_ST * (o + 1)] = h_scr[:, st2 * o:st2 * o + OCT_ST]
            hi_ref[:, OCT_ST * o:OCT_ST * (o + 1)] = h_scr[:, st2 * o + OCT_ST:st2 * (o + 1)]


def _ssm_branch(z, h0r, h0i, h0_layer, a8, bw, cw, dvec, layer, nb, t_len, tc, flat):
    nt = t_len // tc
    m = nb * tc
    kern = functools.partial(_ssm_kernel, nb=nb, tc=tc, nt=nt, flat=flat)
    if flat:
        u_spec = pl.BlockSpec((m, D_SSM), lambda i: (0, OFF_SSM // D_SSM))
        y_spec = pl.BlockSpec((m, D_SSM), lambda i: (0, 0))
        y_shape = jax.ShapeDtypeStruct((m, D_SSM), BF16)
    else:
        u_spec = pl.BlockSpec((nb, tc, D_SSM), lambda i: (0, i, OFF_SSM // D_SSM))
        y_spec = pl.BlockSpec((nb, tc, D_SSM), lambda i: (0, i, 0))
        y_shape = jax.ShapeDtypeStruct((nb, t_len, D_SSM), BF16)
    st_spec = pl.BlockSpec((nb, N_STATE), lambda i: (0, 0))
    if h0_layer is None:
        h0_spec = st_spec
    else:
        h0_spec = pl.BlockSpec((None, nb, N_STATE), lambda i: (h0_layer, 0, 0))
    return pl.pallas_call(
        kern,
        grid=(nt,),
        in_specs=[u_spec, h0_spec, h0_spec,
                  _resident((SUBLANES, 2 * N_STATE)),
                  _resident((OCTETS, OCT_IN, 2 * OCT_ST)),
                  _resident((OCTETS, 2 * OCT_ST, OCT_IN)),
                  _resident((1, D_SSM), layer)],
        out_specs=[y_spec, st_spec, st_spec],
        out_shape=[y_shape, jax.ShapeDtypeStruct((nb, N_STATE), F32),
                   jax.ShapeDtypeStruct((nb, N_STATE), F32)],
        scratch_shapes=[pltpu.VMEM((OCTETS, m, OCT_IN), F32), pltpu.VMEM((m, D_SSM), F32),
                        pltpu.VMEM((m, 2 * N_STATE), F32), pltpu.VMEM((m, D_SSM), F32),
                        pltpu.VMEM((nb, 2 * N_STATE), F32)],
        compiler_params=_params(("arbitrary",)),
        name="ssm_branch",
    )(z, h0r, h0i, a8, bw, cw, dvec)


def _att_prompt_kernel(q0_ref, q1_ref, k0_ref, k1_ref, v0_ref, v1_ref, o_ref,
                       qkv_scr, o_scr, l_scr, bp_scr, *, t_len):
    g_id = pl.program_id(1)
    slopes = _slopes()
    lane = lax.broadcasted_iota(jnp.int32, (BAND, LANES), 1)
    low = lane < HEAD_DIM

    for n, r in enumerate((q0_ref, q1_ref, k0_ref, k1_ref, v0_ref, v1_ref)):
        qkv_scr[n] = r[...].astype(F32)

    def group(g, dil):
        nblk = t_len // dil // BAND
        qi = lax.broadcasted_iota(jnp.int32, (BAND, 2 * BAND), 0)
        kj = lax.broadcasted_iota(jnp.int32, (BAND, 2 * BAND), 1)
        dist = qi + BAND - kj
        valid = (dist >= 0) & (dist <= BAND)
        valid0 = valid & (kj >= BAND)
        distf = dist.astype(F32)
        for h in range(HEADS_PER_GROUP):
            bias = (-slopes[g * HEADS_PER_GROUP + h] * dil) * distf
            bp_scr[0, h] = jnp.where(valid0, bias, NEG)
            bp_scr[1, h] = jnp.where(valid, bias, NEG)

        def rows(start):
            if dil == 1:
                return pl.ds(start, BAND)
            return pl.ds(start, BAND, stride=dil)

        def block(blk, carry):
            r = blk // nblk
            n = blk % nblk
            start_c = r + dil * BAND * n
            start_p = r + dil * BAND * jnp.maximum(n - 1, 0)
            sel = jnp.minimum(n, 1)
            for pair in range(2):
                qp = qkv_scr[pair, rows(start_c), :] * ATT_SCALE
                kp = jnp.concatenate([qkv_scr[2 + pair, rows(start_p), :],
                                      qkv_scr[2 + pair, rows(start_c), :]], axis=0).astype(BF16)
                vp = jnp.concatenate([qkv_scr[4 + pair, rows(start_p), :],
                                      qkv_scr[4 + pair, rows(start_c), :]], axis=0).astype(BF16)
                o_h, l_h = [], []
                for hh in range(2):
                    h = 2 * pair + hh
                    qm = jnp.where(low if hh == 0 else jnp.logical_not(low), qp, 0.0).astype(BF16)
                    t = _dot_nt(qm, kp) + bp_scr[sel, h]
                    mx = jnp.max(t, axis=-1, keepdims=True)
                    p = jnp.exp(t - mx)
                    den = jnp.sum(p, axis=-1, keepdims=True)
                    pv = _dot(p.astype(BF16), vp)
                    o_h.append(pv / den)
                    l_h.append(jnp.broadcast_to(mx + jnp.log(den), (BAND, LANES)))
                o_scr[2 * g + pair, rows(start_c), :] = jnp.where(low, o_h[0], o_h[1])
                l_scr[2 * g + pair, rows(start_c), :] = jnp.where(low, l_h[0], l_h[1])
            return carry

        lax.fori_loop(0, dil * nblk, block, 0, unroll=4)

    for g, (window, dil) in enumerate(ATT_PATTERNS):
        @pl.when(g_id == g)
        def _(g=g, dil=dil):
            group(g, dil)

    @pl.when(g_id == len(ATT_PATTERNS) - 1)
    def _():
        tr = 256

        def comb(i, carry):
            r0 = pl.multiple_of(i * tr, tr)
            for pair in range(2):
                l0 = l_scr[pair, pl.ds(r0, tr), :]
                l1 = l_scr[2 + pair, pl.ds(r0, tr), :]
                l2 = l_scr[4 + pair, pl.ds(r0, tr), :]
                mx = jnp.maximum(jnp.maximum(l0, l1), l2)
                w0 = jnp.exp(l0 - mx)
                w1 = jnp.exp(l1 - mx)
                w2 = jnp.exp(l2 - mx)
                num = (w0 * o_scr[pair, pl.ds(r0, tr), :] + w1 * o_scr[2 + pair, pl.ds(r0, tr), :]
                       + w2 * o_scr[4 + pair, pl.ds(r0, tr), :])
                o_ref[pl.ds(r0, tr), LANES * pair:LANES * (pair + 1)] = (
                    num / (w0 + w1 + w2)).astype(o_ref.dtype)
            return carry

        lax.fori_loop(0, t_len // tr, comb, 0)


def _att_prompt(z3):
    nbatch, t_len, _ = z3.shape
    ng = len(ATT_PATTERNS)
    kern = functools.partial(_att_prompt_kernel, t_len=t_len)

    def spec(off, pair):
        return pl.BlockSpec((None, t_len, LANES), lambda b, g: (b, 0, off // LANES + 2 * g + pair))

    return pl.pallas_call(
        kern,
        grid=(nbatch, ng),
        in_specs=[spec(off, pair) for off in (OFF_Q, OFF_K, OFF_V) for pair in range(2)],
        out_specs=pl.BlockSpec((None, t_len, D_ATT_OUT), lambda b, g: (b, 0, 0)),
        out_shape=jax.ShapeDtypeStruct((nbatch, t_len, D_ATT_OUT), BF16),
        scratch_shapes=[pltpu.VMEM((6, t_len, LANES), F32),
                        pltpu.VMEM((2 * ng, t_len, LANES), F32),
                        pltpu.VMEM((2 * ng, t_len, LANES), F32),
                        pltpu.VMEM((2, HEADS_PER_GROUP, BAND, 2 * BAND), F32)],
        compiler_params=_params(("parallel", "arbitrary")),
        name="att_prompt",
    )(z3, z3, z3, z3, z3, z3)


QROWS = 8
ATT_S_BB = 2


def _att_sample_stages(q_ref, kn_ref, vn_ref, k0_ref, k1_ref, k2_ref, v0_ref, v1_ref, v2_ref,
                       b0_ref, b1_ref, b2_ref, bn_ref, o_ref):
    kc_refs = (k0_ref, k1_ref, k2_ref)
    vc_refs = (v0_ref, v1_ref, v2_ref)
    bc_refs = (b0_ref, b1_ref, b2_ref)
    w = D_ATT_OUT
    ng = len(ATT_PATTERNS)
    state = {}

    def stage(b, g):
        def run():
            head = lax.broadcasted_iota(jnp.int32, (QROWS, w), 1) // HEAD_DIM
            zpad = jnp.zeros((LANES - QROWS, w), BF16)
            rows = slice(QROWS * b, QROWS * (b + 1))
            outs, lses = state.setdefault(b, ([], []))
            cols = slice(g * w, (g + 1) * w)
            qg = q_ref[rows, cols] * ATT_SCALE
            qblk = jnp.concatenate([jnp.where(head == h, qg, 0.0) for h in range(HEADS_PER_GROUP)],
                                   axis=0).astype(BF16)
            s_c = _dot(qblk, kc_refs[g][b].astype(BF16))
            bias_c = bc_refs[g][...]
            t_c = jnp.where(bias_c > 0.5 * NEG, s_c + bias_c, NEG)
            kn = jnp.concatenate([kn_ref[rows, cols].astype(BF16), zpad], axis=0)
            vn = jnp.concatenate([vn_ref[rows, cols].astype(BF16), zpad], axis=0)
            bias_n = bn_ref[g]
            t_n = jnp.where(bias_n > 0.5 * NEG, _dot_nt(qblk, kn) + bias_n, NEG)
            mx = jnp.maximum(jnp.max(t_c, axis=-1, keepdims=True), jnp.max(t_n, axis=-1, keepdims=True))
            p_c = jnp.exp(t_c - mx)
            p_n = jnp.exp(t_n - mx)
            den = jnp.sum(p_c, axis=-1, keepdims=True) + jnp.sum(p_n, axis=-1, keepdims=True)
            o = _dot_nt(p_c.astype(BF16), vc_refs[g][b].astype(BF16))
            o = (o + _dot(p_n.astype(BF16), vn)) / den
            lse = jnp.broadcast_to(mx + jnp.log(den), o.shape)
            og = jnp.zeros((QROWS, w), F32)
            lg = jnp.zeros((QROWS, w), F32)
            for h in range(HEADS_PER_GROUP):
                og = jnp.where(head == h, o[QROWS * h:QROWS * (h + 1)], og)
                lg = jnp.where(head == h, lse[QROWS * h:QROWS * (h + 1)], lg)
            outs.append(og)
            lses.append(lg)
            if g == ng - 1:
                mx = jnp.maximum(jnp.maximum(lses[0], lses[1]), lses[2])
                ws = [jnp.exp(l - mx) for l in lses]
                o_ref[rows, :] = ((ws[0] * outs[0] + ws[1] * outs[1] + ws[2] * outs[2])
                                  / (ws[0] + ws[1] + ws[2])).astype(o_ref.dtype)

        return run

    return [stage(b, g) for b in range(ATT_S_BB) for g in range(ng)]


def _att_sample_consts(t_new):
    slopes = _slopes()
    rows = HEADS_PER_GROUP * QROWS
    bcs = []
    bn = np.full((3, rows, LANES), NEG, np.float32)
    for g, (window, dil) in enumerate(ATT_PATTERNS):
        bc = np.zeros((rows, window), np.float32)
        pos = np.arange(window)
        for h in range(HEADS_PER_GROUP):
            s = slopes[g * HEADS_PER_GROUP + h]
            for j in range(QROWS):
                r = QROWS * h + j
                if j >= t_new:
                    bn[g, r, 0] = 0.0
                    continue
                dist = window + j - pos
                ok = (dist % dil == 0) & (dist <= window)
                bc[r] = np.where(ok, -s * dist, NEG)
                for i_ in range(j + 1):
                    if (j - i_) % dil == 0:
                        bn[g, r, i_] = -s * (j - i_)
        bcs.append(jnp.asarray(bc))
    return bcs, jnp.asarray(bn)


def _cache_t(c):
    d, nb_, length, nh, hd = c.shape
    return jnp.transpose(c, (0, 1, 3, 4, 2)).reshape(d, nb_, nh * hd, length)


class _AttSampleHost:
    def __init__(self, qkv, kts, vts, consts, layer, seq0, nsteps):
        bb = ATT_S_BB
        w = D_ATT_OUT
        blk0 = seq0 // bb
        bcs, bn = consts
        rows = HEADS_PER_GROUP * QROWS

        def new_spec(part):
            return pl.BlockSpec((bb * QROWS, D_ATT), lambda i: (blk0 + i, part))

        def cache_spec(length):
            return pl.BlockSpec((None, bb, w, length), lambda i: (layer, blk0 + i, 0, 0))

        self.inputs = [qkv, qkv, qkv] + list(kts) + list(vts) + list(bcs) + [bn]
        self.in_specs = ([new_spec(0), new_spec(1), new_spec(2)]
                         + [cache_spec(win) for win, _ in ATT_PATTERNS] * 2
                         + [_resident((rows, win)) for win, _ in ATT_PATTERNS]
                         + [_resident((3, rows, LANES))])
        self.out_spec = pl.BlockSpec((bb * QROWS, w), lambda i: (i, 0))
        self.out_shape = jax.ShapeDtypeStruct((nsteps * bb * QROWS, w), BF16)


N_ATT_HOST_INPUTS = 3 + 3 * len(ATT_PATTERNS) + 1


MERGE_CHUNK = 256

def _merge_kernel(*refs, n_host):
    n_gate = 3 * D_MODEL // GATE_BLOCK
    cy_ref, zs_ref, at_ref = refs[:3]
    gate_refs = refs[3:3 + n_gate]
    n_in = 13 + n_gate
    (x_ref, gm_ref, shf_ref, scf_ref, wco_ref, wsg_ref, wat_ref, wo_ref, gpost_ref,
     gpre_ref) = refs[3 + n_gate:n_in]
    x1_ref, hf_ref = refs[n_in + n_host:n_in + n_host + 2]
    mg_scr = refs[-1]
    if n_host:
        for run in _att_sample_stages(*refs[n_in:n_in + n_host], refs[n_in + n_host + 2]):
            run()

    def gate(branch, lo):
        col = branch * D_MODEL + lo
        ref = gate_refs[col // GATE_BLOCK]
        return _sigmoid(ref[:, col % GATE_BLOCK:col % GATE_BLOCK + MERGE_CHUNK].astype(F32))

    cy = cy_ref[...]
    zs = zs_ref[...]
    at = at_ref[...]
    for c in range(D_MODEL // MERGE_CHUNK):
        lo = c * MERGE_CHUNK
        cols = slice(lo, lo + MERGE_CHUNK)
        a = _dot(cy, wco_ref[:, cols])
        bm = _dot(zs, wsg_ref[:, cols]) * _sigmoid(_dot(zs, wsg_ref[:, D_MODEL + lo:D_MODEL + lo + MERGE_CHUNK]))
        cc = _dot(at, wat_ref[:, cols])
        merged = gate(0, lo) * a + gate(1, lo) * bm + gate(2, lo) * cc
        mg_scr[:, cols] = merged.astype(BF16)
    mo = _dot(mg_scr[...], wo_ref[...])
    x1 = x_ref[...] + gm_ref[...] * _rms(mo, gpost_ref[...])
    x1_ref[...] = x1
    hf_ref[...] = (_rms(x1, gpre_ref[...]) * (1.0 + scf_ref[...]) + shf_ref[...]).astype(BF16)


def _merge(cy, zs, at, z2, x2, mod_arr, mod_spec, w, layer, tm, host=None):
    rows = x2.shape[0]

    def rowspec(width):
        return pl.BlockSpec((tm, width), lambda i: (i, 0))

    n_host = 0 if host is None else len(host.inputs)
    n_gate = 3 * D_MODEL // GATE_BLOCK
    gate_specs = [pl.BlockSpec((tm, GATE_BLOCK), lambda i, k=k: (i, OFF_GATES // GATE_BLOCK + k))
                  for k in range(n_gate)]
    outs = pl.pallas_call(
        functools.partial(_merge_kernel, n_host=n_host),
        grid=(rows // tm,),
        in_specs=[rowspec(D_CONV), rowspec(D_SSM), rowspec(D_ATT_OUT)] + gate_specs
        + [rowspec(D_MODEL), mod_spec(2), mod_spec(3), mod_spec(4),
                  _resident((D_CONV, D_MODEL), layer), _resident((D_SSM, 2 * D_MODEL), layer),
                  _resident((D_ATT_OUT, D_MODEL), layer), _resident((D_MODEL, D_MODEL), layer),
                  _resident((1, D_MODEL), layer), _resident((1, D_MODEL), layer)]
        + ([] if host is None else host.in_specs),
        out_specs=[rowspec(D_MODEL), rowspec(D_MODEL)] + ([] if host is None else [host.out_spec]),
        out_shape=[jax.ShapeDtypeStruct((rows, D_MODEL), F32),
                   jax.ShapeDtypeStruct((rows, D_MODEL), BF16)]
        + ([] if host is None else [host.out_shape]),
        scratch_shapes=[pltpu.VMEM((tm, D_MODEL), BF16)],
        compiler_params=_params(("parallel",)),
        name="merge",
    )(cy, zs, at, *([z2] * n_gate), x2, mod_arr, mod_arr, mod_arr, w["w_conv_out"], w["w_ssm_glu"], w["w_att"],
      w["w_out"], w["g_post_mix"], w["g_pre_ffn"], *([] if host is None else host.inputs))
    return outs


FF_CHUNK = 256


def _ffn_kernel(*refs, n_host):
    hf_ref, x1_ref, gf_ref, w1_ref, w2_ref, gpost_ref = refs[:6]
    o_ref = refs[6 + n_host]
    acc_ref = refs[-1]
    if n_host:
        for run in _att_sample_stages(*refs[6:6 + n_host], refs[7 + n_host]):
            run()
    hf = hf_ref[...]
    for c in range(D_FF // FF_CHUNK):
        lo = c * FF_CHUNK
        a = _dot(hf, w1_ref[:, lo:lo + FF_CHUNK])
        b = _dot(hf, w1_ref[:, D_FF + lo:D_FF + lo + FF_CHUNK])
        f = (_silu(a) * b).astype(BF16)
        part = _dot(f, w2_ref[lo:lo + FF_CHUNK, :])
        if c == 0:
            acc_ref[...] = part
        else:
            acc_ref[...] += part
    o_ref[...] = x1_ref[...] + gf_ref[...] * _rms(acc_ref[...], gpost_ref[...])


def _ffn(hf, x1, mod_arr, mod_spec, w, layer, tm, host=None):
    rows = x1.shape[0]
    n_host = 0 if host is None else len(host.inputs)
    return pl.pallas_call(
        functools.partial(_ffn_kernel, n_host=n_host),
        grid=(rows // tm,),
        in_specs=[pl.BlockSpec((tm, D_MODEL), lambda i: (i, 0)),
                  pl.BlockSpec((tm, D_MODEL), lambda i: (i, 0)),
                  mod_spec(5),
                  _resident((D_MODEL, 2 * D_FF), layer),
                  _resident((D_FF, D_MODEL), layer),
                  _resident((1, D_MODEL), layer)] + ([] if host is None else host.in_specs),
        out_specs=[pl.BlockSpec((tm, D_MODEL), lambda i: (i, 0))]
        + ([] if host is None else [host.out_spec]),
        out_shape=[jax.ShapeDtypeStruct((rows, D_MODEL), F32)]
        + ([] if host is None else [host.out_shape]),
        scratch_shapes=[pltpu.VMEM((tm, D_MODEL), F32)],
        compiler_params=_params(("parallel",)),
        name="ffn",
    )(hf, x1, mod_arr, w["w_ffn_in"], w["w_ffn_out"], w["g_post_ffn"],
      *([] if host is None else host.inputs))


ROW_TILE = 512


def kernel(x_prompt, x_sample, cache_k0, cache_v0, cache_k1, cache_v1, cache_k2, cache_v2, state_conv, state_ssm_re, state_ssm_im, c_prompt, c_sample, w_mod, b_mod, g_pre_mix, g_post_mix, g_pre_ffn, g_post_ffn, w_in, conv_w, conv_b, conv_ln_g, conv_ln_b, w_conv_out, ssm_a_re, ssm_a_im, ssm_log_dt, ssm_b_re, ssm_b_im, ssm_c_re, ssm_c_im, ssm_d, w_ssm_glu, w_att, w_out, w_ffn_in, w_ffn_out):
    nbp, t_p, _ = x_prompt.shape
    nbs, t_s, _ = x_sample.shape
    caches_k = (cache_k0, cache_k1, cache_k2)
    caches_v = (cache_v0, cache_v1, cache_v2)

    tm = ROW_TILE
    rows_p, rows_s = nbp * t_p, nbs * t_s
    per_seq = t_p // tm
    assert rows_s == tm

    mod_p, mod_s = _modulation(c_prompt, jnp.tile(c_sample, (t_s, 1)), w_mod, b_mod)
    mod_p = mod_p.reshape(DEPTH, nbp, 1, 6 * D_MODEL)
    abr, abi, bbr, bbi = _ssm_discretise(ssm_a_re, ssm_a_im, ssm_log_dt, ssm_b_re, ssm_b_im)

    w = dict(
        g_pre_mix=g_pre_mix[:, None], g_post_mix=g_post_mix[:, None],
        g_pre_ffn=g_pre_ffn[:, None], g_post_ffn=g_post_ffn[:, None],
        w_in=w_in.astype(BF16),
        conv_w=jnp.broadcast_to(conv_w[:, :, None, :], (DEPTH, CONV_WIDTH, SUBLANES, D_CONV)),
        conv_b=conv_b[:, None], conv_ln_g=conv_ln_g[:, None], conv_ln_b=conv_ln_b[:, None],
        ssm_d=ssm_d.reshape(DEPTH, 1, D_SSM),
        w_conv_out=w_conv_out.astype(BF16), w_ssm_glu=w_ssm_glu.astype(BF16),
        w_att=w_att.astype(BF16), w_out=w_out.astype(BF16),
        w_ffn_in=w_ffn_in.astype(BF16), w_ffn_out=w_ffn_out.astype(BF16))
    hist_s = jnp.transpose(state_conv, (0, 2, 1, 3))
    h0r_s = state_ssm_re.reshape(DEPTH, nbs, N_STATE)
    h0i_s = state_ssm_im.reshape(DEPTH, nbs, N_STATE)
    prev_p = jnp.zeros((nbp, HALO, D_CONV), F32)
    h0_p = jnp.zeros((nbp, N_STATE), F32)
    kts = [_cache_t(c) for c in caches_k]
    vts = [_cache_t(c) for c in caches_v]
    att_consts = _att_sample_consts(t_s)
    n_host_steps = rows_p // tm
    seqs_per_host = n_host_steps * ATT_S_BB
    assert 2 * seqs_per_host == nbs

    xp = x_prompt.reshape(rows_p, D_MODEL)
    xs = jnp.transpose(x_sample, (1, 0, 2)).reshape(rows_s, D_MODEL)
    prompt_states, sample_states = [], []
    kvt_p = None
    kvt_s = None
    hist_out = None
    for l in range(DEPTH):
        a8, bw, cw = _ssm_pack(abr[l], abi[l], bbr[l], bbi[l], ssm_c_re[l], ssm_c_im[l])

        def spec_p(c, l=l):
            return pl.BlockSpec((None, None, 1, D_MODEL), lambda i: (l, i // per_seq, 0, c))

        def spec_s(c, l=l):
            return pl.BlockSpec((None, tm, D_MODEL), lambda i: (l, i, c))

        z_s, kvt_s = _in_proj_sample(xs, mod_s, spec_s, w["g_pre_mix"], w["w_in"], nbs, t_s, l, kvt_s)
        cy_s, hist_out = _conv_sample(z_s.reshape(t_s, nbs, N_IN), hist_s, w["conv_w"], w["conv_b"],
                                      w["conv_ln_g"], w["conv_ln_b"], l, hist_out)
        zs_s, sre_s, sim_s = _ssm_branch(z_s, h0r_s, h0i_s, l, a8, bw, cw, w["ssm_d"], l,
                                         nb=nbs, t_len=t_s, tc=t_s, flat=True)
        qkv_s = z_s[:, OFF_Q:OFF_Q + 3 * D_ATT].astype(F32).reshape(t_s, nbs, 3 * D_ATT)
        qkv_s = jnp.pad(jnp.transpose(qkv_s, (1, 0, 2)), ((0, 0), (0, QROWS - t_s), (0, 0)))
        qkv_s = qkv_s.reshape(nbs * QROWS, 3 * D_ATT)

        z_p, kvt_p = _in_proj_prompt(xp, mod_p, spec_p, w["g_pre_mix"], w["w_in"], tm,
                                     nbp, t_p, l, kvt_p)
        z3 = z_p.reshape(nbp, t_p, N_IN)
        cy_p, conv_p = _conv_branch(z3, prev_p, w["conv_w"], w["conv_b"], w["conv_ln_g"],
                                    w["conv_ln_b"], l, bb=1, tq=512)
        zs_p, sre_p, sim_p = _ssm_branch(z3, h0_p, h0_p, None, a8, bw, cw, w["ssm_d"], l,
                                         nb=nbp, t_len=t_p, tc=128, flat=False)
        att_p = _att_prompt(z3)
        host_a = _AttSampleHost(qkv_s, kts, vts, att_consts, l, 0, n_host_steps)
        host_b = _AttSampleHost(qkv_s, kts, vts, att_consts, l, seqs_per_host, n_host_steps)
        x1_p, hf_p, att_a = _merge(cy_p.reshape(rows_p, D_CONV), zs_p.reshape(rows_p, D_SSM),
                                   att_p.reshape(rows_p, D_ATT_OUT), z_p, xp, mod_p, spec_p, w, l, tm,
                                   host=host_a)
        xp, att_b = _ffn(hf_p, x1_p, mod_p, spec_p, w, l, tm, host=host_b)

        att_s = jnp.concatenate([att_a, att_b], axis=0).reshape(nbs, QROWS, D_ATT_OUT)[:, :t_s]
        att_s = jnp.transpose(att_s, (1, 0, 2)).reshape(rows_s, D_ATT_OUT)
        x1_s, hf_s = _merge(cy_s.reshape(rows_s, D_CONV), zs_s, att_s, z_s, xs, mod_s, spec_s, w, l, tm)
        xs, = _ffn(hf_s, x1_s, mod_s, spec_s, w, l, tm)

        prompt_states.append((conv_p, sre_p.reshape(nbp, N_SSM_GROUPS, SSM_STATE),
                              sim_p.reshape(nbp, N_SSM_GROUPS, SSM_STATE)))
        sample_states.append((sre_s.reshape(nbs, N_SSM_GROUPS, SSM_STATE),
                              sim_s.reshape(nbs, N_SSM_GROUPS, SSM_STATE)))

    ps = [jnp.transpose(a.reshape(DEPTH, nbp, HEADS_PER_GROUP, HEAD_DIM, a.shape[-1]), (0, 1, 4, 2, 3))
          for a in kvt_p]
    ps += [jnp.stack([s[i] for s in prompt_states]) for i in range(3)]
    ss = [jnp.transpose(a.reshape(DEPTH, t_s, HEADS_PER_GROUP, HEAD_DIM, nbs), (0, 4, 1, 2, 3))
          for a in kvt_s]
    ss.append(jnp.transpose(hist_out, (0, 2, 1, 3)))
    ss += [jnp.stack([s[i] for s in sample_states]) for i in range(2)]
    xs = jnp.transpose(xs.reshape(t_s, nbs, D_MODEL), (1, 0, 2))
    return (xp.reshape(nbp, t_p, D_MODEL), xs, *ps, *ss)
```

```python
import functools

import numpy as np
import jax
import jax.numpy as jnp
from jax import lax
from jax.experimental import pallas as pl
from jax.experimental.pallas import tpu as pltpu

F32 = jnp.float32
BF16 = jnp.bfloat16

D_MODEL = 1024
DEPTH = 2
D_CONV = 512
CONV_WIDTH = 31
D_SSM = 512
SSM_GROUP = 16
N_SSM_GROUPS = 32
SSM_STATE = 64
HEAD_DIM = 64
HEADS_PER_GROUP = 4
ATT_PATTERNS = ((128, 1), (512, 4), (2048, 16))
N_ATT_HEADS = 12
D_ATT = 768
D_ATT_OUT = 256
ATT_SCALE = HEAD_DIM ** -0.5
D_FF = 2816
N_IN = 6912
RMS_EPS = 1e-6
LN_EPS = 1e-5
NEG = -1e30
BAND = 128

OFF_CONV = 0
OFF_SSM = 1024
OFF_Q = 1536
OFF_K = 2304
OFF_V = 3072
OFF_GATES = 3840
GATE_BLOCK = 768
IN_CHUNK = 768

LANES = 128
SUBLANES = 8
HALO = 32
N_STATE = N_SSM_GROUPS * SSM_STATE
OCTETS = 4
OCT_IN = D_SSM // OCTETS
OCT_ST = N_STATE // OCTETS
VMEM_LIMIT = 56 * 1024 * 1024


def _params(sem):
    return pltpu.CompilerParams(dimension_semantics=sem, vmem_limit_bytes=VMEM_LIMIT)


def _resident(shape, layer=None):
    if layer is None:
        return pl.BlockSpec(shape, lambda *_: (0,) * len(shape), pipeline_mode=pl.Buffered(1))
    return pl.BlockSpec((None,) + tuple(shape), lambda *_: (layer,) + (0,) * len(shape),
                        pipeline_mode=pl.Buffered(1))


def _slopes():
    return [2.0 ** (-8.0 * (i + 1) / N_ATT_HEADS) for i in range(N_ATT_HEADS)]


def _rms(x, g):
    return x * lax.rsqrt(jnp.mean(x * x, axis=-1, keepdims=True) + RMS_EPS) * g


def _sigmoid(x):
    return 0.5 * jnp.tanh(0.5 * x) + 0.5


def _silu(x):
    return x * _sigmoid(x)


def _dot(a, b):
    return jnp.dot(a, b, preferred_element_type=F32)


def _dot_nt(a, b):
    return lax.dot_general(a, b, (((1,), (1,)), ((), ())), preferred_element_type=F32)


def _mod_kernel(cp_ref, cs_ref, w_ref, b_ref, op_ref, os_ref):
    w = w_ref[...].astype(BF16)
    op_ref[...] = _dot(_silu(cp_ref[...]).astype(BF16), w) + b_ref[...]
    os_ref[...] = _dot(_silu(cs_ref[...]).astype(BF16), w) + b_ref[...]


def _modulation(c_p, c_s_rows, w_mod, b_mod):
    np_, ns = c_p.shape[0], c_s_rows.shape[0]
    tn = 2048
    return pl.pallas_call(
        _mod_kernel,
        grid=(DEPTH, 6 * D_MODEL // tn),
        in_specs=[pl.BlockSpec((np_, D_MODEL), lambda l, j: (0, 0)),
                  pl.BlockSpec((ns, D_MODEL), lambda l, j: (0, 0)),
                  pl.BlockSpec((None, D_MODEL, tn), lambda l, j: (l, 0, j)),
                  pl.BlockSpec((None, 1, tn), lambda l, j: (l, 0, j))],
        out_specs=[pl.BlockSpec((None, np_, tn), lambda l, j: (l, 0, j)),
                   pl.BlockSpec((None, ns, tn), lambda l, j: (l, 0, j))],
        out_shape=[jax.ShapeDtypeStruct((DEPTH, np_, 6 * D_MODEL), F32),
                   jax.ShapeDtypeStruct((DEPTH, ns, 6 * D_MODEL), F32)],
        compiler_params=_params(("parallel", "parallel")),
        name="modulation",
    )(c_p, c_s_rows, w_mod, b_mod.reshape(DEPTH, 1, 6 * D_MODEL))


def _ssm_disc_kernel(ar_ref, ai_ref, ldt_ref, br_ref, bi_ref, abr_ref, abi_ref, bbr_ref, bbi_ref):
    ar = ar_ref[...]
    ai = ai_ref[...]
    dt = jnp.exp(ldt_ref[...])
    mag = jnp.exp(dt * ar)
    abr = mag * jnp.cos(dt * ai)
    abi = mag * jnp.sin(dt * ai)
    den = ar * ar + ai * ai
    fr = ((abr - 1.0) * ar + abi * ai) / den
    fi = (abi * ar - (abr - 1.0) * ai) / den
    br = br_ref[...]
    bi = bi_ref[...]
    abr_ref[...] = abr
    abi_ref[...] = abi
    bbr_ref[...] = fr * br - fi * bi
    bbi_ref[...] = fr * bi + fi * br


def _ssm_discretise(a_re, a_im, log_dt, b_re, b_im):
    ar = a_re.reshape(DEPTH, 1, N_STATE)
    ai = a_im.reshape(DEPTH, 1, N_STATE)
    ldt = jnp.repeat(log_dt, SSM_STATE, axis=-1).reshape(DEPTH, 1, N_STATE)
    br = jnp.transpose(b_re, (0, 3, 1, 2)).reshape(DEPTH, SSM_GROUP, N_STATE)
    bi = jnp.transpose(b_im, (0, 3, 1, 2)).reshape(DEPTH, SSM_GROUP, N_STATE)
    row = pl.BlockSpec((None, 1, N_STATE), lambda l: (l, 0, 0))
    mat = pl.BlockSpec((None, SSM_GROUP, N_STATE), lambda l: (l, 0, 0))
    return pl.pallas_call(
        _ssm_disc_kernel,
        grid=(DEPTH,),
        in_specs=[row, row, row, mat, mat],
        out_specs=[row, row, mat, mat],
        out_shape=[jax.ShapeDtypeStruct((DEPTH, 1, N_STATE), F32)] * 2
        + [jax.ShapeDtypeStruct((DEPTH, SSM_GROUP, N_STATE), F32)] * 2,
        compiler_params=_params(("parallel",)),
        name="ssm_discretise",
    )(ar, ai, ldt, br, bi)


def _ssm_pack(abr, abi, bbr, bbi, c_re, c_im):
    gpo = N_SSM_GROUPS // OCTETS
    eye = jnp.eye(gpo, dtype=F32)

    def oct_state(v):
        return v.reshape(OCTETS, OCT_ST)

    a8 = jnp.concatenate([oct_state(abr), oct_state(abi)], axis=1).reshape(1, 2 * N_STATE)
    a8 = jnp.broadcast_to(a8, (SUBLANES, 2 * N_STATE))

    def b_tiles(bb):
        t = bb.reshape(SSM_GROUP, OCTETS, gpo, SSM_STATE)
        t = jnp.transpose(t, (1, 2, 0, 3))
        t = t[:, :, :, None, :] * eye[None, :, None, :, None]
        return t.reshape(OCTETS, OCT_IN, OCT_ST)

    bw = jnp.concatenate([b_tiles(bbr), b_tiles(bbi)], axis=2).astype(BF16)

    def c_tiles(cc):
        t = cc.reshape(OCTETS, gpo, SSM_GROUP, SSM_STATE)
        t = jnp.transpose(t, (0, 1, 3, 2))
        t = t[:, :, :, None, :] * eye[None, :, None, :, None]
        return t.reshape(OCTETS, OCT_ST, OCT_IN)

    cw = jnp.concatenate([c_tiles(c_re), -c_tiles(c_im)], axis=1).astype(BF16)
    return a8, bw, cw


def _is_kv_chunk(lo):
    return OFF_K <= lo < OFF_K + 2 * D_ATT


def _in_proj_sample_kernel(*refs, nb, t_len, layer, n_alias):
    x_ref, sh_ref, sc_ref, g_ref, w_ref = refs[:5]
    z_ref = refs[5 + n_alias]
    kvt_refs = refs[6 + n_alias:]
    y = _rms(x_ref[...], g_ref[...])
    h = (y * (1.0 + sc_ref[...]) + sh_ref[...]).astype(BF16)
    for c in range(N_IN // IN_CHUNK):
        lo = c * IN_CHUNK
        r = _dot(h, w_ref[:, lo:lo + IN_CHUNK])
        z_ref[:, lo:lo + IN_CHUNK] = r.astype(BF16)
        if _is_kv_chunk(lo):
            which = (lo - OFF_K) // D_ATT
            for g in range(len(ATT_PATTERNS)):
                for t in range(t_len):
                    tile = r[t * nb:(t + 1) * nb, g * D_ATT_OUT:(g + 1) * D_ATT_OUT].T
                    if n_alias:
                        kvt_refs[2 * g + which][t] = tile
                    else:
                        for l in range(DEPTH):
                            kvt_refs[2 * g + which][l, t] = tile if l == layer else jnp.zeros_like(tile)


def _in_proj_sample(x2, mod_arr, mod_spec, g_pre, w_bf, nb, t_len, layer, prev_kvt):
    rows = x2.shape[0]
    n_alias = 0 if prev_kvt is None else len(prev_kvt)
    kern = functools.partial(_in_proj_sample_kernel, nb=nb, t_len=t_len, layer=layer, n_alias=n_alias)
    n_kv = 2 * len(ATT_PATTERNS)
    if n_alias:
        kv_spec = pl.BlockSpec((None, t_len, D_ATT_OUT, nb), lambda i: (layer, 0, 0, 0))
    else:
        kv_spec = pl.BlockSpec((DEPTH, t_len, D_ATT_OUT, nb), lambda i: (0, 0, 0, 0))
    outs = pl.pallas_call(
        kern,
        grid=(1,),
        in_specs=[pl.BlockSpec((rows, D_MODEL), lambda i: (0, 0)),
                  mod_spec(0), mod_spec(1),
                  _resident((1, D_MODEL), layer),
                  _resident((D_MODEL, N_IN), layer)] + [pl.BlockSpec(memory_space=pl.ANY)] * n_alias,
        out_specs=[pl.BlockSpec((rows, N_IN), lambda i: (0, 0))]
        + [kv_spec] * n_kv,
        out_shape=[jax.ShapeDtypeStruct((rows, N_IN), BF16)]
        + [jax.ShapeDtypeStruct((DEPTH, t_len, D_ATT_OUT, nb), F32)] * n_kv,
        input_output_aliases={5 + k: 1 + k for k in range(n_alias)},
        compiler_params=_params(("arbitrary",)),
        name="in_proj_sample",
    )(x2, mod_arr, mod_arr, g_pre, w_bf, *(prev_kvt or ()))
    return outs[0], list(outs[1:])


def _in_proj_prompt_kernel(*refs, keeps, tm, layer, n_alias):
    x_ref, sh_ref, sc_ref, g_ref, w_ref = refs[:5]
    z_ref = refs[5 + n_alias]
    kvt_refs = refs[6 + n_alias:]
    y = _rms(x_ref[...], g_ref[...])
    h = (y * (1.0 + sc_ref[...]) + sh_ref[...]).astype(BF16)
    for c in range(N_IN // IN_CHUNK):
        lo = c * IN_CHUNK
        r = _dot(h, w_ref[:, lo:lo + IN_CHUNK])
        z_ref[:, lo:lo + IN_CHUNK] = r.astype(BF16)
        if _is_kv_chunk(lo):
            which = (lo - OFF_K) // D_ATT
            for g, keep in enumerate(keeps):
                kept = min(keep, tm)
                tile = r[tm - kept:, g * D_ATT_OUT:(g + 1) * D_ATT_OUT].T
                if n_alias:
                    kvt_refs[2 * g + which][...] = tile
                else:
                    for l in range(DEPTH):
                        kvt_refs[2 * g + which][l] = tile if l == layer else jnp.zeros_like(tile)


def _in_proj_prompt(x2, mod_arr, mod_spec, g_pre, w_bf, tm, nbatch, t_len, layer, prev_kvt):
    rows = x2.shape[0]
    per_seq = t_len // tm
    keeps = tuple(min(window, t_len) for window, _ in ATT_PATTERNS)
    n_alias = 0 if prev_kvt is None else len(prev_kvt)
    kern = functools.partial(_in_proj_prompt_kernel, keeps=keeps, tm=tm, layer=layer, n_alias=n_alias)
    kvt_specs, kvt_shapes = [], []
    for keep in keeps:
        first = per_seq - max(keep // tm, 1)
        kept = min(keep, tm)
        for _ in range(2):
            kvt_specs.append(pl.BlockSpec(
                (None if n_alias else DEPTH, None, D_ATT_OUT, kept),
                lambda i, first=first: (layer if n_alias else 0, i // per_seq, 0,
                                        jnp.maximum(i % per_seq - first, 0))))
            kvt_shapes.append(jax.ShapeDtypeStruct((DEPTH, nbatch, D_ATT_OUT, keep), F32))
    outs = pl.pallas_call(
        kern,
        grid=(rows // tm,),
        in_specs=[pl.BlockSpec((tm, D_MODEL), lambda i: (i, 0)),
                  mod_spec(0), mod_spec(1),
                  _resident((1, D_MODEL), layer),
                  _resident((D_MODEL, N_IN), layer)] + [pl.BlockSpec(memory_space=pl.ANY)] * n_alias,
        out_specs=[pl.BlockSpec((tm, N_IN), lambda i: (i, 0))] + kvt_specs,
        out_shape=[jax.ShapeDtypeStruct((rows, N_IN), BF16)] + kvt_shapes,
        input_output_aliases={5 + k: 1 + k for k in range(n_alias)},
        compiler_params=_params(("arbitrary",)),
        name="in_proj_prompt",
    )(x2, mod_arr, mod_arr, g_pre, w_bf, *(prev_kvt or ()))
    return outs[0], list(outs[1:])


def _conv_kernel(a_ref, prev_ref, w_ref, b_ref, lg_ref, lb_ref, y_ref, st_ref, ubuf, sh_scr,
                 *, bb, tq, nt, sub):
    i = pl.program_id(1)
    base = HALO - (CONV_WIDTH - 1)
    span = tq + HALO - SUBLANES
    nrt = max(sub // SUBLANES, 1)
    rpt = min(sub, SUBLANES)
    for b in range(bb):
        @pl.when(i == 0)
        def _():
            ubuf[b, 0:HALO, :] = prev_ref[b]

        @pl.when(i > 0)
        def _():
            ubuf[b, 0:HALO, :] = ubuf[b, tq:tq + HALO, :]

        a = a_ref[b].astype(F32)
        ubuf[b, HALO:HALO + tq, :] = a[:, :D_CONV] * _sigmoid(a[:, D_CONV:])
        for r in range(1, SUBLANES):
            sh_scr[r - 1, 0:span, :] = ubuf[b, r:r + span, :]
        for r0 in range(0, tq, sub):
            acc = jnp.zeros((nrt, rpt, D_CONV), F32) + b_ref[...]
            for j in range(CONV_WIDTH):
                off = base + j
                al = r0 + (off // SUBLANES) * SUBLANES
                if off % SUBLANES == 0:
                    src = ubuf[b, al:al + sub, :]
                else:
                    src = sh_scr[off % SUBLANES - 1, al:al + sub, :]
                acc = acc + w_ref[j, 0:rpt, :][None] * src.reshape(nrt, rpt, D_CONV)
            acc = acc.reshape(sub, D_CONV)
            mu = jnp.mean(acc, axis=-1, keepdims=True)
            xc = acc - mu
            yn = xc * lax.rsqrt(jnp.mean(xc * xc, axis=-1, keepdims=True) + LN_EPS)
            y_ref[b, r0:r0 + sub, :] = _silu(yn * lg_ref[...] + lb_ref[...]).astype(y_ref.dtype)

        @pl.when(i == nt - 1)
        def _():
            st_ref[b] = ubuf[b, base + tq:base + tq + CONV_WIDTH - 1, :]


def _conv_branch(z3, prev_pad, w_b8, b, lg, lb, layer, bb, tq):
    nbatch, t_len, _ = z3.shape
    nt = t_len // tq
    sub = min(tq, 32)
    kern = functools.partial(_conv_kernel, bb=bb, tq=tq, nt=nt, sub=sub)
    col = OFF_CONV // (2 * D_CONV)
    return pl.pallas_call(
        kern,
        grid=(nbatch // bb, nt),
        in_specs=[pl.BlockSpec((bb, tq, 2 * D_CONV), lambda i, j: (i, j, col)),
                  pl.BlockSpec((bb, HALO, D_CONV), lambda i, j: (i, 0, 0)),
                  _resident((CONV_WIDTH, SUBLANES, D_CONV), layer),
                  _resident((1, D_CONV), layer), _resident((1, D_CONV), layer),
                  _resident((1, D_CONV), layer)],
        out_specs=[pl.BlockSpec((bb, tq, D_CONV), lambda i, j: (i, j, 0)),
                   pl.BlockSpec((bb, CONV_WIDTH - 1, D_CONV), lambda i, j: (i, 0, 0))],
        out_shape=[jax.ShapeDtypeStruct((nbatch, t_len, D_CONV), BF16),
                   jax.ShapeDtypeStruct((nbatch, CONV_WIDTH - 1, D_CONV), F32)],
        scratch_shapes=[pltpu.VMEM((bb, HALO + tq + SUBLANES, D_CONV), F32),
                        pltpu.VMEM((SUBLANES - 1, HALO + tq, D_CONV), F32)],
        compiler_params=_params(("parallel", "arbitrary")),
        name="conv_branch",
    )(z3, prev_pad, w_b8, b, lg, lb)


def _conv_sample_kernel(*refs, t_len, nbb, layer, n_alias):
    a_ref, prev_ref, w_ref, b_ref, lg_ref, lb_ref = refs[:6]
    y_ref, st_ref = refs[6 + n_alias:]
    hist = CONV_WIDTH - 1
    if not n_alias:
        for l in range(DEPTH):
            if l != layer:
                st_ref[l] = jnp.zeros((hist, nbb, D_CONV), F32)
        st_ref = st_ref.at[layer]
    for k in range(hist - t_len):
        st_ref[k] = prev_ref[k + t_len]
    for t in range(t_len):
        a = a_ref[t].astype(F32)
        st_ref[hist - t_len + t] = a[:, :D_CONV] * _sigmoid(a[:, D_CONV:])
    nrt = nbb // SUBLANES
    for t in range(t_len):
        acc = jnp.zeros((nrt, SUBLANES, D_CONV), F32) + b_ref[...]
        for j in range(CONV_WIDTH):
            k = t + j
            src = prev_ref[k] if k < hist else st_ref[k - t_len]
            acc = acc + w_ref[j][None] * src.reshape(nrt, SUBLANES, D_CONV)
        acc = acc.reshape(nbb, D_CONV)
        mu = jnp.mean(acc, axis=-1, keepdims=True)
        xc = acc - mu
        yn = xc * lax.rsqrt(jnp.mean(xc * xc, axis=-1, keepdims=True) + LN_EPS)
        y_ref[t] = _silu(yn * lg_ref[...] + lb_ref[...]).astype(y_ref.dtype)


def _conv_sample(z_t, state_t, w_b8, b, lg, lb, layer, prev_out):
    t_len, nb, _ = z_t.shape
    nbb = 32
    hist = CONV_WIDTH - 1
    n_alias = 0 if prev_out is None else 1
    kern = functools.partial(_conv_sample_kernel, t_len=t_len, nbb=nbb, layer=layer, n_alias=n_alias)
    st_spec = pl.BlockSpec((None, hist, nbb, D_CONV), lambda i: (layer, 0, i, 0))
    if n_alias:
        st_out_spec = st_spec
    else:
        st_out_spec = pl.BlockSpec((DEPTH, hist, nbb, D_CONV), lambda i: (0, 0, i, 0))
    return pl.pallas_call(
        kern,
        grid=(nb // nbb,),
        in_specs=[pl.BlockSpec((t_len, nbb, 2 * D_CONV), lambda i: (0, i, OFF_CONV // (2 * D_CONV))),
                  st_spec,
                  _resident((CONV_WIDTH, SUBLANES, D_CONV), layer),
                  _resident((1, D_CONV), layer), _resident((1, D_CONV), layer),
                  _resident((1, D_CONV), layer)] + [pl.BlockSpec(memory_space=pl.ANY)] * n_alias,
        out_specs=[pl.BlockSpec((t_len, nbb, D_CONV), lambda i: (0, i, 0)), st_out_spec],
        out_shape=[jax.ShapeDtypeStruct((t_len, nb, D_CONV), BF16),
                   jax.ShapeDtypeStruct((DEPTH, hist, nb, D_CONV), F32)],
        input_output_aliases={6: 1} if n_alias else {},
        compiler_params=_params(("parallel",)),
        name="conv_sample",
    )(z_t, state_t, w_b8, b, lg, lb, *(() if prev_out is None else (prev_out,)))


def _ssm_kernel(u_ref, h0r_ref, h0i_ref, a_ref, bw_ref, cw_ref, d_ref, y_ref, hr_ref, hi_ref,
                ub_scr, u_scr, s_scr, y_scr, h_scr, *, nb, tc, nt, flat):
    i = pl.program_id(0)
    st2 = 2 * OCT_ST

    @pl.when(i == 0)
    def _():
        for o in range(OCTETS):
            h_scr[:, st2 * o:st2 * o + OCT_ST] = h0r_ref[:, OCT_ST * o:OCT_ST * (o + 1)]
            h_scr[:, st2 * o + OCT_ST:st2 * (o + 1)] = h0i_ref[:, OCT_ST * o:OCT_ST * (o + 1)]

    if flat:
        u_scr[...] = u_ref[...].astype(F32)
    else:
        for o in range(OCTETS):
            cols = slice(OCT_IN * o, OCT_IN * (o + 1))
            for b in range(nb):
                ub_scr[o, b * tc:(b + 1) * tc, :] = u_ref[b, :, cols].astype(F32)
            for t in range(tc):
                u_scr[t * nb:(t + 1) * nb, cols] = ub_scr[o, pl.ds(t, nb, stride=tc), :]

    for o in range(OCTETS):
        ub = u_scr[:, OCT_IN * o:OCT_IN * (o + 1)].astype(BF16)
        s_scr[:, st2 * o:st2 * (o + 1)] = _dot(ub, bw_ref[o])

    for o in range(OCTETS):
        c0 = st2 * o
        re = slice(c0, c0 + OCT_ST)
        im = slice(c0 + OCT_ST, c0 + st2)
        ar = a_ref[:, re]
        ai = a_ref[:, im]

        def run(r0, hr, hi, re=re, im=im, ar=ar, ai=ai):
            for t in range(tc):
                row = r0 + t * nb
                if isinstance(row, int):
                    rows = slice(row, row + SUBLANES)
                else:
                    rows = pl.ds(pl.multiple_of(row, SUBLANES), SUBLANES)
                nr = ar * hr - ai * hi + s_scr[rows, re]
                ni = ar * hi + ai * hr + s_scr[rows, im]
                s_scr[rows, re] = nr
                s_scr[rows, im] = ni
                hr, hi = nr, ni
            return hr, hi

        if nb == SUBLANES:
            hr, hi = run(0, h_scr[:, re], h_scr[:, im])
            h_scr[:, re] = hr
            h_scr[:, im] = hi
        else:
            def rows_body(rg, carry, re=re, im=im, run=run):
                r0 = pl.multiple_of(rg * SUBLANES, SUBLANES)
                hr, hi = run(r0, h_scr[pl.ds(r0, SUBLANES), re], h_scr[pl.ds(r0, SUBLANES), im])
                h_scr[pl.ds(r0, SUBLANES), re] = hr
                h_scr[pl.ds(r0, SUBLANES), im] = hi
                return carry

            lax.fori_loop(0, nb // SUBLANES, rows_body, 0)

    for o in range(OCTETS):
        hb = s_scr[:, st2 * o:st2 * (o + 1)].astype(BF16)
        y_scr[:, OCT_IN * o:OCT_IN * (o + 1)] = _dot(hb, cw_ref[o])

    yv = jax.nn.gelu(y_scr[...] + d_ref[...] * u_scr[...])
    if flat:
        y_ref[...] = yv.astype(y_ref.dtype)
    else:
        y_scr[...] = yv
        for o in range(OCTETS):
            cols = slice(OCT_IN * o, OCT_IN * (o + 1))
            for t in range(tc):
                ub_scr[o, pl.ds(t, nb, stride=tc), :] = y_scr[t * nb:(t + 1) * nb, cols]
            for b in range(nb):
                y_ref[b, :, cols] = ub_scr[o, b * tc:(b + 1) * tc, :].astype(y_ref.dtype)

    @pl.when(i == nt - 1)
    def _():
        for o in range(OCTETS):
            hr_ref[:, OCT_ST * o:OCT_ST * (o + 1)] = h_scr[:, st2 * o:st2 * o + OCT_ST]
            hi_ref[:, OCT_ST * o:OCT_ST * (o + 1)] = h_scr[:, st2 * o + OCT_ST:st2 * (o + 1)]


def _ssm_branch(z, h0r, h0i, h0_layer, a8, bw, cw, dvec, layer, nb, t_len, tc, flat):
    nt = t_len // tc
    m = nb * tc
    kern = functools.partial(_ssm_kernel, nb=nb, tc=tc, nt=nt, flat=flat)
    if flat:
        u_spec = pl.BlockSpec((m, D_SSM), lambda i: (0, OFF_SSM // D_SSM))
        y_spec = pl.BlockSpec((m, D_SSM), lambda i: (0, 0))
        y_shape = jax.ShapeDtypeStruct((m, D_SSM), BF16)
    else:
        u_spec = pl.BlockSpec((nb, tc, D_SSM), lambda i: (0, i, OFF_SSM // D_SSM))
        y_spec = pl.BlockSpec((nb, tc, D_SSM), lambda i: (0, i, 0))
        y_shape = jax.ShapeDtypeStruct((nb, t_len, D_SSM), BF16)
    st_spec = pl.BlockSpec((nb, N_STATE), lambda i: (0, 0))
    if h0_layer is None:
        h0_spec = st_spec
    else:
        h0_spec = pl.BlockSpec((None, nb, N_STATE), lambda i: (h0_layer, 0, 0))
    return pl.pallas_call(
        kern,
        grid=(nt,),
        in_specs=[u_spec, h0_spec, h0_spec,
                  _resident((SUBLANES, 2 * N_STATE)),
                  _resident((OCTETS, OCT_IN, 2 * OCT_ST)),
                  _resident((OCTETS, 2 * OCT_ST, OCT_IN)),
                  _resident((1, D_SSM), layer)],
        out_specs=[y_spec, st_spec, st_spec],
        out_shape=[y_shape, jax.ShapeDtypeStruct((nb, N_STATE), F32),
                   jax.ShapeDtypeStruct((nb, N_STATE), F32)],
        scratch_shapes=[pltpu.VMEM((OCTETS, m, OCT_IN), F32), pltpu.VMEM((m, D_SSM), F32),
                        pltpu.VMEM((m, 2 * N_STATE), F32), pltpu.VMEM((m, D_SSM), F32),
                        pltpu.VMEM((nb, 2 * N_STATE), F32)],
        compiler_params=_params(("arbitrary",)),
        name="ssm_branch",
    )(z, h0r, h0i, a8, bw, cw, dvec)


def _att_prompt_kernel(q0_ref, q1_ref, k0_ref, k1_ref, v0_ref, v1_ref, o_ref,
                       qkv_scr, o_scr, l_scr, bp_scr, *, t_len):
    g_id = pl.program_id(1)
    slopes = _slopes()
    lane = lax.broadcasted_iota(jnp.int32, (BAND, LANES), 1)
    low = lane < HEAD_DIM

    in_refs = (q0_ref, q1_ref, k0_ref, k1_ref, v0_ref, v1_ref)

    def group(g, dil):
        nblk = t_len // dil // BAND
        if dil > 1:
            for n, r in enumerate(in_refs):
                qkv_scr[n] = r[...].astype(F32)
        qi = lax.broadcasted_iota(jnp.int32, (BAND, 2 * BAND), 0)
        kj = lax.broadcasted_iota(jnp.int32, (BAND, 2 * BAND), 1)
        dist = qi + BAND - kj
        valid = (dist >= 0) & (dist <= BAND)
        valid0 = valid & (kj >= BAND)
        distf = dist.astype(F32)
        for h in range(HEADS_PER_GROUP):
            bias = (-slopes[g * HEADS_PER_GROUP + h] * dil) * distf
            bp_scr[0, h] = jnp.where(valid0, bias, NEG)
            bp_scr[1, h] = jnp.where(valid, bias, NEG)

        def rows(start):
            if dil == 1:
                return pl.ds(start, BAND)
            return pl.ds(start, BAND, stride=dil)

        def block(blk, carry):
            r = blk // nblk
            n = blk % nblk
            start_c = r + dil * BAND * n
            start_p = r + dil * BAND * jnp.maximum(n - 1, 0)
            sel = jnp.minimum(n, 1)
            for pair in range(2):
                if dil == 1:
                    def take(n, start):
                        return in_refs[n][pl.ds(pl.multiple_of(start, BAND), BAND), :]
                else:
                    def take(n, start):
                        return qkv_scr[n, rows(start), :]
                qp = take(pair, start_c) * ATT_SCALE
                kp = jnp.concatenate([take(2 + pair, start_p), take(2 + pair, start_c)], axis=0).astype(BF16)
                vp = jnp.concatenate([take(4 + pair, start_p), take(4 + pair, start_c)], axis=0).astype(BF16)
                o_h, l_h = [], []
                for hh in range(2):
                    h = 2 * pair + hh
                    qm = jnp.where(low if hh == 0 else jnp.logical_not(low), qp, 0.0).astype(BF16)
                    t = _dot_nt(qm, kp) + bp_scr[sel, h]
                    mx = jnp.max(t, axis=-1, keepdims=True)
                    p = jnp.exp(t - mx)
                    den = jnp.sum(p, axis=-1, keepdims=True)
                    pv = _dot(p.astype(BF16), vp)
                    o_h.append(pv / den)
                    l_h.append(jnp.broadcast_to(mx + jnp.log(den), (BAND, LANES)))
                o_scr[2 * g + pair, rows(start_c), :] = jnp.where(low, o_h[0], o_h[1])
                l_scr[2 * g + pair, rows(start_c), :] = jnp.where(low, l_h[0], l_h[1])
            return carry

        lax.fori_loop(0, dil * nblk, block, 0, unroll=4)

    for g, (window, dil) in enumerate(ATT_PATTERNS):
        @pl.when(g_id == g)
        def _(g=g, dil=dil):
            group(g, dil)

    @pl.when(g_id == len(ATT_PATTERNS) - 1)
    def _():
        tr = 256

        def comb(i, carry):
            r0 = pl.multiple_of(i * tr, tr)
            for pair in range(2):
                l0 = l_scr[pair, pl.ds(r0, tr), :]
                l1 = l_scr[2 + pair, pl.ds(r0, tr), :]
                l2 = l_scr[4 + pair, pl.ds(r0, tr), :]
                mx = jnp.maximum(jnp.maximum(l0, l1), l2)
                w0 = jnp.exp(l0 - mx)
                w1 = jnp.exp(l1 - mx)
                w2 = jnp.exp(l2 - mx)
                num = (w0 * o_scr[pair, pl.ds(r0, tr), :] + w1 * o_scr[2 + pair, pl.ds(r0, tr), :]
                       + w2 * o_scr[4 + pair, pl.ds(r0, tr), :])
                o_ref[pl.ds(r0, tr), LANES * pair:LANES * (pair + 1)] = (
                    num / (w0 + w1 + w2)).astype(o_ref.dtype)
            return carry

        lax.fori_loop(0, t_len // tr, comb, 0)


def _att_prompt(z3):
    nbatch, t_len, _ = z3.shape
    ng = len(ATT_PATTERNS)
    kern = functools.partial(_att_prompt_kernel, t_len=t_len)

    def spec(off, pair):
        return pl.BlockSpec((None, t_len, LANES), lambda b, g: (b, 0, off // LANES + 2 * g + pair))

    return pl.pallas_call(
        kern,
        grid=(nbatch, ng),
        in_specs=[spec(off, pair) for off in (OFF_Q, OFF_K, OFF_V) for pair in range(2)],
        out_specs=pl.BlockSpec((None, t_len, D_ATT_OUT), lambda b, g: (b, 0, 0)),
        out_shape=jax.ShapeDtypeStruct((nbatch, t_len, D_ATT_OUT), BF16),
        scratch_shapes=[pltpu.VMEM((6, t_len, LANES), F32),
                        pltpu.VMEM((2 * ng, t_len, LANES), F32),
                        pltpu.VMEM((2 * ng, t_len, LANES), F32),
                        pltpu.VMEM((2, HEADS_PER_GROUP, BAND, 2 * BAND), F32)],
        compiler_params=_params(("parallel", "arbitrary")),
        name="att_prompt",
    )(z3, z3, z3, z3, z3, z3)


QROWS = 8
ATT_S_BB = 2


def _att_sample_stages(q_ref, kn_ref, vn_ref, k0_ref, k1_ref, k2_ref, v0_ref, v1_ref, v2_ref,
                       b0_ref, b1_ref, b2_ref, bn_ref, o_ref):
    kc_refs = (k0_ref, k1_ref, k2_ref)
    vc_refs = (v0_ref, v1_ref, v2_ref)
    bc_refs = (b0_ref, b1_ref, b2_ref)
    w = D_ATT_OUT
    ng = len(ATT_PATTERNS)
    state = {}

    def stage(b, g):
        def run():
            head = lax.broadcasted_iota(jnp.int32, (QROWS, w), 1) // HEAD_DIM
            zpad = jnp.zeros((LANES - QROWS, w), BF16)
            rows = slice(QROWS * b, QROWS * (b + 1))
            outs, lses = state.setdefault(b, ([], []))
            cols = slice(g * w, (g + 1) * w)
            qg = q_ref[rows, cols] * ATT_SCALE
            qblk = jnp.concatenate([jnp.where(head == h, qg, 0.0) for h in range(HEADS_PER_GROUP)],
                                   axis=0).astype(BF16)
            s_c = _dot(qblk, kc_refs[g][b].astype(BF16))
            bias_c = bc_refs[g][...]
            t_c = jnp.where(bias_c > 0.5 * NEG, s_c + bias_c, NEG)
            kn = jnp.concatenate([kn_ref[rows, cols].astype(BF16), zpad], axis=0)
            vn = jnp.concatenate([vn_ref[rows, cols].astype(BF16), zpad], axis=0)
            bias_n = bn_ref[g]
            t_n = jnp.where(bias_n > 0.5 * NEG, _dot_nt(qblk, kn) + bias_n, NEG)
            mx = jnp.maximum(jnp.max(t_c, axis=-1, keepdims=True), jnp.max(t_n, axis=-1, keepdims=True))
            p_c = jnp.exp(t_c - mx)
            p_n = jnp.exp(t_n - mx)
            den = jnp.sum(p_c, axis=-1, keepdims=True) + jnp.sum(p_n, axis=-1, keepdims=True)
            o = _dot_nt(p_c.astype(BF16), vc_refs[g][b].astype(BF16))
            o = (o + _dot(p_n.astype(BF16), vn)) / den
            lse = jnp.broadcast_to(mx + jnp.log(den), o.shape)
            og = jnp.zeros((QROWS, w), F32)
            lg = jnp.zeros((QROWS, w), F32)
            for h in range(HEADS_PER_GROUP):
                og = jnp.where(head == h, o[QROWS * h:QROWS * (h + 1)], og)
                lg = jnp.where(head == h, lse[QROWS * h:QROWS * (h + 1)], lg)
            outs.append(og)
            lses.append(lg)
            if g == ng - 1:
                mx = jnp.maximum(jnp.maximum(lses[0], lses[1]), lses[2])
                ws = [jnp.exp(l - mx) for l in lses]
                o_ref[rows, :] = ((ws[0] * outs[0] + ws[1] * outs[1] + ws[2] * outs[2])
                                  / (ws[0] + ws[1] + ws[2])).astype(o_ref.dtype)

        return run

    return [stage(b, g) for b in range(ATT_S_BB) for g in range(ng)]


def _att_sample_consts(t_new):
    slopes = _slopes()
    rows = HEADS_PER_GROUP * QROWS
    bcs = []
    bn = np.full((3, rows, LANES), NEG, np.float32)
    for g, (window, dil) in enumerate(ATT_PATTERNS):
        bc = np.zeros((rows, window), np.float32)
        pos = np.arange(window)
        for h in range(HEADS_PER_GROUP):
            s = slopes[g * HEADS_PER_GROUP + h]
            for j in range(QROWS):
                r = QROWS * h + j
                if j >= t_new:
                    bn[g, r, 0] = 0.0
                    continue
                dist = window + j - pos
                ok = (dist % dil == 0) & (dist <= window)
                bc[r] = np.where(ok, -s * dist, NEG)
                for i_ in range(j + 1):
                    if (j - i_) % dil == 0:
                        bn[g, r, i_] = -s * (j - i_)
        bcs.append(jnp.asarray(bc))
    return bcs, jnp.asarray(bn)


def _cache_t(c):
    d, nb_, length, nh, hd = c.shape
    return jnp.transpose(c, (0, 1, 3, 4, 2)).reshape(d, nb_, nh * hd, length)


class _AttSampleHost:
    def __init__(self, qkv, kts, vts, consts, layer, seq0, nsteps):
        bb = ATT_S_BB
        w = D_ATT_OUT
        blk0 = seq0 // bb
        bcs, bn = consts
        rows = HEADS_PER_GROUP * QROWS

        def new_spec(part):
            return pl.BlockSpec((bb * QROWS, D_ATT), lambda i: (blk0 + i, part))

        def cache_spec(length):
            return pl.BlockSpec((None, bb, w, length), lambda i: (layer, blk0 + i, 0, 0))

        self.inputs = [qkv, qkv, qkv] + list(kts) + list(vts) + list(bcs) + [bn]
        self.in_specs = ([new_spec(0), new_spec(1), new_spec(2)]
                         + [cache_spec(win) for win, _ in ATT_PATTERNS] * 2
                         + [_resident((rows, win)) for win, _ in ATT_PATTERNS]
                         + [_resident((3, rows, LANES))])
        self.out_spec = pl.BlockSpec((bb * QROWS, w), lambda i: (i, 0))
        self.out_shape = jax.ShapeDtypeStruct((nsteps * bb * QROWS, w), BF16)


N_ATT_HOST_INPUTS = 3 + 3 * len(ATT_PATTERNS) + 1


MERGE_CHUNK = 256

def _merge_kernel(*refs, n_host):
    n_gate = 3 * D_MODEL // GATE_BLOCK
    cy_ref, zs_ref, at_ref = refs[:3]
    gate_refs = refs[3:3 + n_gate]
    n_in = 13 + n_gate
    (x_ref, gm_ref, shf_ref, scf_ref, wco_ref, wsg_ref, wat_ref, wo_ref, gpost_ref,
     gpre_ref) = refs[3 + n_gate:n_in]
    x1_ref, hf_ref = refs[n_in + n_host:n_in + n_host + 2]
    mg_scr = refs[-1]
    if n_host:
        for run in _att_sample_stages(*refs[n_in:n_in + n_host], refs[n_in + n_host + 2]):
            run()

    def gate(branch, lo):
        col = branch * D_MODEL + lo
        ref = gate_refs[col // GATE_BLOCK]
        return _sigmoid(ref[:, col % GATE_BLOCK:col % GATE_BLOCK + MERGE_CHUNK].astype(F32))

    cy = cy_ref[...]
    zs = zs_ref[...]
    at = at_ref[...]
    for c in range(D_MODEL // MERGE_CHUNK):
        lo = c * MERGE_CHUNK
        cols = slice(lo, lo + MERGE_CHUNK)
        a = _dot(cy, wco_ref[:, cols])
        bm = _dot(zs, wsg_ref[:, cols]) * _sigmoid(_dot(zs, wsg_ref[:, D_MODEL + lo:D_MODEL + lo + MERGE_CHUNK]))
        cc = _dot(at, wat_ref[:, cols])
        merged = gate(0, lo) * a + gate(1, lo) * bm + gate(2, lo) * cc
        mg_scr[:, cols] = merged.astype(BF16)
    mo = _dot(mg_scr[...], wo_ref[...])
    x1 = x_ref[...] + gm_ref[...] * _rms(mo, gpost_ref[...])
    x1_ref[...] = x1
    hf_ref[...] = (_rms(x1, gpre_ref[...]) * (1.0 + scf_ref[...]) + shf_ref[...]).astype(BF16)


def _merge(cy, zs, at, z2, x2, mod_arr, mod_spec, w, layer, tm, host=None):
    rows = x2.shape[0]

    def rowspec(width):
        return pl.BlockSpec((tm, width), lambda i: (i, 0))

    n_host = 0 if host is None else len(host.inputs)
    n_gate = 3 * D_MODEL // GATE_BLOCK
    gate_specs = [pl.BlockSpec((tm, GATE_BLOCK), lambda i, k=k: (i, OFF_GATES // GATE_BLOCK + k))
                  for k in range(n_gate)]
    outs = pl.pallas_call(
        functools.partial(_merge_kernel, n_host=n_host),
        grid=(rows // tm,),
        in_specs=[rowspec(D_CONV), rowspec(D_SSM), rowspec(D_ATT_OUT)] + gate_specs
        + [rowspec(D_MODEL), mod_spec(2), mod_spec(3), mod_spec(4),
                  _resident((D_CONV, D_MODEL), layer), _resident((D_SSM, 2 * D_MODEL), layer),
                  _resident((D_ATT_OUT, D_MODEL), layer), _resident((D_MODEL, D_MODEL), layer),
                  _resident((1, D_MODEL), layer), _resident((1, D_MODEL), layer)]
        + ([] if host is None else host.in_specs),
        out_specs=[rowspec(D_MODEL), rowspec(D_MODEL)] + ([] if host is None else [host.out_spec]),
        out_shape=[jax.ShapeDtypeStruct((rows, D_MODEL), F32),
                   jax.ShapeDtypeStruct((rows, D_MODEL), BF16)]
        + ([] if host is None else [host.out_shape]),
        scratch_shapes=[pltpu.VMEM((tm, D_MODEL), BF16)],
        compiler_params=_params(("parallel",)),
        name="merge",
    )(cy, zs, at, *([z2] * n_gate), x2, mod_arr, mod_arr, mod_arr, w["w_conv_out"], w["w_ssm_glu"], w["w_att"],
      w["w_out"], w["g_post_mix"], w["g_pre_ffn"], *([] if host is None else host.inputs))
    return outs


FF_CHUNK = 256


def _ffn_kernel(*refs, n_host):
    hf_ref, x1_ref, gf_ref, w1_ref, w2_ref, gpost_ref = refs[:6]
    o_ref = refs[6 + n_host]
    acc_ref = refs[-1]
    if n_host:
        for run in _att_sample_stages(*refs[6:6 + n_host], refs[7 + n_host]):
            run()
    hf = hf_ref[...]
    for c in range(D_FF // FF_CHUNK):
        lo = c * FF_CHUNK
        a = _dot(hf, w1_ref[:, lo:lo + FF_CHUNK])
        b = _dot(hf, w1_ref[:, D_FF + lo:D_FF + lo + FF_CHUNK])
        f = (_silu(a) * b).astype(BF16)
        part = _dot(f, w2_ref[lo:lo + FF_CHUNK, :])
        if c == 0:
            acc_ref[...] = part
        else:
            acc_ref[...] += part
    o_ref[...] = x1_ref[...] + gf_ref[...] * _rms(acc_ref[...], gpost_ref[...])


def _ffn(hf, x1, mod_arr, mod_spec, w, layer, tm, host=None):
    rows = x1.shape[0]
    n_host = 0 if host is None else len(host.inputs)
    return pl.pallas_call(
        functools.partial(_ffn_kernel, n_host=n_host),
        grid=(rows // tm,),
        in_specs=[pl.BlockSpec((tm, D_MODEL), lambda i: (i, 0)),
                  pl.BlockSpec((tm, D_MODEL), lambda i: (i, 0)),
                  mod_spec(5),
                  _resident((D_MODEL, 2 * D_FF), layer),
                  _resident((D_FF, D_MODEL), layer),
                  _resident((1, D_MODEL), layer)] + ([] if host is None else host.in_specs),
        out_specs=[pl.BlockSpec((tm, D_MODEL), lambda i: (i, 0))]
        + ([] if host is None else [host.out_spec]),
        out_shape=[jax.ShapeDtypeStruct((rows, D_MODEL), F32)]
        + ([] if host is None else [host.out_shape]),
        scratch_shapes=[pltpu.VMEM((tm, D_MODEL), F32)],
        compiler_params=_params(("parallel",)),
        name="ffn",
    )(hf, x1, mod_arr, w["w_ffn_in"], w["w_ffn_out"], w["g_post_ffn"],
      *([] if host is None else host.inputs))


ROW_TILE = 512


def kernel(x_prompt, x_sample, cache_k0, cache_v0, cache_k1, cache_v1, cache_k2, cache_v2, state_conv, state_ssm_re, state_ssm_im, c_prompt, c_sample, w_mod, b_mod, g_pre_mix, g_post_mix, g_pre_ffn, g_post_ffn, w_in, conv_w, conv_b, conv_ln_g, conv_ln_b, w_conv_out, ssm_a_re, ssm_a_im, ssm_log_dt, ssm_b_re, ssm_b_im, ssm_c_re, ssm_c_im, ssm_d, w_ssm_glu, w_att, w_out, w_ffn_in, w_ffn_out):
    nbp, t_p, _ = x_prompt.shape
    nbs, t_s, _ = x_sample.shape
    caches_k = (cache_k0, cache_k1, cache_k2)
    caches_v = (cache_v0, cache_v1, cache_v2)

    tm = ROW_TILE
    rows_p, rows_s = nbp * t_p, nbs * t_s
    per_seq = t_p // tm
    assert rows_s == tm

    mod_p, mod_s = _modulation(c_prompt, jnp.tile(c_sample, (t_s, 1)), w_mod, b_mod)
    mod_p = mod_p.reshape(DEPTH, nbp, 1, 6 * D_MODEL)
    abr, abi, bbr, bbi = _ssm_discretise(ssm_a_re, ssm_a_im, ssm_log_dt, ssm_b_re, ssm_b_im)

    w = dict(
        g_pre_mix=g_pre_mix[:, None], g_post_mix=g_post_mix[:, None],
        g_pre_ffn=g_pre_ffn[:, None], g_post_ffn=g_post_ffn[:, None],
        w_in=w_in.astype(BF16),
        conv_w=jnp.broadcast_to(conv_w[:, :, None, :], (DEPTH, CONV_WIDTH, SUBLANES, D_CONV)),
        conv_b=conv_b[:, None], conv_ln_g=conv_ln_g[:, None], conv_ln_b=conv_ln_b[:, None],
        ssm_d=ssm_d.reshape(DEPTH, 1, D_SSM),
        w_conv_out=w_conv_out.astype(BF16), w_ssm_glu=w_ssm_glu.astype(BF16),
        w_att=w_att.astype(BF16), w_out=w_out.astype(BF16),
        w_ffn_in=w_ffn_in.astype(BF16), w_ffn_out=w_ffn_out.astype(BF16))
    hist_s = jnp.transpose(state_conv, (0, 2, 1, 3))
    h0r_s = state_ssm_re.reshape(DEPTH, nbs, N_STATE)
    h0i_s = state_ssm_im.reshape(DEPTH, nbs, N_STATE)
    prev_p = jnp.zeros((nbp, HALO, D_CONV), F32)
    h0_p = jnp.zeros((nbp, N_STATE), F32)
    kts = [_cache_t(c) for c in caches_k]
    vts = [_cache_t(c) for c in caches_v]
    att_consts = _att_sample_consts(t_s)
    n_host_steps = rows_p // tm
    seqs_per_host = n_host_steps * ATT_S_BB
    assert 2 * seqs_per_host == nbs

    xp = x_prompt.reshape(rows_p, D_MODEL)
    xs = jnp.transpose(x_sample, (1, 0, 2)).reshape(rows_s, D_MODEL)
    prompt_states, sample_states = [], []
    kvt_p = None
    kvt_s = None
    hist_out = None
    for l in range(DEPTH):
        a8, bw, cw = _ssm_pack(abr[l], abi[l], bbr[l], bbi[l], ssm_c_re[l], ssm_c_im[l])

        def spec_p(c, l=l):
            return pl.BlockSpec((None, None, 1, D_MODEL), lambda i: (l, i // per_seq, 0, c))

        def spec_s(c, l=l):
            return pl.BlockSpec((None, tm, D_MODEL), lambda i: (l, i, c))

        z_s, kvt_s = _in_proj_sample(xs, mod_s, spec_s, w["g_pre_mix"], w["w_in"], nbs, t_s, l, kvt_s)
        cy_s, hist_out = _conv_sample(z_s.reshape(t_s, nbs, N_IN), hist_s, w["conv_w"], w["conv_b"],
                                      w["conv_ln_g"], w["conv_ln_b"], l, hist_out)
        zs_s, sre_s, sim_s = _ssm_branch(z_s, h0r_s, h0i_s, l, a8, bw, cw, w["ssm_d"], l,
                                         nb=nbs, t_len=t_s, tc=t_s, flat=True)
        qkv_s = z_s[:, OFF_Q:OFF_Q + 3 * D_ATT].astype(F32).reshape(t_s, nbs, 3 * D_ATT)
        qkv_s = jnp.pad(jnp.transpose(qkv_s, (1, 0, 2)), ((0, 0), (0, QROWS - t_s), (0, 0)))
        qkv_s = qkv_s.reshape(nbs * QROWS, 3 * D_ATT)

        z_p, kvt_p = _in_proj_prompt(xp, mod_p, spec_p, w["g_pre_mix"], w["w_in"], tm,
                                     nbp, t_p, l, kvt_p)
        z3 = z_p.reshape(nbp, t_p, N_IN)
        cy_p, conv_p = _conv_branch(z3, prev_p, w["conv_w"], w["conv_b"], w["conv_ln_g"],
                                    w["conv_ln_b"], l, bb=1, tq=512)
        zs_p, sre_p, sim_p = _ssm_branch(z3, h0_p, h0_p, None, a8, bw, cw, w["ssm_d"], l,
                                         nb=nbp, t_len=t_p, tc=128, flat=False)
        att_p = _att_prompt(z3)
        host_a = _AttSampleHost(qkv_s, kts, vts, att_consts, l, 0, n_host_steps)
        host_b = _AttSampleHost(qkv_s, kts, vts, att_consts, l, seqs_per_host, n_host_steps)
        x1_p, hf_p, att_a = _merge(cy_p.reshape(rows_p, D_CONV), zs_p.reshape(rows_p, D_SSM),
                                   att_p.reshape(rows_p, D_ATT_OUT), z_p, xp, mod_p, spec_p, w, l, tm,
                                   host=host_a)
        xp, att_b = _ffn(hf_p, x1_p, mod_p, spec_p, w, l, tm, host=host_b)

        att_s = jnp.concatenate([att_a, att_b], axis=0).reshape(nbs, QROWS, D_ATT_OUT)[:, :t_s]
        att_s = jnp.transpose(att_s, (1, 0, 2)).reshape(rows_s, D_ATT_OUT)
        x1_s, hf_s = _merge(cy_s.reshape(rows_s, D_CONV), zs_s, att_s, z_s, xs, mod_s, spec_s, w, l, tm)
        xs, = _ffn(hf_s, x1_s, mod_s, spec_s, w, l, tm)

        prompt_states.append((conv_p, sre_p.reshape(nbp, N_SSM_GROUPS, SSM_STATE),
                              sim_p.reshape(nbp, N_SSM_GROUPS, SSM_STATE)))
        sample_states.append((sre_s.reshape(nbs, N_SSM_GROUPS, SSM_STATE),
                              sim_s.reshape(nbs, N_SSM_GROUPS, SSM_STATE)))

    ps = [jnp.transpose(a.reshape(DEPTH, nbp, HEADS_PER_GROUP, HEAD_DIM, a.shape[-1]), (0, 1, 4, 2, 3))
          for a in kvt_p]
    ps += [jnp.stack([s[i] for s in prompt_states]) for i in range(3)]
    ss = [jnp.transpose(a.reshape(DEPTH, t_s, HEADS_PER_GROUP, HEAD_DIM, nbs), (0, 4, 1, 2, 3))
          for a in kvt_s]
    ss.append(jnp.transpose(hist_out, (0, 2, 1, 3)))
    ss += [jnp.stack([s[i] for s in sample_states]) for i in range(2)]
    xs = jnp.transpose(xs.reshape(t_s, nbs, D_MODEL), (1, 0, 2))
    return (xp.reshape(nbp, t_p, D_MODEL), xs, *ps, *ss)
```

```python
import functools

import numpy as np
import jax
import jax.numpy as jnp
from jax import lax
from jax.experimental import pallas as pl
from jax.experimental.pallas import tpu as pltpu

F32 = jnp.float32
BF16 = jnp.bfloat16

D_MODEL = 1024
DEPTH = 2
D_CONV = 512
CONV_WIDTH = 31
D_SSM = 512
SSM_GROUP = 16
N_SSM_GROUPS = 32
SSM_STATE = 64
HEAD_DIM = 64
HEADS_PER_GROUP = 4
ATT_PATTERNS = ((128, 1), (512, 4), (2048, 16))
N_ATT_HEADS = 12
D_ATT = 768
D_ATT_OUT = 256
ATT_SCALE = HEAD_DIM ** -0.5
D_FF = 2816
N_IN = 6912
RMS_EPS = 1e-6
LN_EPS = 1e-5
NEG = -1e30
BAND = 128

OFF_CONV = 0
OFF_SSM = 1024
OFF_Q = 1536
OFF_K = 2304
OFF_V = 3072
OFF_GATES = 3840
GATE_BLOCK = 768
IN_CHUNK = 768

LANES = 128
SUBLANES = 8
HALO = 32
N_STATE = N_SSM_GROUPS * SSM_STATE
OCTETS = 4
OCT_IN = D_SSM // OCTETS
OCT_ST = N_STATE // OCTETS
VMEM_LIMIT = 56 * 1024 * 1024


def _params(sem):
    return pltpu.CompilerParams(dimension_semantics=sem, vmem_limit_bytes=VMEM_LIMIT)


def _resident(shape, layer=None):
    if layer is None:
        return pl.BlockSpec(shape, lambda *_: (0,) * len(shape), pipeline_mode=pl.Buffered(1))
    return pl.BlockSpec((None,) + tuple(shape), lambda *_: (layer,) + (0,) * len(shape),
                        pipeline_mode=pl.Buffered(1))


def _slopes():
    return [2.0 ** (-8.0 * (i + 1) / N_ATT_HEADS) for i in range(N_ATT_HEADS)]


def _rms(x, g):
    return x * lax.rsqrt(jnp.mean(x * x, axis=-1, keepdims=True) + RMS_EPS) * g


def _sigmoid(x):
    return 0.5 * jnp.tanh(0.5 * x) + 0.5


def _silu(x):
    return x * _sigmoid(x)


def _dot(a, b):
    return jnp.dot(a, b, preferred_element_type=F32)


def _dot_nt(a, b):
    return lax.dot_general(a, b, (((1,), (1,)), ((), ())), preferred_element_type=F32)


def _mod_kernel(cp_ref, cs_ref, w_ref, b_ref, op_ref, os_ref):
    w = w_ref[...].astype(BF16)
    op_ref[...] = _dot(_silu(cp_ref[...]).astype(BF16), w) + b_ref[...]
    os_ref[...] = _dot(_silu(cs_ref[...]).astype(BF16), w) + b_ref[...]


def _modulation(c_p, c_s_rows, w_mod, b_mod):
    np_, ns = c_p.shape[0], c_s_rows.shape[0]
    tn = 2048
    return pl.pallas_call(
        _mod_kernel,
        grid=(DEPTH, 6 * D_MODEL // tn),
        in_specs=[pl.BlockSpec((np_, D_MODEL), lambda l, j: (0, 0)),
                  pl.BlockSpec((ns, D_MODEL), lambda l, j: (0, 0)),
                  pl.BlockSpec((None, D_MODEL, tn), lambda l, j: (l, 0, j)),
                  pl.BlockSpec((None, 1, tn), lambda l, j: (l, 0, j))],
        out_specs=[pl.BlockSpec((None, np_, tn), lambda l, j: (l, 0, j)),
                   pl.BlockSpec((None, ns, tn), lambda l, j: (l, 0, j))],
        out_shape=[jax.ShapeDtypeStruct((DEPTH, np_, 6 * D_MODEL), F32),
                   jax.ShapeDtypeStruct((DEPTH, ns, 6 * D_MODEL), F32)],
        compiler_params=_params(("parallel", "parallel")),
        name="modulation",
    )(c_p, c_s_rows, w_mod, b_mod.reshape(DEPTH, 1, 6 * D_MODEL))


def _ssm_disc_kernel(ar_ref, ai_ref, ldt_ref, br_ref, bi_ref, abr_ref, abi_ref, bbr_ref, bbi_ref):
    ar = ar_ref[...]
    ai = ai_ref[...]
    dt = jnp.exp(ldt_ref[...])
    mag = jnp.exp(dt * ar)
    abr = mag * jnp.cos(dt * ai)
    abi = mag * jnp.sin(dt * ai)
    den = ar * ar + ai * ai
    fr = ((abr - 1.0) * ar + abi * ai) / den
    fi = (abi * ar - (abr - 1.0) * ai) / den
    br = br_ref[...]
    bi = bi_ref[...]
    abr_ref[...] = abr
    abi_ref[...] = abi
    bbr_ref[...] = fr * br - fi * bi
    bbi_ref[...] = fr * bi + fi * br


def _ssm_discretise(a_re, a_im, log_dt, b_re, b_im):
    ar = a_re.reshape(DEPTH, 1, N_STATE)
    ai = a_im.reshape(DEPTH, 1, N_STATE)
    ldt = jnp.repeat(log_dt, SSM_STATE, axis=-1).reshape(DEPTH, 1, N_STATE)
    br = jnp.transpose(b_re, (0, 3, 1, 2)).reshape(DEPTH, SSM_GROUP, N_STATE)
    bi = jnp.transpose(b_im, (0, 3, 1, 2)).reshape(DEPTH, SSM_GROUP, N_STATE)
    row = pl.BlockSpec((None, 1, N_STATE), lambda l: (l, 0, 0))
    mat = pl.BlockSpec((None, SSM_GROUP, N_STATE), lambda l: (l, 0, 0))
    return pl.pallas_call(
        _ssm_disc_kernel,
        grid=(DEPTH,),
        in_specs=[row, row, row, mat, mat],
        out_specs=[row, row, mat, mat],
        out_shape=[jax.ShapeDtypeStruct((DEPTH, 1, N_STATE), F32)] * 2
        + [jax.ShapeDtypeStruct((DEPTH, SSM_GROUP, N_STATE), F32)] * 2,
        compiler_params=_params(("parallel",)),
        name="ssm_discretise",
    )(ar, ai, ldt, br, bi)


def _ssm_pack(abr, abi, bbr, bbi, c_re, c_im):
    gpo = N_SSM_GROUPS // OCTETS
    eye = jnp.eye(gpo, dtype=F32)

    def oct_state(v):
        return v.reshape(OCTETS, OCT_ST)

    a8 = jnp.concatenate([oct_state(abr), oct_state(abi)], axis=1).reshape(1, 2 * N_STATE)
    a8 = jnp.broadcast_to(a8, (SUBLANES, 2 * N_STATE))

    def b_tiles(bb):
        t = bb.reshape(SSM_GROUP, OCTETS, gpo, SSM_STATE)
        t = jnp.transpose(t, (1, 2, 0, 3))
        t = t[:, :, :, None, :] * eye[None, :, None, :, None]
        return t.reshape(OCTETS, OCT_IN, OCT_ST)

    bw = jnp.concatenate([b_tiles(bbr), b_tiles(bbi)], axis=2).astype(BF16)

    def c_tiles(cc):
        t = cc.reshape(OCTETS, gpo, SSM_GROUP, SSM_STATE)
        t = jnp.transpose(t, (0, 1, 3, 2))
        t = t[:, :, :, None, :] * eye[None, :, None, :, None]
        return t.reshape(OCTETS, OCT_ST, OCT_IN)

    cw = jnp.concatenate([c_tiles(c_re), -c_tiles(c_im)], axis=1).astype(BF16)
    return a8, bw, cw


def _is_kv_chunk(lo):
    return OFF_K <= lo < OFF_K + 2 * D_ATT


def _in_proj_sample_kernel(*refs, nb, t_len, layer, n_alias):
    x_ref, sh_ref, sc_ref, g_ref, w_ref = refs[:5]
    z_ref = refs[5 + n_alias]
    kvt_refs = refs[6 + n_alias:]
    y = _rms(x_ref[...], g_ref[...])
    h = (y * (1.0 + sc_ref[...]) + sh_ref[...]).astype(BF16)
    for c in range(N_IN // IN_CHUNK):
        lo = c * IN_CHUNK
        r = _dot(h, w_ref[:, lo:lo + IN_CHUNK])
        z_ref[:, lo:lo + IN_CHUNK] = r.astype(BF16)
        if _is_kv_chunk(lo):
            which = (lo - OFF_K) // D_ATT
            for g in range(len(ATT_PATTERNS)):
                for t in range(t_len):
                    tile = r[t * nb:(t + 1) * nb, g * D_ATT_OUT:(g + 1) * D_ATT_OUT].T
                    if n_alias:
                        kvt_refs[2 * g + which][t] = tile
                    else:
                        for l in range(DEPTH):
                            kvt_refs[2 * g + which][l, t] = tile if l == layer else jnp.zeros_like(tile)


def _in_proj_sample(x2, mod_arr, mod_spec, g_pre, w_bf, nb, t_len, layer, prev_kvt):
    rows = x2.shape[0]
    n_alias = 0 if prev_kvt is None else len(prev_kvt)
    kern = functools.partial(_in_proj_sample_kernel, nb=nb, t_len=t_len, layer=layer, n_alias=n_alias)
    n_kv = 2 * len(ATT_PATTERNS)
    if n_alias:
        kv_spec = pl.BlockSpec((None, t_len, D_ATT_OUT, nb), lambda i: (layer, 0, 0, 0))
    else:
        kv_spec = pl.BlockSpec((DEPTH, t_len, D_ATT_OUT, nb), lambda i: (0, 0, 0, 0))
    outs = pl.pallas_call(
        kern,
        grid=(1,),
        in_specs=[pl.BlockSpec((rows, D_MODEL), lambda i: (0, 0)),
                  mod_spec(0), mod_spec(1),
                  _resident((1, D_MODEL), layer),
                  _resident((D_MODEL, N_IN), layer)] + [pl.BlockSpec(memory_space=pl.ANY)] * n_alias,
        out_specs=[pl.BlockSpec((rows, N_IN), lambda i: (0, 0))]
        + [kv_spec] * n_kv,
        out_shape=[jax.ShapeDtypeStruct((rows, N_IN), BF16)]
        + [jax.ShapeDtypeStruct((DEPTH, t_len, D_ATT_OUT, nb), F32)] * n_kv,
        input_output_aliases={5 + k: 1 + k for k in range(n_alias)},
        compiler_params=_params(("arbitrary",)),
        name="in_proj_sample",
    )(x2, mod_arr, mod_arr, g_pre, w_bf, *(prev_kvt or ()))
    return outs[0], list(outs[1:])


def _in_proj_prompt_kernel(*refs, keeps, tm, layer, n_alias):
    x_ref, sh_ref, sc_ref, g_ref, w_ref = refs[:5]
    z_ref = refs[5 + n_alias]
    kvt_refs = refs[6 + n_alias:]
    y = _rms(x_ref[...], g_ref[...])
    h = (y * (1.0 + sc_ref[...]) + sh_ref[...]).astype(BF16)
    for c in range(N_IN // IN_CHUNK):
        lo = c * IN_CHUNK
        r = _dot(h, w_ref[:, lo:lo + IN_CHUNK])
        z_ref[:, lo:lo + IN_CHUNK] = r.astype(BF16)
        if _is_kv_chunk(lo):
            which = (lo - OFF_K) // D_ATT
            for g, keep in enumerate(keeps):
                kept = min(keep, tm)
                tile = r[tm - kept:, g * D_ATT_OUT:(g + 1) * D_ATT_OUT].T
                if n_alias:
                    kvt_refs[2 * g + which][...] = tile
                else:
                    for l in range(DEPTH):
                        kvt_refs[2 * g + which][l] = tile if l == layer else jnp.zeros_like(tile)


def _in_proj_prompt(x2, mod_arr, mod_spec, g_pre, w_bf, tm, nbatch, t_len, layer, prev_kvt):
    rows = x2.shape[0]
    per_seq = t_len // tm
    keeps = tuple(min(window, t_len) for window, _ in ATT_PATTERNS)
    n_alias = 0 if prev_kvt is None else len(prev_kvt)
    kern = functools.partial(_in_proj_prompt_kernel, keeps=keeps, tm=tm, layer=layer, n_alias=n_alias)
    kvt_specs, kvt_shapes = [], []
    for keep in keeps:
        first = per_seq - max(keep // tm, 1)
        kept = min(keep, tm)
        for _ in range(2):
            kvt_specs.append(pl.BlockSpec(
                (None if n_alias else DEPTH, None, D_ATT_OUT, kept),
                lambda i, first=first: (layer if n_alias else 0, i // per_seq, 0,
                                        jnp.maximum(i % per_seq - first, 0))))
            kvt_shapes.append(jax.ShapeDtypeStruct((DEPTH, nbatch, D_ATT_OUT, keep), F32))
    outs = pl.pallas_call(
        kern,
        grid=(rows // tm,),
        in_specs=[pl.BlockSpec((tm, D_MODEL), lambda i: (i, 0)),
                  mod_spec(0), mod_spec(1),
                  _resident((1, D_MODEL), layer),
                  _resident((D_MODEL, N_IN), layer)] + [pl.BlockSpec(memory_space=pl.ANY)] * n_alias,
        out_specs=[pl.BlockSpec((tm, N_IN), lambda i: (i, 0))] + kvt_specs,
        out_shape=[jax.ShapeDtypeStruct((rows, N_IN), BF16)] + kvt_shapes,
        input_output_aliases={5 + k: 1 + k for k in range(n_alias)},
        compiler_params=_params(("arbitrary",)),
        name="in_proj_prompt",
    )(x2, mod_arr, mod_arr, g_pre, w_bf, *(prev_kvt or ()))
    return outs[0], list(outs[1:])


def _conv_kernel(a_ref, prev_ref, w_ref, b_ref, lg_ref, lb_ref, y_ref, st_ref, ubuf, sh_scr,
                 *, bb, tq, nt, sub):
    i = pl.program_id(1)
    base = HALO - (CONV_WIDTH - 1)
    span = tq + HALO - SUBLANES
    nrt = max(sub // SUBLANES, 1)
    rpt = min(sub, SUBLANES)
    for b in range(bb):
        @pl.when(i == 0)
        def _():
            ubuf[b, 0:HALO, :] = prev_ref[b]

        @pl.when(i > 0)
        def _():
            ubuf[b, 0:HALO, :] = ubuf[b, tq:tq + HALO, :]

        a = a_ref[b].astype(F32)
        ubuf[b, HALO:HALO + tq, :] = a[:, :D_CONV] * _sigmoid(a[:, D_CONV:])
        for r in range(1, SUBLANES):
            sh_scr[r - 1, 0:span, :] = ubuf[b, r:r + span, :]
        for r0 in range(0, tq, sub):
            acc = jnp.zeros((nrt, rpt, D_CONV), F32) + b_ref[...]
            for j in range(CONV_WIDTH):
                off = base + j
                al = r0 + (off // SUBLANES) * SUBLANES
                if off % SUBLANES == 0:
                    src = ubuf[b, al:al + sub, :]
                else:
                    src = sh_scr[off % SUBLANES - 1, al:al + sub, :]
                acc = acc + w_ref[j, 0:rpt, :][None] * src.reshape(nrt, rpt, D_CONV)
            acc = acc.reshape(sub, D_CONV)
            mu = jnp.mean(acc, axis=-1, keepdims=True)
            xc = acc - mu
            yn = xc * lax.rsqrt(jnp.mean(xc * xc, axis=-1, keepdims=True) + LN_EPS)
            y_ref[b, r0:r0 + sub, :] = _silu(yn * lg_ref[...] + lb_ref[...]).astype(y_ref.dtype)

        @pl.when(i == nt - 1)
        def _():
            st_ref[b] = ubuf[b, base + tq:base + tq + CONV_WIDTH - 1, :]


def _conv_branch(z3, prev_pad, w_b8, b, lg, lb, layer, bb, tq):
    nbatch, t_len, _ = z3.shape
    nt = t_len // tq
    sub = min(tq, 32)
    kern = functools.partial(_conv_kernel, bb=bb, tq=tq, nt=nt, sub=sub)
    col = OFF_CONV // (2 * D_CONV)
    return pl.pallas_call(
        kern,
        grid=(nbatch // bb, nt),
        in_specs=[pl.BlockSpec((bb, tq, 2 * D_CONV), lambda i, j: (i, j, col)),
                  pl.BlockSpec((bb, HALO, D_CONV), lambda i, j: (i, 0, 0)),
                  _resident((CONV_WIDTH, SUBLANES, D_CONV), layer),
                  _resident((1, D_CONV), layer), _resident((1, D_CONV), layer),
                  _resident((1, D_CONV), layer)],
        out_specs=[pl.BlockSpec((bb, tq, D_CONV), lambda i, j: (i, j, 0)),
                   pl.BlockSpec((bb, CONV_WIDTH - 1, D_CONV), lambda i, j: (i, 0, 0))],
        out_shape=[jax.ShapeDtypeStruct((nbatch, t_len, D_CONV), BF16),
                   jax.ShapeDtypeStruct((nbatch, CONV_WIDTH - 1, D_CONV), F32)],
        scratch_shapes=[pltpu.VMEM((bb, HALO + tq + SUBLANES, D_CONV), F32),
                        pltpu.VMEM((SUBLANES - 1, HALO + tq, D_CONV), F32)],
        compiler_params=_params(("parallel", "arbitrary")),
        name="conv_branch",
    )(z3, prev_pad, w_b8, b, lg, lb)


def _conv_sample_kernel(*refs, t_len, nbb, layer, n_alias):
    a_ref, prev_ref, w_ref, b_ref, lg_ref, lb_ref = refs[:6]
    y_ref, st_ref = refs[6 + n_alias:]
    hist = CONV_WIDTH - 1
    if not n_alias:
        for l in range(DEPTH):
            if l != layer:
                st_ref[l] = jnp.zeros((hist, nbb, D_CONV), F32)
        st_ref = st_ref.at[layer]
    for k in range(hist - t_len):
        st_ref[k] = prev_ref[k + t_len]
    for t in range(t_len):
        a = a_ref[t].astype(F32)
        st_ref[hist - t_len + t] = a[:, :D_CONV] * _sigmoid(a[:, D_CONV:])
    nrt = nbb // SUBLANES
    for t in range(t_len):
        acc = jnp.zeros((nrt, SUBLANES, D_CONV), F32) + b_ref[...]
        for j in range(CONV_WIDTH):
            k = t + j
            src = prev_ref[k] if k < hist else st_ref[k - t_len]
            acc = acc + w_ref[j][None] * src.reshape(nrt, SUBLANES, D_CONV)
        acc = acc.reshape(nbb, D_CONV)
        mu = jnp.mean(acc, axis=-1, keepdims=True)
        xc = acc - mu
        yn = xc * lax.rsqrt(jnp.mean(xc * xc, axis=-1, keepdims=True) + LN_EPS)
        y_ref[t] = _silu(yn * lg_ref[...] + lb_ref[...]).astype(y_ref.dtype)


def _conv_sample(z_t, state_t, w_b8, b, lg, lb, layer, prev_out):
    t_len, nb, _ = z_t.shape
    nbb = 32
    hist = CONV_WIDTH - 1
    n_alias = 0 if prev_out is None else 1
    kern = functools.partial(_conv_sample_kernel, t_len=t_len, nbb=nbb, layer=layer, n_alias=n_alias)
    st_spec = pl.BlockSpec((None, hist, nbb, D_CONV), lambda i: (layer, 0, i, 0))
    if n_alias:
        st_out_spec = st_spec
    else:
        st_out_spec = pl.BlockSpec((DEPTH, hist, nbb, D_CONV), lambda i: (0, 0, i, 0))
    return pl.pallas_call(
        kern,
        grid=(nb // nbb,),
        in_specs=[pl.BlockSpec((t_len, nbb, 2 * D_CONV), lambda i: (0, i, OFF_CONV // (2 * D_CONV))),
                  st_spec,
                  _resident((CONV_WIDTH, SUBLANES, D_CONV), layer),
                  _resident((1, D_CONV), layer), _resident((1, D_CONV), layer),
                  _resident((1, D_CONV), layer)] + [pl.BlockSpec(memory_space=pl.ANY)] * n_alias,
        out_specs=[pl.BlockSpec((t_len, nbb, D_CONV), lambda i: (0, i, 0)), st_out_spec],
        out_shape=[jax.ShapeDtypeStruct((t_len, nb, D_CONV), BF16),
                   jax.ShapeDtypeStruct((DEPTH, hist, nb, D_CONV), F32)],
        input_output_aliases={6: 1} if n_alias else {},
        compiler_params=_params(("parallel",)),
        name="conv_sample",
    )(z_t, state_t, w_b8, b, lg, lb, *(() if prev_out is None else (prev_out,)))


def _ssm_kernel(u_ref, h0r_ref, h0i_ref, a_ref, bw_ref, cw_ref, d_ref, y_ref, hr_ref, hi_ref,
                ub_scr, u_scr, s_scr, y_scr, h_scr, *, nb, tc, nt, flat):
    i = pl.program_id(0)
    st2 = 2 * OCT_ST

    @pl.when(i == 0)
    def _():
        for o in range(OCTETS):
            h_scr[:, st2 * o:st2 * o + OCT_ST] = h0r_ref[:, OCT_ST * o:OCT_ST * (o + 1)]
            h_scr[:, st2 * o + OCT_ST:st2 * (o + 1)] = h0i_ref[:, OCT_ST * o:OCT_ST * (o + 1)]

    if flat:
        u_scr[...] = u_ref[...].astype(F32)
    else:
        for o in range(OCTETS):
            cols = slice(OCT_IN * o, OCT_IN * (o + 1))
            for b in range(nb):
                ub_scr[o, b * tc:(b + 1) * tc, :] = u_ref[b, :, cols].astype(F32)
            for t in range(tc):
                u_scr[t * nb:(t + 1) * nb, cols] = ub_scr[o, pl.ds(t, nb, stride=tc), :]

    for o in range(OCTETS):
        ub = u_scr[:, OCT_IN * o:OCT_IN * (o + 1)].astype(BF16)
        s_scr[:, st2 * o:st2 * (o + 1)] = _dot(ub, bw_ref[o])

    for o in range(OCTETS):
        c0 = st2 * o
        re = slice(c0, c0 + OCT_ST)
        im = slice(c0 + OCT_ST, c0 + st2)
        ar = a_ref[:, re]
        ai = a_ref[:, im]

        def run(r0, hr, hi, re=re, im=im, ar=ar, ai=ai):
            for t in range(tc):
                row = r0 + t * nb
                if isinstance(row, int):
                    rows = slice(row, row + SUBLANES)
                else:
                    rows = pl.ds(pl.multiple_of(row, SUBLANES), SUBLANES)
                nr = ar * hr - ai * hi + s_scr[rows, re]
                ni = ar * hi + ai * hr + s_scr[rows, im]
                s_scr[rows, re] = nr
                s_scr[rows, im] = ni
                hr, hi = nr, ni
            return hr, hi

        if nb == SUBLANES:
            hr, hi = run(0, h_scr[:, re], h_scr[:, im])
            h_scr[:, re] = hr
            h_scr[:, im] = hi
        else:
            def rows_body(rg, carry, re=re, im=im, run=run):
                r0 = pl.multiple_of(rg * SUBLANES, SUBLANES)
                hr, hi = run(r0, h_scr[pl.ds(r0, SUBLANES), re], h_scr[pl.ds(r0, SUBLANES), im])
                h_scr[pl.ds(r0, SUBLANES), re] = hr
                h_scr[pl.ds(r0, SUBLANES), im] = hi
                return carry

            lax.fori_loop(0, nb // SUBLANES, rows_body, 0)

    for o in range(OCTETS):
        hb = s_scr[:, st2 * o:st2 * (o + 1)].astype(BF16)
        y_scr[:, OCT_IN * o:OCT_IN * (o + 1)] = _dot(hb, cw_ref[o])

    yv = jax.nn.gelu(y_scr[...] + d_ref[...] * u_scr[...])
    if flat:
        y_ref[...] = yv.astype(y_ref.dtype)
    else:
        y_scr[...] = yv
        for o in range(OCTETS):
            cols = slice(OCT_IN * o, OCT_IN * (o + 1))
            for t in range(tc):
                ub_scr[o, pl.ds(t, nb, stride=tc), :] = y_scr[t * nb:(t + 1) * nb, cols]
            for b in range(nb):
                y_ref[b, :, cols] = ub_scr[o, b * tc:(b + 1) * tc, :].astype(y_ref.dtype)

    @pl.when(i == nt - 1)
    def _():
        for o in range(OCTETS):
            hr_ref[:, OCT_ST * o:OCT_ST * (o + 1)] = h_scr[:, st2 * o:st2 * o + OCT_ST]
            hi_ref[:, OCT_ST * o:OCT_ST * (o + 1)] = h_scr[:, st2 * o + OCT_ST:st2 * (o + 1)]


def _ssm_branch(z, h0r, h0i, h0_layer, a8, bw, cw, dvec, layer, nb, t_len, tc, flat):
    nt = t_len // tc
    m = nb * tc
    kern = functools.partial(_ssm_kernel, nb=nb, tc=tc, nt=nt, flat=flat)
    if flat:
        u_spec = pl.BlockSpec((m, D_SSM), lambda i: (0, OFF_SSM // D_SSM))
        y_spec = pl.BlockSpec((m, D_SSM), lambda i: (0, 0))
        y_shape = jax.ShapeDtypeStruct((m, D_SSM), BF16)
    else:
        u_spec = pl.BlockSpec((nb, tc, D_SSM), lambda i: (0, i, OFF_SSM // D_SSM))
        y_spec = pl.BlockSpec((nb, tc, D_SSM), lambda i: (0, i, 0))
        y_shape = jax.ShapeDtypeStruct((nb, t_len, D_SSM), BF16)
    st_spec = pl.BlockSpec((nb, N_STATE), lambda i: (0, 0))
    if h0_layer is None:
        h0_spec = st_spec
    else:
        h0_spec = pl.BlockSpec((None, nb, N_STATE), lambda i: (h0_layer, 0, 0))
    return pl.pallas_call(
        kern,
        grid=(nt,),
        in_specs=[u_spec, h0_spec, h0_spec,
                  _resident((SUBLANES, 2 * N_STATE)),
                  _resident((OCTETS, OCT_IN, 2 * OCT_ST)),
                  _resident((OCTETS, 2 * OCT_ST, OCT_IN)),
                  _resident((1, D_SSM), layer)],
        out_specs=[y_spec, st_spec, st_spec],
        out_shape=[y_shape, jax.ShapeDtypeStruct((nb, N_STATE), F32),
                   jax.ShapeDtypeStruct((nb, N_STATE), F32)],
        scratch_shapes=[pltpu.VMEM((OCTETS, m, OCT_IN), F32), pltpu.VMEM((m, D_SSM), F32),
                        pltpu.VMEM((m, 2 * N_STATE), F32), pltpu.VMEM((m, D_SSM), F32),
                        pltpu.VMEM((nb, 2 * N_STATE), F32)],
        compiler_params=_params(("arbitrary",)),
        name="ssm_branch",
    )(z, h0r, h0i, a8, bw, cw, dvec)


def _att_prompt_kernel(q0_ref, q1_ref, k0_ref, k1_ref, v0_ref, v1_ref, o_ref,
                       qkv_scr, o_scr, l_scr, bp_scr, *, t_len):
    g_id = pl.program_id(1)
    slopes = _slopes()
    lane = lax.broadcasted_iota(jnp.int32, (BAND, LANES), 1)
    low = lane < HEAD_DIM

    in_refs = (q0_ref, q1_ref, k0_ref, k1_ref, v0_ref, v1_ref)

    def group(g, dil):
        nblk = t_len // dil // BAND
        if dil > 1:
            for n, r in enumerate(in_refs):
                qkv_scr[n] = r[...].astype(F32)
        qi = lax.broadcasted_iota(jnp.int32, (BAND, 2 * BAND), 0)
        kj = lax.broadcasted_iota(jnp.int32, (BAND, 2 * BAND), 1)
        dist = qi + BAND - kj
        valid = (dist >= 0) & (dist <= BAND)
        valid0 = valid & (kj >= BAND)
        distf = dist.astype(F32)
        for h in range(HEADS_PER_GROUP):
            bias = (-slopes[g * HEADS_PER_GROUP + h] * dil) * distf
            bp_scr[0, h] = jnp.where(valid0, bias, NEG)
            bp_scr[1, h] = jnp.where(valid, bias, NEG)

        def rows(start):
            if dil == 1:
                return pl.ds(start, BAND)
            return pl.ds(start, BAND, stride=dil)

        def block(blk, carry):
            r = blk // nblk
            n = blk % nblk
            start_c = r + dil * BAND * n
            start_p = r + dil * BAND * jnp.maximum(n - 1, 0)
            sel = jnp.minimum(n, 1)
            for pair in range(2):
                if dil == 1:
                    def take(n, start):
                        return in_refs[n][pl.ds(pl.multiple_of(start, BAND), BAND), :]
                else:
                    def take(n, start):
                        return qkv_scr[n, rows(start), :]
                qp = take(pair, start_c) * ATT_SCALE
                kp = jnp.concatenate([take(2 + pair, start_p), take(2 + pair, start_c)], axis=0).astype(BF16)
                vp = jnp.concatenate([take(4 + pair, start_p), take(4 + pair, start_c)], axis=0).astype(BF16)
                o_h, l_h = [], []
                for hh in range(2):
                    h = 2 * pair + hh
                    qm = jnp.where(low if hh == 0 else jnp.logical_not(low), qp, 0.0).astype(BF16)
                    t = _dot_nt(qm, kp) + bp_scr[sel, h]
                    mx = jnp.max(t, axis=-1, keepdims=True)
                    p = jnp.exp(t - mx)
                    den = jnp.sum(p, axis=-1, keepdims=True)
                    pv = _dot(p.astype(BF16), vp)
                    o_h.append(pv / den)
                    l_h.append(jnp.broadcast_to(mx + jnp.log(den), (BAND, LANES)))
                o_scr[2 * g + pair, rows(start_c), :] = jnp.where(low, o_h[0], o_h[1])
                l_scr[2 * g + pair, rows(start_c), :] = jnp.where(low, l_h[0], l_h[1])
            return carry

        lax.fori_loop(0, dil * nblk, block, 0, unroll=4 if dil > 4 else 8)

    for g, (window, dil) in enumerate(ATT_PATTERNS):
        @pl.when(g_id == g)
        def _(g=g, dil=dil):
            group(g, dil)

    @pl.when(g_id == len(ATT_PATTERNS) - 1)
    def _():
        tr = 256

        def comb(i, carry):
            r0 = pl.multiple_of(i * tr, tr)
            for pair in range(2):
                l0 = l_scr[pair, pl.ds(r0, tr), :]
                l1 = l_scr[2 + pair, pl.ds(r0, tr), :]
                l2 = l_scr[4 + pair, pl.ds(r0, tr), :]
                mx = jnp.maximum(jnp.maximum(l0, l1), l2)
                w0 = jnp.exp(l0 - mx)
                w1 = jnp.exp(l1 - mx)
                w2 = jnp.exp(l2 - mx)
                num = (w0 * o_scr[pair, pl.ds(r0, tr), :] + w1 * o_scr[2 + pair, pl.ds(r0, tr), :]
                       + w2 * o_scr[4 + pair, pl.ds(r0, tr), :])
                o_ref[pl.ds(r0, tr), LANES * pair:LANES * (pair + 1)] = (
                    num / (w0 + w1 + w2)).astype(o_ref.dtype)
            return carry

        lax.fori_loop(0, t_len // tr, comb, 0)


def _att_prompt(z3):
    nbatch, t_len, _ = z3.shape
    ng = len(ATT_PATTERNS)
    kern = functools.partial(_att_prompt_kernel, t_len=t_len)

    def spec(off, pair):
        return pl.BlockSpec((None, t_len, LANES), lambda b, g: (b, 0, off // LANES + 2 * g + pair))

    return pl.pallas_call(
        kern,
        grid=(nbatch, ng),
        in_specs=[spec(off, pair) for off in (OFF_Q, OFF_K, OFF_V) for pair in range(2)],
        out_specs=pl.BlockSpec((None, t_len, D_ATT_OUT), lambda b, g: (b, 0, 0)),
        out_shape=jax.ShapeDtypeStruct((nbatch, t_len, D_ATT_OUT), BF16),
        scratch_shapes=[pltpu.VMEM((6, t_len, LANES), F32),
                        pltpu.VMEM((2 * ng, t_len, LANES), F32),
                        pltpu.VMEM((2 * ng, t_len, LANES), F32),
                        pltpu.VMEM((2, HEADS_PER_GROUP, BAND, 2 * BAND), F32)],
        compiler_params=_params(("parallel", "arbitrary")),
        name="att_prompt",
    )(z3, z3, z3, z3, z3, z3)


QROWS = 8
ATT_S_BB = 2


def _att_sample_stages(q_ref, kn_ref, vn_ref, k0_ref, k1_ref, k2_ref, v0_ref, v1_ref, v2_ref,
                       b0_ref, b1_ref, b2_ref, bn_ref, o_ref):
    kc_refs = (k0_ref, k1_ref, k2_ref)
    vc_refs = (v0_ref, v1_ref, v2_ref)
    bc_refs = (b0_ref, b1_ref, b2_ref)
    w = D_ATT_OUT
    ng = len(ATT_PATTERNS)
    state = {}

    def stage(b, g):
        def run():
            head = lax.broadcasted_iota(jnp.int32, (QROWS, w), 1) // HEAD_DIM
            zpad = jnp.zeros((LANES - QROWS, w), BF16)
            rows = slice(QROWS * b, QROWS * (b + 1))
            outs, lses = state.setdefault(b, ([], []))
            cols = slice(g * w, (g + 1) * w)
            qg = q_ref[rows, cols] * ATT_SCALE
            qblk = jnp.concatenate([jnp.where(head == h, qg, 0.0) for h in range(HEADS_PER_GROUP)],
                                   axis=0).astype(BF16)
            s_c = _dot(qblk, kc_refs[g][b].astype(BF16))
            bias_c = bc_refs[g][...]
            t_c = jnp.where(bias_c > 0.5 * NEG, s_c + bias_c, NEG)
            kn = jnp.concatenate([kn_ref[rows, cols].astype(BF16), zpad], axis=0)
            vn = jnp.concatenate([vn_ref[rows, cols].astype(BF16), zpad], axis=0)
            bias_n = bn_ref[g]
            t_n = jnp.where(bias_n > 0.5 * NEG, _dot_nt(qblk, kn) + bias_n, NEG)
            mx = jnp.maximum(jnp.max(t_c, axis=-1, keepdims=True), jnp.max(t_n, axis=-1, keepdims=True))
            p_c = jnp.exp(t_c - mx)
            p_n = jnp.exp(t_n - mx)
            den = jnp.sum(p_c, axis=-1, keepdims=True) + jnp.sum(p_n, axis=-1, keepdims=True)
            o = _dot_nt(p_c.astype(BF16), vc_refs[g][b].astype(BF16))
            o = (o + _dot(p_n.astype(BF16), vn)) / den
            lse = jnp.broadcast_to(mx + jnp.log(den), o.shape)
            og = jnp.zeros((QROWS, w), F32)
            lg = jnp.zeros((QROWS, w), F32)
            for h in range(HEADS_PER_GROUP):
                og = jnp.where(head == h, o[QROWS * h:QROWS * (h + 1)], og)
                lg = jnp.where(head == h, lse[QROWS * h:QROWS * (h + 1)], lg)
            outs.append(og)
            lses.append(lg)
            if g == ng - 1:
                mx = jnp.maximum(jnp.maximum(lses[0], lses[1]), lses[2])
                ws = [jnp.exp(l - mx) for l in lses]
                o_ref[rows, :] = ((ws[0] * outs[0] + ws[1] * outs[1] + ws[2] * outs[2])
                                  / (ws[0] + ws[1] + ws[2])).astype(o_ref.dtype)

        return run

    return [stage(b, g) for b in range(ATT_S_BB) for g in range(ng)]


def _att_sample_consts(t_new):
    slopes = _slopes()
    rows = HEADS_PER_GROUP * QROWS
    bcs = []
    bn = np.full((3, rows, LANES), NEG, np.float32)
    for g, (window, dil) in enumerate(ATT_PATTERNS):
        bc = np.zeros((rows, window), np.float32)
        pos = np.arange(window)
        for h in range(HEADS_PER_GROUP):
            s = slopes[g * HEADS_PER_GROUP + h]
            for j in range(QROWS):
                r = QROWS * h + j
                if j >= t_new:
                    bn[g, r, 0] = 0.0
                    continue
                dist = window + j - pos
                ok = (dist % dil == 0) & (dist <= window)
                bc[r] = np.where(ok, -s * dist, NEG)
                for i_ in range(j + 1):
                    if (j - i_) % dil == 0:
                        bn[g, r, i_] = -s * (j - i_)
        bcs.append(jnp.asarray(bc))
    return bcs, jnp.asarray(bn)


def _cache_t(c):
    d, nb_, length, nh, hd = c.shape
    return jnp.transpose(c, (0, 1, 3, 4, 2)).reshape(d, nb_, nh * hd, length)


class _AttSampleHost:
    def __init__(self, qkv, kts, vts, consts, layer, seq0, nsteps):
        bb = ATT_S_BB
        w = D_ATT_OUT
        blk0 = seq0 // bb
        bcs, bn = consts
        rows = HEADS_PER_GROUP * QROWS

        def new_spec(part):
            return pl.BlockSpec((bb * QROWS, D_ATT), lambda i: (blk0 + i, part))

        def cache_spec(length):
            return pl.BlockSpec((None, bb, w, length), lambda i: (layer, blk0 + i, 0, 0))

        self.inputs = [qkv, qkv, qkv] + list(kts) + list(vts) + list(bcs) + [bn]
        self.in_specs = ([new_spec(0), new_spec(1), new_spec(2)]
                         + [cache_spec(win) for win, _ in ATT_PATTERNS] * 2
                         + [_resident((rows, win)) for win, _ in ATT_PATTERNS]
                         + [_resident((3, rows, LANES))])
        self.out_spec = pl.BlockSpec((bb * QROWS, w), lambda i: (i, 0))
        self.out_shape = jax.ShapeDtypeStruct((nsteps * bb * QROWS, w), BF16)


N_ATT_HOST_INPUTS = 3 + 3 * len(ATT_PATTERNS) + 1


MERGE_CHUNK = 256

def _merge_kernel(*refs, n_host):
    n_gate = 3 * D_MODEL // GATE_BLOCK
    cy_ref, zs_ref, at_ref = refs[:3]
    gate_refs = refs[3:3 + n_gate]
    n_in = 13 + n_gate
    (x_ref, gm_ref, shf_ref, scf_ref, wco_ref, wsg_ref, wat_ref, wo_ref, gpost_ref,
     gpre_ref) = refs[3 + n_gate:n_in]
    x1_ref, hf_ref = refs[n_in + n_host:n_in + n_host + 2]
    mg_scr = refs[-1]
    if n_host:
        for run in _att_sample_stages(*refs[n_in:n_in + n_host], refs[n_in + n_host + 2]):
            run()

    def gate(branch, lo):
        col = branch * D_MODEL + lo
        ref = gate_refs[col // GATE_BLOCK]
        return _sigmoid(ref[:, col % GATE_BLOCK:col % GATE_BLOCK + MERGE_CHUNK].astype(F32))

    cy = cy_ref[...]
    zs = zs_ref[...]
    at = at_ref[...]
    for c in range(D_MODEL // MERGE_CHUNK):
        lo = c * MERGE_CHUNK
        cols = slice(lo, lo + MERGE_CHUNK)
        a = _dot(cy, wco_ref[:, cols])
        bm = _dot(zs, wsg_ref[:, cols]) * _sigmoid(_dot(zs, wsg_ref[:, D_MODEL + lo:D_MODEL + lo + MERGE_CHUNK]))
        cc = _dot(at, wat_ref[:, cols])
        merged = gate(0, lo) * a + gate(1, lo) * bm + gate(2, lo) * cc
        mg_scr[:, cols] = merged.astype(BF16)
    mo = _dot(mg_scr[...], wo_ref[...])
    x1 = x_ref[...] + gm_ref[...] * _rms(mo, gpost_ref[...])
    x1_ref[...] = x1
    hf_ref[...] = (_rms(x1, gpre_ref[...]) * (1.0 + scf_ref[...]) + shf_ref[...]).astype(BF16)


def _merge(cy, zs, at, z2, x2, mod_arr, mod_spec, w, layer, tm, host=None):
    rows = x2.shape[0]

    def rowspec(width):
        return pl.BlockSpec((tm, width), lambda i: (i, 0))

    n_host = 0 if host is None else len(host.inputs)
    n_gate = 3 * D_MODEL // GATE_BLOCK
    gate_specs = [pl.BlockSpec((tm, GATE_BLOCK), lambda i, k=k: (i, OFF_GATES // GATE_BLOCK + k))
                  for k in range(n_gate)]
    outs = pl.pallas_call(
        functools.partial(_merge_kernel, n_host=n_host),
        grid=(rows // tm,),
        in_specs=[rowspec(D_CONV), rowspec(D_SSM), rowspec(D_ATT_OUT)] + gate_specs
        + [rowspec(D_MODEL), mod_spec(2), mod_spec(3), mod_spec(4),
                  _resident((D_CONV, D_MODEL), layer), _resident((D_SSM, 2 * D_MODEL), layer),
                  _resident((D_ATT_OUT, D_MODEL), layer), _resident((D_MODEL, D_MODEL), layer),
                  _resident((1, D_MODEL), layer), _resident((1, D_MODEL), layer)]
        + ([] if host is None else host.in_specs),
        out_specs=[rowspec(D_MODEL), rowspec(D_MODEL)] + ([] if host is None else [host.out_spec]),
        out_shape=[jax.ShapeDtypeStruct((rows, D_MODEL), F32),
                   jax.ShapeDtypeStruct((rows, D_MODEL), BF16)]
        + ([] if host is None else [host.out_shape]),
        scratch_shapes=[pltpu.VMEM((tm, D_MODEL), BF16)],
        compiler_params=_params(("parallel",)),
        name="merge",
    )(cy, zs, at, *([z2] * n_gate), x2, mod_arr, mod_arr, mod_arr, w["w_conv_out"], w["w_ssm_glu"], w["w_att"],
      w["w_out"], w["g_post_mix"], w["g_pre_ffn"], *([] if host is None else host.inputs))
    return outs


FF_CHUNK = 256


def _ffn_kernel(*refs, n_host):
    hf_ref, x1_ref, gf_ref, w1_ref, w2_ref, gpost_ref = refs[:6]
    o_ref = refs[6 + n_host]
    acc_ref = refs[-1]
    if n_host:
        for run in _att_sample_stages(*refs[6:6 + n_host], refs[7 + n_host]):
            run()
    hf = hf_ref[...]
    for c in range(D_FF // FF_CHUNK):
        lo = c * FF_CHUNK
        a = _dot(hf, w1_ref[:, lo:lo + FF_CHUNK])
        b = _dot(hf, w1_ref[:, D_FF + lo:D_FF + lo + FF_CHUNK])
        f = (_silu(a) * b).astype(BF16)
        part = _dot(f, w2_ref[lo:lo + FF_CHUNK, :])
        if c == 0:
            acc_ref[...] = part
        else:
            acc_ref[...] += part
    o_ref[...] = x1_ref[...] + gf_ref[...] * _rms(acc_ref[...], gpost_ref[...])


def _ffn(hf, x1, mod_arr, mod_spec, w, layer, tm, host=None):
    rows = x1.shape[0]
    n_host = 0 if host is None else len(host.inputs)
    return pl.pallas_call(
        functools.partial(_ffn_kernel, n_host=n_host),
        grid=(rows // tm,),
        in_specs=[pl.BlockSpec((tm, D_MODEL), lambda i: (i, 0)),
                  pl.BlockSpec((tm, D_MODEL), lambda i: (i, 0)),
                  mod_spec(5),
                  _resident((D_MODEL, 2 * D_FF), layer),
                  _resident((D_FF, D_MODEL), layer),
                  _resident((1, D_MODEL), layer)] + ([] if host is None else host.in_specs),
        out_specs=[pl.BlockSpec((tm, D_MODEL), lambda i: (i, 0))]
        + ([] if host is None else [host.out_spec]),
        out_shape=[jax.ShapeDtypeStruct((rows, D_MODEL), F32)]
        + ([] if host is None else [host.out_shape]),
        scratch_shapes=[pltpu.VMEM((tm, D_MODEL), F32)],
        compiler_params=_params(("parallel",)),
        name="ffn",
    )(hf, x1, mod_arr, w["w_ffn_in"], w["w_ffn_out"], w["g_post_ffn"],
      *([] if host is None else host.inputs))


ROW_TILE = 512


def kernel(x_prompt, x_sample, cache_k0, cache_v0, cache_k1, cache_v1, cache_k2, cache_v2, state_conv, state_ssm_re, state_ssm_im, c_prompt, c_sample, w_mod, b_mod, g_pre_mix, g_post_mix, g_pre_ffn, g_post_ffn, w_in, conv_w, conv_b, conv_ln_g, conv_ln_b, w_conv_out, ssm_a_re, ssm_a_im, ssm_log_dt, ssm_b_re, ssm_b_im, ssm_c_re, ssm_c_im, ssm_d, w_ssm_glu, w_att, w_out, w_ffn_in, w_ffn_out):
    nbp, t_p, _ = x_prompt.shape
    nbs, t_s, _ = x_sample.shape
    caches_k = (cache_k0, cache_k1, cache_k2)
    caches_v = (cache_v0, cache_v1, cache_v2)

    tm = ROW_TILE
    rows_p, rows_s = nbp * t_p, nbs * t_s
    per_seq = t_p // tm
    assert rows_s == tm

    mod_p, mod_s = _modulation(c_prompt, jnp.tile(c_sample, (t_s, 1)), w_mod, b_mod)
    mod_p = mod_p.reshape(DEPTH, nbp, 1, 6 * D_MODEL)
    abr, abi, bbr, bbi = _ssm_discretise(ssm_a_re, ssm_a_im, ssm_log_dt, ssm_b_re, ssm_b_im)

    w = dict(
        g_pre_mix=g_pre_mix[:, None], g_post_mix=g_post_mix[:, None],
        g_pre_ffn=g_pre_ffn[:, None], g_post_ffn=g_post_ffn[:, None],
        w_in=w_in.astype(BF16),
        conv_w=jnp.broadcast_to(conv_w[:, :, None, :], (DEPTH, CONV_WIDTH, SUBLANES, D_CONV)),
        conv_b=conv_b[:, None], conv_ln_g=conv_ln_g[:, None], conv_ln_b=conv_ln_b[:, None],
        ssm_d=ssm_d.reshape(DEPTH, 1, D_SSM),
        w_conv_out=w_conv_out.astype(BF16), w_ssm_glu=w_ssm_glu.astype(BF16),
        w_att=w_att.astype(BF16), w_out=w_out.astype(BF16),
        w_ffn_in=w_ffn_in.astype(BF16), w_ffn_out=w_ffn_out.astype(BF16))
    hist_s = jnp.transpose(state_conv, (0, 2, 1, 3))
    h0r_s = state_ssm_re.reshape(DEPTH, nbs, N_STATE)
    h0i_s = state_ssm_im.reshape(DEPTH, nbs, N_STATE)
    prev_p = jnp.zeros((nbp, HALO, D_CONV), F32)
    h0_p = jnp.zeros((nbp, N_STATE), F32)
    kts = [_cache_t(c) for c in caches_k]
    vts = [_cache_t(c) for c in caches_v]
    att_consts = _att_sample_consts(t_s)
    n_host_steps = rows_p // tm
    seqs_per_host = n_host_steps * ATT_S_BB
    assert 2 * seqs_per_host == nbs

    xp = x_prompt.reshape(rows_p, D_MODEL)
    xs = jnp.transpose(x_sample, (1, 0, 2)).reshape(rows_s, D_MODEL)
    prompt_states, sample_states = [], []
    kvt_p = None
    kvt_s = None
    hist_out = None
    for l in range(DEPTH):
        a8, bw, cw = _ssm_pack(abr[l], abi[l], bbr[l], bbi[l], ssm_c_re[l], ssm_c_im[l])

        def spec_p(c, l=l):
            return pl.BlockSpec((None, None, 1, D_MODEL), lambda i: (l, i // per_seq, 0, c))

        def spec_s(c, l=l):
            return pl.BlockSpec((None, tm, D_MODEL), lambda i: (l, i, c))

        z_s, kvt_s = _in_proj_sample(xs, mod_s, spec_s, w["g_pre_mix"], w["w_in"], nbs, t_s, l, kvt_s)
        cy_s, hist_out = _conv_sample(z_s.reshape(t_s, nbs, N_IN), hist_s, w["conv_w"], w["conv_b"],
                                      w["conv_ln_g"], w["conv_ln_b"], l, hist_out)
        zs_s, sre_s, sim_s = _ssm_branch(z_s, h0r_s, h0i_s, l, a8, bw, cw, w["ssm_d"], l,
                                         nb=nbs, t_len=t_s, tc=t_s, flat=True)
        qkv_s = z_s[:, OFF_Q:OFF_Q + 3 * D_ATT].astype(F32).reshape(t_s, nbs, 3 * D_ATT)
        qkv_s = jnp.pad(jnp.transpose(qkv_s, (1, 0, 2)), ((0, 0), (0, QROWS - t_s), (0, 0)))
        qkv_s = qkv_s.reshape(nbs * QROWS, 3 * D_ATT)

        z_p, kvt_p = _in_proj_prompt(xp, mod_p, spec_p, w["g_pre_mix"], w["w_in"], tm,
                                     nbp, t_p, l, kvt_p)
        z3 = z_p.reshape(nbp, t_p, N_IN)
        cy_p, conv_p = _conv_branch(z3, prev_p, w["conv_w"], w["conv_b"], w["conv_ln_g"],
                                    w["conv_ln_b"], l, bb=1, tq=512)
        zs_p, sre_p, sim_p = _ssm_branch(z3, h0_p, h0_p, None, a8, bw, cw, w["ssm_d"], l,
                                         nb=nbp, t_len=t_p, tc=128, flat=False)
        att_p = _att_prompt(z3)
        host_a = _AttSampleHost(qkv_s, kts, vts, att_consts, l, 0, n_host_steps)
        host_b = _AttSampleHost(qkv_s, kts, vts, att_consts, l, seqs_per_host, n_host_steps)
        x1_p, hf_p, att_a = _merge(cy_p.reshape(rows_p, D_CONV), zs_p.reshape(rows_p, D_SSM),
                                   att_p.reshape(rows_p, D_ATT_OUT), z_p, xp, mod_p, spec_p, w, l, tm,
                                   host=host_a)
        xp, att_b = _ffn(hf_p, x1_p, mod_p, spec_p, w, l, tm, host=host_b)

        att_s = jnp.concatenate([att_a, att_b], axis=0).reshape(nbs, QROWS, D_ATT_OUT)[:, :t_s]
        att_s = jnp.transpose(att_s, (1, 0, 2)).reshape(rows_s, D_ATT_OUT)
        x1_s, hf_s = _merge(cy_s.reshape(rows_s, D_CONV), zs_s, att_s, z_s, xs, mod_s, spec_s, w, l, tm)
        xs, = _ffn(hf_s, x1_s, mod_s, spec_s, w, l, tm)

        prompt_states.append((conv_p, sre_p.reshape(nbp, N_SSM_GROUPS, SSM_STATE),
                              sim_p.reshape(nbp, N_SSM_GROUPS, SSM_STATE)))
        sample_states.append((sre_s.reshape(nbs, N_SSM_GROUPS, SSM_STATE),
                              sim_s.reshape(nbs, N_SSM_GROUPS, SSM_STATE)))

    ps = [jnp.transpose(a.reshape(DEPTH, nbp, HEADS_PER_GROUP, HEAD_DIM, a.shape[-1]), (0, 1, 4, 2, 3))
          for a in kvt_p]
    ps += [jnp.stack([s[i] for s in prompt_states]) for i in range(3)]
    ss = [jnp.transpose(a.reshape(DEPTH, t_s, HEADS_PER_GROUP, HEAD_DIM, nbs), (0, 4, 1, 2, 3))
          for a in kvt_s]
    ss.append(jnp.transpose(hist_out, (0, 2, 1, 3)))
    ss += [jnp.stack([s[i] for s in sample_states]) for i in range(2)]
    xs = jnp.transpose(xs.reshape(t_s, nbs, D_MODEL), (1, 0, 2))
    return (xp.reshape(nbp, t_p, D_MODEL), xs, *ps, *ss)
```
